```python
import jax, jax.numpy as jnp
from jax import lax
import numpy as np

D_MODEL = 1024
BATCH = 4
SEQ = 4096
DEPTH = 1
DEC_BATCH = 128
DEC_SEQ = 4
PAST_LEN = 2048
PAGE_SIZE = 128

D_CONV = 512
CONV_WIDTH = 3
N_HEADS = 8
HEAD_DIM = 64
D_ATTN = N_HEADS * HEAD_DIM
Q_BLOCK = 128
N_GROUPS = 4
EXPERTS_PER_GROUP = 4
N_EXPERTS = N_GROUPS * EXPERTS_PER_GROUP
TOP_K = 2
D_EXPERT = 512
LN_EPS = 1e-5
DEEPNORM_ALPHA = (2 * DEPTH) ** 0.25
DEEPNORM_BETA = (8 * DEPTH) ** -0.25
N_IN = 3 * D_CONV + 3 * D_ATTN + N_HEADS + 2 * D_MODEL
SPLIT_POINTS = (D_CONV, 2 * D_CONV, 3 * D_CONV,
                3 * D_CONV + D_ATTN, 3 * D_CONV + 2 * D_ATTN, 3 * D_CONV + 3 * D_ATTN,
                3 * D_CONV + 3 * D_ATTN + N_HEADS, 3 * D_CONV + 3 * D_ATTN + N_HEADS + D_MODEL)

kernel_name = 'hybrid_conv_fox_hmoe_decode_step'


def layer_norm(x, g, b):
    xf = x.astype(jnp.float32)
    mu = jnp.mean(xf, axis=-1, keepdims=True)
    var = jnp.mean(jnp.square(xf - mu), axis=-1, keepdims=True)
    y = (xf - mu) * lax.rsqrt(var + LN_EPS) * g.astype(jnp.float32) + b.astype(jnp.float32)
    return y.astype(x.dtype)


def short_conv(u_ext, conv_w):
    L = u_ext.shape[1] - (CONV_WIDTH - 1)
    out = conv_w[0] * u_ext[:, 0:L]
    for j in range(1, CONV_WIDTH):
        out = out + conv_w[j] * u_ext[:, j:j + L]
    return out


def fox_attention(q, k, v, logf):
    N, Sq = q.shape[0], q.shape[1]
    Sk = k.shape[1]
    offset = Sk - Sq
    c = jnp.cumsum(logf.astype(jnp.float32), axis=1)
    ck = jnp.transpose(c, (0, 2, 1))
    cq = c[:, offset:]
    kf = k.astype(jnp.float32)
    vf = v.astype(jnp.float32)
    kpos = jnp.arange(Sk)
    qb = min(Q_BLOCK, Sq)
    nb = Sq // qb
    q_blocks = jnp.swapaxes(q.reshape(N, nb, qb, N_HEADS, HEAD_DIM), 0, 1)
    c_blocks = jnp.swapaxes(cq.reshape(N, nb, qb, N_HEADS), 0, 1)
    pos_blocks = (offset + jnp.arange(Sq)).reshape(nb, qb)
    scale = HEAD_DIM ** -0.5

    def block(args):
        qblk, cblk, qpos = args
        s = jnp.einsum('nqhd,nkhd->nhqk', qblk.astype(jnp.float32), kf) * scale
        s = s + jnp.transpose(cblk, (0, 2, 1))[..., None] - ck[:, :, None, :]
        s = jnp.where(kpos[None, :] <= qpos[:, None], s, -1e30)
        p = jax.nn.softmax(s, axis=-1)
        return jnp.einsum('nhqk,nkhd->nqhd', p, vf)

    o = lax.map(block, (q_blocks, c_blocks, pos_blocks))
    return jnp.swapaxes(o, 0, 1).reshape(N, Sq, D_ATTN).astype(q.dtype)


def hier_moe(x, w_group, w_router, w1, w3, w2):
    N, S, D = x.shape
    t = x.reshape(N * S, D)
    gl = (t @ w_group).astype(jnp.float32)
    gp = jax.nn.softmax(gl, axis=-1)
    p_g, g_idx = lax.top_k(gp, 1)
    el = (t @ w_router).astype(jnp.float32).reshape(-1, N_GROUPS, EXPERTS_PER_GROUP)
    el_sel = jnp.take_along_axis(el, g_idx[:, :, None], axis=1)[:, 0]
    ep = jax.nn.softmax(el_sel, axis=-1)
    top_w, top_i = lax.top_k(ep, TOP_K)
    top_w = top_w / jnp.sum(top_w, axis=-1, keepdims=True)
    wts = p_g * top_w
    ids = g_idx * EXPERTS_PER_GROUP + top_i
    gate = jnp.sum(jax.nn.one_hot(ids, N_EXPERTS, dtype=jnp.float32) * wts[..., None], axis=1)
    gate = gate.astype(t.dtype)
    y = jnp.zeros_like(t)
    for e in range(N_EXPERTS):
        h = jax.nn.silu(t @ w1[e]) * (t @ w3[e])
        y = y + gate[:, e:e + 1] * (h @ w2[e])
    return y.reshape(N, S, D)


def layer(x, conv_buf, k_past, v_past, logf_past,
          w_in, b_f, conv_w, w_conv_out, w_attn_out, w_o, ln1_g, ln1_b,
          w_group, w_router, w1, w3, w2, ln2_g, ln2_b):
    N, S, _ = x.shape
    h = x @ w_in
    xc, bg, cg, q, k, v, fg, ga, gb = jnp.split(h, SPLIT_POINTS, axis=-1)
    u = cg * xc
    u_ext = jnp.concatenate([conv_buf.astype(u.dtype), u], axis=1)
    y_a = (bg * short_conv(u_ext, conv_w)) @ w_conv_out
    conv_new = u_ext[:, -(CONV_WIDTH - 1):]
    q = q.reshape(N, S, N_HEADS, HEAD_DIM)
    k = k.reshape(N, S, N_HEADS, HEAD_DIM)
    v = v.reshape(N, S, N_HEADS, HEAD_DIM)
    logf = jax.nn.log_sigmoid(fg.astype(jnp.float32) + b_f.astype(jnp.float32))
    k_all = jnp.concatenate([k_past.astype(k.dtype), k], axis=1)
    v_all = jnp.concatenate([v_past.astype(v.dtype), v], axis=1)
    logf_all = jnp.concatenate([logf_past.astype(jnp.float32), logf], axis=1)
    y_b = fox_attention(q, k_all, v_all, logf_all) @ w_attn_out
    mix = (jax.nn.sigmoid(ga) * y_a + jax.nn.sigmoid(gb) * y_b) @ w_o
    x = layer_norm(DEEPNORM_ALPHA * x + mix, ln1_g, ln1_b)
    x = layer_norm(DEEPNORM_ALPHA * x + hier_moe(x, w_group, w_router, w1, w3, w2), ln2_g, ln2_b)
    return x, k, v, logf, conv_new


def setup_inputs(seed: int = 0) -> dict:
    key = jax.random.key(seed)
    ks = jax.random.split(key, 24)
    n_pages = PAST_LEN // PAGE_SIZE
    n_used = DEC_BATCH * n_pages
    n_pool = n_used + (n_used + 3) // 4
    f32 = jnp.float32
    nrm = lambda k, shape, s: jax.random.normal(k, shape, f32) * s
    page_table = jax.random.permutation(ks[0], n_pool)[:n_used].reshape(DEC_BATCH, n_pages).astype(jnp.int32)
    return {
        'x_prompt': nrm(ks[1], (BATCH, SEQ, D_MODEL), 1.0),
        'x_sample': nrm(ks[2], (DEC_BATCH, DEC_SEQ, D_MODEL), 1.0),
        'cache_k': nrm(ks[3], (DEPTH, n_pool, PAGE_SIZE, N_HEADS, HEAD_DIM), 1.0),
        'cache_v': nrm(ks[4], (DEPTH, n_pool, PAGE_SIZE, N_HEADS, HEAD_DIM), 1.0),
        'cache_logf': jax.nn.log_sigmoid(3.0 + nrm(ks[5], (DEPTH, n_pool, PAGE_SIZE, N_HEADS), 1.0)),
        'state_conv': nrm(ks[6], (DEPTH, DEC_BATCH, CONV_WIDTH - 1, D_CONV), 0.5),
        'page_table': page_table,
        'w_in': nrm(ks[7], (DEPTH, D_MODEL, N_IN), D_MODEL ** -0.5),
        'b_f': 3.0 + nrm(ks[8], (DEPTH, N_HEADS), 0.5),
        'conv_w': nrm(ks[9], (DEPTH, CONV_WIDTH, D_CONV), 0.5),
        'w_conv_out': nrm(ks[10], (DEPTH, D_CONV, D_MODEL), D_CONV ** -0.5),
        'w_attn_out': nrm(ks[11], (DEPTH, D_ATTN, D_MODEL), D_ATTN ** -0.5),
        'w_o': nrm(ks[12], (DEPTH, D_MODEL, D_MODEL), DEEPNORM_BETA * D_MODEL ** -0.5),
        'ln1_g': 1.0 + nrm(ks[13], (DEPTH, D_MODEL), 0.05),
        'ln1_b': nrm(ks[14], (DEPTH, D_MODEL), 0.02),
        'w_group': nrm(ks[15], (DEPTH, D_MODEL, N_GROUPS), D_MODEL ** -0.5),
        'w_router': nrm(ks[16], (DEPTH, D_MODEL, N_EXPERTS), D_MODEL ** -0.5),
        'w1': nrm(ks[17], (DEPTH, N_EXPERTS, D_MODEL, D_EXPERT), D_MODEL ** -0.5),
        'w3': nrm(ks[18], (DEPTH, N_EXPERTS, D_MODEL, D_EXPERT), D_MODEL ** -0.5),
        'w2': nrm(ks[19], (DEPTH, N_EXPERTS, D_EXPERT, D_MODEL), DEEPNORM_BETA * D_EXPERT ** -0.5),
        'ln2_g': 1.0 + nrm(ks[20], (DEPTH, D_MODEL), 0.05),
        'ln2_b': nrm(ks[21], (DEPTH, D_MODEL), 0.02),
    }


def reference(x_prompt, x_sample, cache_k, cache_v, cache_logf, state_conv, page_table,
              w_in, b_f, conv_w, w_conv_out, w_attn_out, w_o, ln1_g, ln1_b,
              w_group, w_router, w1, w3, w2, ln2_g, ln2_b):
    n_pages = page_table.shape[1]
    past_len = n_pages * PAGE_SIZE
    xp, xs = x_prompt, x_sample
    kp_l, vp_l, fp_l, cp_l, ks_l, vs_l, fs_l, cs_l = [], [], [], [], [], [], [], []
    for l in range(DEPTH):
        params = (w_in[l], b_f[l], conv_w[l], w_conv_out[l], w_attn_out[l], w_o[l], ln1_g[l], ln1_b[l],
                  w_group[l], w_router[l], w1[l], w3[l], w2[l], ln2_g[l], ln2_b[l])
        nb = xp.shape[0]
        xp, kp, vp, fp, cp = layer(
            xp, jnp.zeros((nb, CONV_WIDTH - 1, D_CONV), xp.dtype),
            jnp.zeros((nb, 0, N_HEADS, HEAD_DIM), xp.dtype), jnp.zeros((nb, 0, N_HEADS, HEAD_DIM), xp.dtype),
            jnp.zeros((nb, 0, N_HEADS), jnp.float32), *params)
        db = xs.shape[0]
        k_past = cache_k[l][page_table].reshape(db, past_len, N_HEADS, HEAD_DIM)
        v_past = cache_v[l][page_table].reshape(db, past_len, N_HEADS, HEAD_DIM)
        f_past = cache_logf[l][page_table].reshape(db, past_len, N_HEADS)
        xs, ksn, vsn, fsn, csn = layer(xs, state_conv[l], k_past, v_past, f_past, *params)
        kp_l.append(kp); vp_l.append(vp); fp_l.append(fp); cp_l.append(cp)
        ks_l.append(ksn); vs_l.append(vsn); fs_l.append(fsn); cs_l.append(csn)
    return (xp, xs,
            jnp.stack(kp_l), jnp.stack(vp_l), jnp.stack(fp_l), jnp.stack(cp_l),
            jnp.stack(ks_l), jnp.stack(vs_l), jnp.stack(fs_l), jnp.stack(cs_l))
```

```python
import functools

import jax
import jax.numpy as jnp
from jax import lax
from jax.experimental import pallas as pl
from jax.experimental.pallas import tpu as pltpu

F32 = jnp.float32
BF16 = jnp.bfloat16

LANES = 128
SUBLANES = 8
VMEM_LIMIT_BYTES = 56 * 1024 * 1024

CONV_WIDTH = 3
N_HEADS = 8
HEAD_DIM = 64
D_ATTN = N_HEADS * HEAD_DIM
N_GROUPS = 4
EXPERTS_PER_GROUP = 4
N_EXPERTS = N_GROUPS * EXPERTS_PER_GROUP
LN_EPS = 1e-5
PAGE_SIZE = 128
MASK_VALUE = -1e30
AUG_Q_C = HEAD_DIM
AUG_SPLITS = 3


def _dot(a, b):
    return jnp.dot(a, b, preferred_element_type=F32)


def _dot_nt(a, b):
    return lax.dot_general(a, b, (((1,), (1,)), ((), ())), preferred_element_type=F32)


def _split3(x):
    h1 = x.astype(BF16)
    r1 = x - h1.astype(F32)
    h2 = r1.astype(BF16)
    r2 = r1 - h2.astype(F32)
    h3 = r2.astype(BF16)
    return h1, h2, h3


def _dot_exact_rhs(a_bf16, x):
    h1, h2, h3 = _split3(x)
    return _dot(a_bf16, h1) + _dot(a_bf16, h2) + _dot(a_bf16, h3)


def _dot_nt_exact_rhs(a_bf16, x):
    h1, h2, h3 = _split3(x)
    return _dot_nt(a_bf16, h1) + _dot_nt(a_bf16, h2) + _dot_nt(a_bf16, h3)


def _log_sigmoid(x):
    return jnp.minimum(x, 0.0) - jnp.log1p(jnp.exp(-jnp.abs(x)))


def _layer_norm(r, g, b):
    mu = jnp.mean(r, axis=-1, keepdims=True)
    d = r - mu
    var = jnp.mean(d * d, axis=-1, keepdims=True)
    return d * lax.rsqrt(var + LN_EPS) * g + b


def _const_spec(shape):
    nd = len(shape)
    return pl.BlockSpec(shape, lambda *_: (0,) * nd, pipeline_mode=pl.Buffered(1))


def _inproj_kernel(*refs, rows, tiles_per_seq, sample, d_conv, d_model):
    if sample:
        (x_ref, wm_ref, wq_ref, wf_ref, wg_ref, bf_ref, cw_ref, wco_ref, s0_ref, s1_ref,
         q_ref, k_ref, v_ref, lf_ref, ap_ref, sgb_ref, u_ref, ubuf) = refs
    else:
        (x_ref, wm_ref, wq_ref, wf_ref, wg_ref, bf_ref, cw_ref, wco_ref, tri_ref, sel_ref, pq_ref, pk_ref,
         oq_ref, ok_ref,
         q_ref, k_ref, v_ref, ka_ref, vb_ref, lf_ref, ap_ref, sgb_ref, tail_ref, ubuf, ccarry) = refs
    i = pl.program_id(0)
    xb = x_ref[...].astype(BF16)

    xc = _dot(xb, wm_ref[:, 0:d_conv])
    bg = _dot(xb, wm_ref[:, d_conv:2 * d_conv])
    cg = _dot(xb, wm_ref[:, 2 * d_conv:3 * d_conv])
    u = cg * xc
    if sample:
        ubuf[0:SUBLANES, :] = jnp.zeros((SUBLANES, d_conv), F32)
    else:
        @pl.when(i % tiles_per_seq == 0)
        def _():
            ubuf[0:SUBLANES, :] = jnp.zeros((SUBLANES, d_conv), F32)
    ubuf[SUBLANES:SUBLANES + rows, :] = u
    prev1 = ubuf[SUBLANES - 1:SUBLANES - 1 + rows, :]
    prev2 = ubuf[SUBLANES - 2:SUBLANES - 2 + rows, :]
    if sample:
        pos = lax.broadcasted_iota(jnp.int32, (rows, 1), 0) % 4
        prev1 = jnp.where(pos == 0, s1_ref[...], prev1)
        prev2 = jnp.where(pos == 0, s0_ref[...], jnp.where(pos == 1, s1_ref[...], prev2))
        u_ref[...] = u
    else:
        ubuf[0:SUBLANES, :] = u[rows - SUBLANES:rows, :]

        @pl.when(i % tiles_per_seq == tiles_per_seq - 1)
        def _():
            tail_ref[0] = u[rows - SUBLANES:rows, :]
    conv = cw_ref[0:1, :] * prev2 + cw_ref[1:2, :] * prev1 + cw_ref[2:3, :] * u
    ya = _dot((bg * conv).astype(BF16), wco_ref[...])

    k = _dot(xb, wm_ref[:, 3 * d_conv:3 * d_conv + D_ATTN])
    v = _dot(xb, wm_ref[:, 3 * d_conv + D_ATTN:3 * d_conv + 2 * D_ATTN])
    k_ref[...] = k
    v_ref[...] = v
    qa = _dot(xb, wq_ref[...]) * (HEAD_DIM ** -0.5)
    fg = _dot(xb, wf_ref[...]) + bf_ref[...]
    lane = lax.broadcasted_iota(jnp.int32, (rows, LANES), 1)
    logf = jnp.where(lane < N_HEADS, _log_sigmoid(fg), 0.0)
    lf_ref[...] = logf[:, 0:N_HEADS]
    if sample:
        q_ref[...] = qa.astype(BF16)
    else:
        vb_ref[...] = v.astype(BF16)

        @pl.when(i % tiles_per_seq == 0)
        def _():
            ccarry[...] = jnp.zeros((1, LANES), F32)
        c = _dot_exact_rhs(tri_ref[...], logf) + ccarry[...]
        ccarry[...] = c[rows - 1:rows, :]
        c1, c2, c3 = _split3(c)
        c_parts = jnp.concatenate([c1, c2, c3], axis=1)
        q_ref[...] = (qa + _dot(c_parts, pq_ref[...]) + oq_ref[...]).astype(BF16)
        ka = _dot(k.astype(BF16), sel_ref[...]) - _dot(c_parts, pk_ref[...]) + ok_ref[...]
        ka_ref[...] = ka.astype(BF16)

    ga = _dot(xb, wg_ref[:, 0:d_model])
    gb = _dot(xb, wg_ref[:, d_model:2 * d_model])
    ap_ref[...] = jax.nn.sigmoid(ga) * ya
    sgb_ref[...] = jax.nn.sigmoid(gb)


def _inproj(x, w, *, sample, seq_len, state=None):
    t, d_model = x.shape
    d_conv = w['wco'].shape[0]
    rows = 256
    tiles_per_seq = max(seq_len // rows, 1)
    n_tiles = t // rows
    row_spec = lambda width: pl.BlockSpec((rows, width), lambda i: (i, 0))
    weights = [w['wm'], w['wq'], w['wf'], w['wg'], w['bf'], w['cw'], w['wco']]
    in_specs = [row_spec(d_model)] + [_const_spec(a.shape) for a in weights]
    args = [x] + weights
    if sample:
        args += [state[0], state[1]]
        in_specs += [row_spec(d_conv), row_spec(d_conv)]
        out_shape = [
            jax.ShapeDtypeStruct((t, N_HEADS * LANES), BF16),
            jax.ShapeDtypeStruct((t, D_ATTN), F32),
            jax.ShapeDtypeStruct((t, D_ATTN), F32),
            jax.ShapeDtypeStruct((t, N_HEADS), F32),
            jax.ShapeDtypeStruct((t, d_model), F32),
            jax.ShapeDtypeStruct((t, d_model), F32),
            jax.ShapeDtypeStruct((t, d_conv), F32),
        ]
        out_specs = [row_spec(N_HEADS * LANES), row_spec(D_ATTN), row_spec(D_ATTN), row_spec(N_HEADS),
                     row_spec(d_model), row_spec(d_model), row_spec(d_conv)]
        scratch = [pltpu.VMEM((rows + SUBLANES, d_conv), F32)]
    else:
        consts = [w['tri'], w['sel'], w['pq'], w['pk'], w['oq'], w['ok']]
        args += consts
        in_specs += [_const_spec(a.shape) for a in consts]
        n_seq = t // seq_len
        out_shape = [
            jax.ShapeDtypeStruct((t, N_HEADS * LANES), BF16),
            jax.ShapeDtypeStruct((t, D_ATTN), F32),
            jax.ShapeDtypeStruct((t, D_ATTN), F32),
            jax.ShapeDtypeStruct((t, N_HEADS * LANES), BF16),
            jax.ShapeDtypeStruct((t, D_ATTN), BF16),
            jax.ShapeDtypeStruct((t, N_HEADS), F32),
            jax.ShapeDtypeStruct((t, d_model), F32),
            jax.ShapeDtypeStruct((t, d_model), F32),
            jax.ShapeDtypeStruct((n_seq, SUBLANES, d_conv), F32),
        ]
        out_specs = [row_spec(N_HEADS * LANES), row_spec(D_ATTN), row_spec(D_ATTN), row_spec(N_HEADS * LANES),
                     row_spec(D_ATTN), row_spec(N_HEADS), row_spec(d_model), row_spec(d_model),
                     pl.BlockSpec((1, SUBLANES, d_conv), lambda i: (i // tiles_per_seq, 0, 0))]
        scratch = [pltpu.VMEM((rows + SUBLANES, d_conv), F32), pltpu.VMEM((1, LANES), F32)]
    kern = functools.partial(_inproj_kernel, rows=rows, tiles_per_seq=tiles_per_seq, sample=sample,
                             d_conv=d_conv, d_model=d_model)
    return pl.pallas_call(
        kern,
        grid=(n_tiles,),
        in_specs=in_specs,
        out_specs=out_specs,
        out_shape=out_shape,
        scratch_shapes=scratch,
        compiler_params=pltpu.CompilerParams(dimension_semantics=("arbitrary",),
                                             vmem_limit_bytes=VMEM_LIMIT_BYTES),
        name="inproj_sample" if sample else "inproj_prompt",
    )(*args)


def _attn_kernel(q_ref, k_ref, v_ref, o_ref, m_sc, l_sc, acc_sc, *, tq):
    i = pl.program_id(2)
    for hl in range(2):
        m_sc[hl] = jnp.full((tq, 1), MASK_VALUE, F32)
        l_sc[hl] = jnp.zeros((tq, 1), F32)
        acc_sc[hl] = jnp.zeros((tq, LANES), F32)

    def step(j, masked):
        start = pl.multiple_of(j * tq, tq)
        k2 = k_ref[pl.ds(start, tq), :]
        v2 = v_ref[pl.ds(start, tq), :]
        for hl in range(2):
            s = _dot_nt(q_ref[:, hl * LANES:(hl + 1) * LANES], k2[:, hl * LANES:(hl + 1) * LANES])
            if masked:
                row = lax.broadcasted_iota(jnp.int32, (tq, tq), 0)
                col = lax.broadcasted_iota(jnp.int32, (tq, tq), 1)
                s = jnp.where(col <= row, s, MASK_VALUE)
            m_prev = m_sc[hl]
            m_new = jnp.maximum(m_prev, jnp.max(s, axis=1, keepdims=True))
            alpha = jnp.exp(m_prev - m_new)
            p = jnp.exp(s - m_new)
            l_sc[hl] = alpha * l_sc[hl] + jnp.sum(p, axis=1, keepdims=True)
            acc_sc[hl] = alpha * acc_sc[hl] + _dot(p.astype(BF16), v2)
            m_sc[hl] = m_new

    def body(j, carry):
        step(j, False)
        return carry

    lax.fori_loop(0, i, body, 0)
    step(i, True)
    lane = lax.broadcasted_iota(jnp.int32, (tq, LANES), 1)
    o = jnp.where(lane < HEAD_DIM, acc_sc[0] / l_sc[0], acc_sc[1] / l_sc[1])
    o_ref[...] = o.astype(BF16)


def _prompt_attention(qa, ka, vb, *, n_seq, seq_len):
    t = qa.shape[0]
    tq = 512
    qt = seq_len // tq
    n_pairs = N_HEADS // 2
    return pl.pallas_call(
        functools.partial(_attn_kernel, tq=tq),
        grid=(n_seq, n_pairs, qt),
        in_specs=[
            pl.BlockSpec((tq, 2 * LANES), lambda n, hp, i: (n * qt + i, hp)),
            pl.BlockSpec((seq_len, 2 * LANES), lambda n, hp, i: (n, hp)),
            pl.BlockSpec((seq_len, LANES), lambda n, hp, i: (n, hp)),
        ],
        out_specs=pl.BlockSpec((tq, LANES), lambda n, hp, i: (n * qt + i, hp)),
        out_shape=jax.ShapeDtypeStruct((t, D_ATTN), BF16),
        scratch_shapes=[pltpu.VMEM((2, tq, 1), F32), pltpu.VMEM((2, tq, 1), F32),
                        pltpu.VMEM((2, tq, LANES), F32)],
        compiler_params=pltpu.CompilerParams(dimension_semantics=("arbitrary", "arbitrary", "arbitrary"),
                                             vmem_limit_bytes=VMEM_LIMIT_BYTES),
        name="attn_prompt",
    )(qa, ka, vb)


def _sample_attn_kernel(pt_ref, *refs, pages_per_step, n_new):
    del pt_ref
    npg = pages_per_step
    q_ref = refs[0]
    k_refs = refs[1:1 + npg]
    v_refs = refs[1 + npg:1 + 2 * npg]
    f_refs = refs[1 + 2 * npg:1 + 3 * npg]
    kn_ref, vn_ref, fn_ref, e_ref, sui_ref, tri_ref = refs[1 + 3 * npg:7 + 3 * npg]
    o_ref = refs[7 + 3 * npg]
    m_sc, l_sc, acc_sc, cnq_sc, carry_sc, pad_sc, kn_sc, vn_sc = refs[8 + 3 * npg:]
    jj = pl.program_id(1)
    n_rows = N_HEADS * n_new
    qb = q_ref[0]
    e = e_ref[...]

    def online_update(s, vals):
        m_prev = m_sc[...]
        m_new = jnp.maximum(m_prev, jnp.max(s, axis=1, keepdims=True))
        alpha = jnp.exp(m_prev - m_new)
        p = jnp.exp(s - m_new)
        l_sc[...] = alpha * l_sc[...] + jnp.sum(p, axis=1, keepdims=True)
        acc_sc[...] = alpha * acc_sc[...] + _dot(p.astype(BF16), vals)
        m_sc[...] = m_new

    @pl.when(jj == 0)
    def _():
        m_sc[...] = jnp.full((n_rows, 1), MASK_VALUE, F32)
        l_sc[...] = jnp.zeros((n_rows, 1), F32)
        acc_sc[...] = jnp.zeros((n_rows, D_ATTN), F32)
        carry_sc[...] = jnp.zeros((1, LANES), F32)
        pad_sc[...] = jnp.zeros((PAGE_SIZE, LANES), F32)
        pad_sc[0:n_new, 0:N_HEADS] = fn_ref[0]
        kn_sc[...] = jnp.zeros((PAGE_SIZE, D_ATTN), F32)
        kn_sc[0:n_new, :] = kn_ref[0]
        vn_sc[...] = jnp.zeros((PAGE_SIZE, D_ATTN), F32)
        vn_sc[0:n_new, :] = vn_ref[0]
        cn = _dot_exact_rhs(tri_ref[...], pad_sc[...])
        cn_rows = _dot_nt_exact_rhs(e, cn)
        row_t = lax.broadcasted_iota(jnp.int32, (n_rows, LANES), 0) % n_new
        col = lax.broadcasted_iota(jnp.int32, (n_rows, LANES), 1)
        cnq = jnp.sum(jnp.where(col == row_t, cn_rows, 0.0), axis=1, keepdims=True)
        cnq_sc[...] = cnq
        s = _dot_nt(qb, kn_sc[...].astype(BF16)) + (cnq - cn_rows)
        s = jnp.where(col <= row_t, s, MASK_VALUE)
        online_update(s, vn_sc[...].astype(BF16))

    for p in range(npg):
        pad_sc[...] = jnp.zeros((PAGE_SIZE, LANES), F32)
        pad_sc[:, 0:N_HEADS] = f_refs[p][0]
        lf = pad_sc[...]
        incl = _dot_exact_rhs(sui_ref[...], lf)
        d = incl - lf + carry_sc[...]
        carry_sc[...] = carry_sc[...] + incl[0:1, :]
        bias = _dot_nt_exact_rhs(e, d) + cnq_sc[...]
        s = _dot_nt(qb, k_refs[p][0].astype(BF16)) + bias
        online_update(s, v_refs[p][0].astype(BF16))

    @pl.when(jj == pl.num_programs(1) - 1)
    def _():
        o_ref[0] = acc_sc[...] / l_sc[...]


def _sample_attention(qblk, cache_k, cache_v, cache_logf, page_table, k_new, v_new, lf_new):
    n_seq, n_rows, _ = qblk.shape
    n_new = k_new.shape[1]
    n_pages = page_table.shape[1]
    npg = 4
    steps = n_pages // npg
    pt = page_table.reshape(-1)
    row_head = jnp.arange(n_rows) // n_new
    e = (row_head[:, None] == jnp.arange(LANES)[None, :]).astype(BF16)
    pos = jnp.arange(PAGE_SIZE)
    sui = (pos[None, :] >= pos[:, None]).astype(BF16)
    tri = (pos[None, :] <= pos[:, None]).astype(BF16)

    def page_map(p):
        return lambda b, jj, pt_ref: (pt_ref[b * n_pages + n_pages - 1 - (jj * npg + p)], 0, 0)

    seq_map = lambda b, jj, pt_ref: (b, 0, 0)
    const_map = lambda b, jj, pt_ref: (0, 0)
    in_specs = [pl.BlockSpec((1, n_rows, D_ATTN), seq_map)]
    in_specs += [pl.BlockSpec((1, PAGE_SIZE, D_ATTN), page_map(p)) for p in range(npg)]
    in_specs += [pl.BlockSpec((1, PAGE_SIZE, D_ATTN), page_map(p)) for p in range(npg)]
    in_specs += [pl.BlockSpec((1, PAGE_SIZE, N_HEADS), page_map(p)) for p in range(npg)]
    in_specs += [pl.BlockSpec((1, n_new, D_ATTN), seq_map), pl.BlockSpec((1, n_new, D_ATTN), seq_map),
                 pl.BlockSpec((1, n_new, N_HEADS), seq_map),
                 pl.BlockSpec((n_rows, LANES), const_map), pl.BlockSpec((PAGE_SIZE, PAGE_SIZE), const_map),
                 pl.BlockSpec((PAGE_SIZE, PAGE_SIZE), const_map)]
    grid_spec = pltpu.PrefetchScalarGridSpec(
        num_scalar_prefetch=1,
        grid=(n_seq, steps),
        in_specs=in_specs,
        out_specs=pl.BlockSpec((1, n_rows, D_ATTN), seq_map),
        scratch_shapes=[pltpu.VMEM((n_rows, 1), F32), pltpu.VMEM((n_rows, 1), F32),
                        pltpu.VMEM((n_rows, D_ATTN), F32), pltpu.VMEM((n_rows, 1), F32),
                        pltpu.VMEM((1, LANES), F32), pltpu.VMEM((PAGE_SIZE, LANES), F32),
                        pltpu.VMEM((PAGE_SIZE, D_ATTN), F32), pltpu.VMEM((PAGE_SIZE, D_ATTN), F32)],
    )
    args = [pt, qblk] + [cache_k] * npg + [cache_v] * npg + [cache_logf] * npg + [k_new, v_new, lf_new, e, sui, tri]
    return pl.pallas_call(
        functools.partial(_sample_attn_kernel, pages_per_step=npg, n_new=n_new),
        grid_spec=grid_spec,
        out_shape=jax.ShapeDtypeStruct((n_seq, n_rows, D_ATTN), F32),
        compiler_params=pltpu.CompilerParams(dimension_semantics=("arbitrary", "arbitrary"),
                                             vmem_limit_bytes=VMEM_LIMIT_BYTES),
        name="attn_sample",
    )(*args)


def _route(logits):
    rows = logits.shape[0]
    lane = lax.broadcasted_iota(jnp.int32, (rows, LANES), 1).astype(F32)
    big = float(LANES)
    in_g = lane < N_GROUPS
    gl = jnp.where(in_g, logits, MASK_VALUE)
    m_g = jnp.max(gl, axis=1, keepdims=True)
    z_g = jnp.sum(jnp.exp(gl - m_g), axis=1, keepdims=True)
    p_g = 1.0 / z_g
    g_idx = jnp.min(jnp.where(in_g & (gl == m_g), lane, big), axis=1, keepdims=True)
    lo = N_GROUPS + EXPERTS_PER_GROUP * g_idx
    sel = (lane >= lo) & (lane < lo + EXPERTS_PER_GROUP)
    el = jnp.where(sel, logits, MASK_VALUE)
    m_e = jnp.max(el, axis=1, keepdims=True)
    ex = jnp.exp(el - m_e)
    ep = ex / jnp.sum(ex, axis=1, keepdims=True)
    v1 = jnp.max(jnp.where(sel, ep, -1.0), axis=1, keepdims=True)
    i1 = jnp.min(jnp.where(sel & (ep == v1), lane, big), axis=1, keepdims=True)
    sel2 = sel & (lane != i1)
    v2 = jnp.max(jnp.where(sel2, ep, -1.0), axis=1, keepdims=True)
    i2 = jnp.min(jnp.where(sel2 & (ep == v2), lane, big), axis=1, keepdims=True)
    denom = v1 + v2
    w1 = p_g * (v1 / denom)
    w2 = p_g * (v2 / denom)
    return jnp.where(lane == i1 - N_GROUPS, w1, 0.0) + jnp.where(lane == i2 - N_GROUPS, w2, 0.0)


def _post_kernel(o_ref, ap_ref, sgb_ref, x_ref, wao_ref, wo_ref, g_ref, b_ref, wrh_ref, wrl_ref,
                 x1_ref, gate_ref, *, alpha):
    yb = _dot(o_ref[...], wao_ref[...])
    mix = _dot((ap_ref[...] + sgb_ref[...] * yb).astype(BF16), wo_ref[...])
    x1 = _layer_norm(alpha * x_ref[...] + mix, g_ref[...], b_ref[...])
    x1_ref[...] = x1
    xh = x1.astype(BF16)
    xl = (x1 - xh.astype(F32)).astype(BF16)
    logits = _dot(xh, wrh_ref[...]) + _dot(xl, wrh_ref[...]) + _dot(xh, wrl_ref[...])
    gate_ref[...] = _route(logits)


def _post_attention(o, ap, sgb, x, w, *, alpha):
    t, d_model = x.shape
    rows = 512
    row_spec = lambda width: pl.BlockSpec((rows, width), lambda i: (i, 0))
    weights = [w['wao'], w['wo'], w['ln1_g'], w['ln1_b'], w['wr_hi'], w['wr_lo']]
    return pl.pallas_call(
        functools.partial(_post_kernel, alpha=alpha),
        grid=(t // rows,),
        in_specs=[row_spec(D_ATTN), row_spec(d_model), row_spec(d_model), row_spec(d_model)]
                 + [_const_spec(a.shape) for a in weights],
        out_specs=[row_spec(d_model), row_spec(LANES)],
        out_shape=[jax.ShapeDtypeStruct((t, d_model), F32), jax.ShapeDtypeStruct((t, LANES), F32)],
        compiler_params=pltpu.CompilerParams(dimension_semantics=("arbitrary",),
                                             vmem_limit_bytes=VMEM_LIMIT_BYTES),
        name="post_attn",
    )(o, ap, sgb, x, *weights)


def _moe_kernel(x1_ref, gate_ref, w1_ref, w3_ref, w2_ref, g_ref, b_ref, y_ref, xb_sc, acc_sc, *, alpha):
    e = pl.program_id(1)

    @pl.when(e == 0)
    def _():
        xb_sc[...] = x1_ref[...].astype(BF16)
        acc_sc[...] = jnp.zeros_like(acc_sc)

    xb = xb_sc[...]
    h1 = _dot(xb, w1_ref[0])
    h3 = _dot(xb, w3_ref[0])
    h = (h1 * jax.nn.sigmoid(h1)) * h3
    ye = _dot(h.astype(BF16), w2_ref[0])
    gate = gate_ref[...]
    lane = lax.broadcasted_iota(jnp.int32, gate.shape, 1)
    ge = jnp.sum(jnp.where(lane == e, gate, 0.0), axis=1, keepdims=True)
    acc_sc[...] += ge * ye

    @pl.when(e == pl.num_programs(1) - 1)
    def _():
        y_ref[...] = _layer_norm(alpha * x1_ref[...] + acc_sc[...], g_ref[...], b_ref[...])


def _experts(x1, gate, w, *, alpha):
    t, d_model = x1.shape
    n_exp, _, d_exp = w['w1'].shape
    rows = 512
    return pl.pallas_call(
        functools.partial(_moe_kernel, alpha=alpha),
        grid=(t // rows, n_exp),
        in_specs=[
            pl.BlockSpec((rows, d_model), lambda i, e: (i, 0)),
            pl.BlockSpec((rows, LANES), lambda i, e: (i, 0)),
            pl.BlockSpec((1, d_model, d_exp), lambda i, e: (e, 0, 0)),
            pl.BlockSpec((1, d_model, d_exp), lambda i, e: (e, 0, 0)),
            pl.BlockSpec((1, d_exp, d_model), lambda i, e: (e, 0, 0)),
            pl.BlockSpec((1, d_model), lambda i, e: (0, 0)),
            pl.BlockSpec((1, d_model), lambda i, e: (0, 0)),
        ],
        out_specs=pl.BlockSpec((rows, d_model), lambda i, e: (i, 0)),
        out_shape=jax.ShapeDtypeStruct((t, d_model), F32),
        scratch_shapes=[pltpu.VMEM((rows, d_model), BF16), pltpu.VMEM((rows, d_model), F32)],
        compiler_params=pltpu.CompilerParams(dimension_semantics=("arbitrary", "arbitrary"),
                                             vmem_limit_bytes=VMEM_LIMIT_BYTES),
        name="experts",
    )(x1, gate, w['w1'], w['w3'], w['w2'], w['ln2_g'], w['ln2_b'])


def _prepare_weights(w_in, b_f, conv_w, w_conv_out, w_attn_out, w_o, ln1_g, ln1_b,
                     w_group, w_router, w1, w3, w2, ln2_g, ln2_b, rows_inproj):
    d_model = w_in.shape[0]
    d_conv = w_conv_out.shape[0]
    o_q = 3 * d_conv
    o_k = o_q + D_ATTN
    o_v = o_k + D_ATTN
    o_f = o_v + D_ATTN
    o_g = o_f + N_HEADS
    w_bf = w_in.astype(BF16)
    wm = jnp.concatenate([w_bf[:, 0:o_q], w_bf[:, o_k:o_f]], axis=1)
    wq = jnp.pad(w_bf[:, o_q:o_k].reshape(d_model, N_HEADS, HEAD_DIM),
                 ((0, 0), (0, 0), (0, LANES - HEAD_DIM))).reshape(d_model, N_HEADS * LANES)
    wf = jnp.pad(w_bf[:, o_f:o_g], ((0, 0), (0, LANES - N_HEADS)))
    wg = w_bf[:, o_g:]
    bf = jnp.pad(b_f.astype(F32), (0, LANES - N_HEADS)).reshape(1, LANES)

    col = jnp.arange(N_HEADS * LANES)
    col_head, col_lane = col // LANES, col % LANES
    src = jnp.arange(D_ATTN)
    sel = ((src[:, None] // HEAD_DIM == col_head[None, :]) & (src[:, None] % HEAD_DIM == col_lane[None, :]))
    part = jnp.arange(AUG_SPLITS * LANES)
    part_k, part_h = part // LANES, part % LANES
    pq = (part_h[:, None] == col_head[None, :]) & (col_lane[None, :] == AUG_Q_C + part_k[:, None])
    pk = (part_h[:, None] == col_head[None, :]) & (col_lane[None, :] == AUG_Q_C + AUG_SPLITS + part_k[:, None])
    oq = ((col_lane >= AUG_Q_C + AUG_SPLITS) & (col_lane < AUG_Q_C + 2 * AUG_SPLITS)).astype(F32).reshape(1, -1)
    ok = ((col_lane >= AUG_Q_C) & (col_lane < AUG_Q_C + AUG_SPLITS)).astype(F32).reshape(1, -1)
    r = jnp.arange(rows_inproj)
    tri = (r[None, :] <= r[:, None]).astype(BF16)

    wr = jnp.pad(jnp.concatenate([w_group, w_router], axis=1), ((0, 0), (0, LANES - N_GROUPS - N_EXPERTS)))
    wr_hi = wr.astype(BF16)
    wr_lo = (wr - wr_hi.astype(F32)).astype(BF16)
    return dict(
        wm=wm, wq=wq, wf=wf, wg=wg, bf=bf, cw=conv_w.astype(F32), wco=w_conv_out.astype(BF16),
        tri=tri, sel=sel.astype(BF16), pq=pq.astype(BF16), pk=pk.astype(BF16), oq=oq, ok=ok,
        wao=w_attn_out.astype(BF16), wo=w_o.astype(BF16),
        ln1_g=ln1_g.reshape(1, -1), ln1_b=ln1_b.reshape(1, -1), wr_hi=wr_hi, wr_lo=wr_lo,
        w1=w1.astype(BF16), w3=w3.astype(BF16), w2=w2.astype(BF16),
        ln2_g=ln2_g.reshape(1, -1), ln2_b=ln2_b.reshape(1, -1),
    )


def kernel(x_prompt, x_sample, cache_k, cache_v, cache_logf, state_conv, page_table, w_in, b_f, conv_w,
           w_conv_out, w_attn_out, w_o, ln1_g, ln1_b, w_group, w_router, w1, w3, w2, ln2_g, ln2_b):
    depth = w_in.shape[0]
    assert depth == 1, "single-layer stack"
    n_p, s_p, d_model = x_prompt.shape
    n_s, s_s, _ = x_sample.shape
    d_conv = w_conv_out.shape[1]
    alpha = (2 * depth) ** 0.25
    w = _prepare_weights(w_in[0], b_f[0], conv_w[0], w_conv_out[0], w_attn_out[0], w_o[0], ln1_g[0], ln1_b[0],
                         w_group[0], w_router[0], w1[0], w3[0], w2[0], ln2_g[0], ln2_b[0], rows_inproj=256)

    xp = x_prompt.reshape(n_p * s_p, d_model)
    qa, k_p, v_p, ka, vb, lf_p, ap, sgb, tail = _inproj(xp, w, sample=False, seq_len=s_p)
    o_p = _prompt_attention(qa, ka, vb, n_seq=n_p, seq_len=s_p)
    x1_p, gate_p = _post_attention(o_p, ap, sgb, xp, w, alpha=alpha)
    y_p = _experts(x1_p, gate_p, w, alpha=alpha)

    xs = x_sample.reshape(n_s * s_s, d_model)
    state = state_conv[0]
    s0 = jnp.repeat(state[:, 0, :], s_s, axis=0)
    s1 = jnp.repeat(state[:, 1, :], s_s, axis=0)
    q_s, k_s, v_s, lf_s, ap_s, sgb_s, u_s = _inproj(xs, w, sample=True, seq_len=s_s, state=(s0, s1))
    q5 = q_s.reshape(n_s, s_s, N_HEADS, LANES)[..., :HEAD_DIM]
    eye_h = jnp.eye(N_HEADS, dtype=BF16)
    qblk = (q5.transpose(0, 2, 1, 3)[:, :, :, None, :] * eye_h[None, :, None, :, None]).reshape(
        n_s, N_HEADS * s_s, D_ATTN)
    n_pool = cache_k.shape[1]
    o_blk = _sample_attention(
        qblk, cache_k[0].reshape(n_pool, PAGE_SIZE, D_ATTN), cache_v[0].reshape(n_pool, PAGE_SIZE, D_ATTN),
        cache_logf[0], page_table, k_s.reshape(n_s, s_s, D_ATTN), v_s.reshape(n_s, s_s, D_ATTN),
        lf_s.reshape(n_s, s_s, N_HEADS))
    o5 = o_blk.reshape(n_s, N_HEADS, s_s, N_HEADS, HEAD_DIM)
    o_s = jnp.einsum('bhtgd,hg->bthd', o5, jnp.eye(N_HEADS, dtype=F32)).reshape(n_s * s_s, D_ATTN).astype(BF16)
    x1_s, gate_s = _post_attention(o_s, ap_s, sgb_s, xs, w, alpha=alpha)
    y_s = _experts(x1_s, gate_s, w, alpha=alpha)

    return (
        y_p.reshape(n_p, s_p, d_model),
        y_s.reshape(n_s, s_s, d_model),
        k_p.reshape(1, n_p, s_p, N_HEADS, HEAD_DIM),
        v_p.reshape(1, n_p, s_p, N_HEADS, HEAD_DIM),
        lf_p.reshape(1, n_p, s_p, N_HEADS),
        tail[:, SUBLANES - (CONV_WIDTH - 1):, :].reshape(1, n_p, CONV_WIDTH - 1, d_conv),
        k_s.reshape(1, n_s, s_s, N_HEADS, HEAD_DIM),
        v_s.reshape(1, n_s, s_s, N_HEADS, HEAD_DIM),
        lf_s.reshape(1, n_s, s_s, N_HEADS),
        u_s.reshape(n_s, s_s, d_conv)[:, s_s - (CONV_WIDTH - 1):, :].reshape(1, n_s, CONV_WIDTH - 1, d_conv),
    )
```

```python
import functools

import jax
import jax.numpy as jnp
from jax import lax
from jax.experimental import pallas as pl
from jax.experimental.pallas import tpu as pltpu

F32 = jnp.float32
BF16 = jnp.bfloat16

LANES = 128
SUBLANES = 8
VMEM_LIMIT_BYTES = 56 * 1024 * 1024

CONV_WIDTH = 3
N_HEADS = 8
HEAD_DIM = 64
D_ATTN = N_HEADS * HEAD_DIM
N_GROUPS = 4
EXPERTS_PER_GROUP = 4
N_EXPERTS = N_GROUPS * EXPERTS_PER_GROUP
LN_EPS = 1e-5
PAGE_SIZE = 128
MASK_VALUE = -1e30
AUG_Q_C = HEAD_DIM
AUG_SPLITS = 3


def _dot(a, b):
    return jnp.dot(a, b, preferred_element_type=F32)


def _dot_nt(a, b):
    return lax.dot_general(a, b, (((1,), (1,)), ((), ())), preferred_element_type=F32)


def _split3(x):
    h1 = x.astype(BF16)
    r1 = x - h1.astype(F32)
    h2 = r1.astype(BF16)
    r2 = r1 - h2.astype(F32)
    h3 = r2.astype(BF16)
    return h1, h2, h3


def _dot_exact_rhs(a_bf16, x):
    h1, h2, h3 = _split3(x)
    return _dot(a_bf16, h1) + _dot(a_bf16, h2) + _dot(a_bf16, h3)


def _dot_exact_lhs(x, b_bf16):
    h1, h2, h3 = _split3(x)
    return _dot(h1, b_bf16) + _dot(h2, b_bf16) + _dot(h3, b_bf16)


def _dot_nt_exact_rhs(a_bf16, x):
    h1, h2, h3 = _split3(x)
    return _dot_nt(a_bf16, h1) + _dot_nt(a_bf16, h2) + _dot_nt(a_bf16, h3)


def _log_sigmoid(x):
    return jnp.minimum(x, 0.0) - jnp.log1p(jnp.exp(-jnp.abs(x)))


def _layer_norm(r, g, b):
    mu = jnp.mean(r, axis=-1, keepdims=True)
    d = r - mu
    var = jnp.mean(d * d, axis=-1, keepdims=True)
    return d * lax.rsqrt(var + LN_EPS) * g + b


def _const_spec(shape):
    nd = len(shape)
    return pl.BlockSpec(shape, lambda *_: (0,) * nd, pipeline_mode=pl.Buffered(1))


def _inproj_kernel(*refs, rows, tiles_per_seq, sample, d_conv, d_model):
    if sample:
        (x_ref, wm_ref, wq_ref, wf_ref, wg_ref, bf_ref, cw_ref, wco_ref, s0_ref, s1_ref,
         q_ref, k_ref, v_ref, lf_ref, ap_ref, sgb_ref, u_ref, ubuf) = refs
    else:
        (x_ref, wm_ref, wq_ref, wf_ref, wg_ref, bf_ref, cw_ref, wco_ref, tri_ref, sel_ref, pq_ref, pk_ref,
         oq_ref, ok_ref,
         q_ref, k_ref, v_ref, ka_ref, vb_ref, lf_ref, ap_ref, sgb_ref, tail_ref, ubuf, ccarry) = refs
    i = pl.program_id(0)
    xb = x_ref[...].astype(BF16)

    xc = _dot(xb, wm_ref[:, 0:d_conv])
    bg = _dot(xb, wm_ref[:, d_conv:2 * d_conv])
    cg = _dot(xb, wm_ref[:, 2 * d_conv:3 * d_conv])
    u = cg * xc
    if sample:
        ubuf[0:SUBLANES, :] = jnp.zeros((SUBLANES, d_conv), F32)
    else:
        @pl.when(i % tiles_per_seq == 0)
        def _():
            ubuf[0:SUBLANES, :] = jnp.zeros((SUBLANES, d_conv), F32)
    ubuf[SUBLANES:SUBLANES + rows, :] = u
    prev1 = ubuf[SUBLANES - 1:SUBLANES - 1 + rows, :]
    prev2 = ubuf[SUBLANES - 2:SUBLANES - 2 + rows, :]
    if sample:
        pos = lax.broadcasted_iota(jnp.int32, (rows, 1), 0) % 4
        prev1 = jnp.where(pos == 0, s1_ref[...], prev1)
        prev2 = jnp.where(pos == 0, s0_ref[...], jnp.where(pos == 1, s1_ref[...], prev2))
        u_ref[...] = u
    else:
        ubuf[0:SUBLANES, :] = u[rows - SUBLANES:rows, :]

        @pl.when(i % tiles_per_seq == tiles_per_seq - 1)
        def _():
            tail_ref[0] = u[rows - SUBLANES:rows, :]
    conv = cw_ref[0:1, :] * prev2 + cw_ref[1:2, :] * prev1 + cw_ref[2:3, :] * u
    ya = _dot((bg * conv).astype(BF16), wco_ref[...])

    k = _dot(xb, wm_ref[:, 3 * d_conv:3 * d_conv + D_ATTN])
    v = _dot(xb, wm_ref[:, 3 * d_conv + D_ATTN:3 * d_conv + 2 * D_ATTN])
    qa = _dot(xb, wq_ref[...]) * (HEAD_DIM ** -0.5)
    fg = _dot(xb, wf_ref[...]) + bf_ref[...]
    lane = lax.broadcasted_iota(jnp.int32, (rows, LANES), 1)
    logf = jnp.where(lane < N_HEADS, _log_sigmoid(fg), 0.0)
    if sample:
        k_ref[...] = k
        v_ref[...] = v
        lf_ref[...] = logf[:, 0:N_HEADS]
        q_ref[...] = qa.astype(BF16)
    else:
        k_ref[0] = k.T
        vt = v.T
        v_ref[0] = vt
        vb_ref[0] = vt.astype(BF16)
        lf_ref[0] = logf.T[0:N_HEADS, :]

        @pl.when(i % tiles_per_seq == 0)
        def _():
            ccarry[...] = jnp.zeros((1, LANES), F32)
        c = _dot_exact_rhs(tri_ref[...], logf) + ccarry[...]
        ccarry[...] = c[rows - 1:rows, :]
        c1, c2, c3 = _split3(c)
        c_parts = jnp.concatenate([c1, c2, c3], axis=1)
        q_ref[...] = (qa + _dot(c_parts, pq_ref[...]) + oq_ref[...]).astype(BF16)
        ka = _dot(k.astype(BF16), sel_ref[...]) - _dot(c_parts, pk_ref[...]) + ok_ref[...]
        ka_ref[...] = ka.astype(BF16)

    ga = _dot(xb, wg_ref[:, 0:d_model])
    gb = _dot(xb, wg_ref[:, d_model:2 * d_model])
    ap_ref[...] = jax.nn.sigmoid(ga) * ya
    sgb_ref[...] = jax.nn.sigmoid(gb)


def _inproj(x, w, *, sample, seq_len, state=None):
    t, d_model = x.shape
    d_conv = w['wco'].shape[0]
    rows = 256
    tiles_per_seq = max(seq_len // rows, 1)
    n_tiles = t // rows
    row_spec = lambda width: pl.BlockSpec((rows, width), lambda i: (i, 0))
    weights = [w['wm'], w['wq'], w['wf'], w['wg'], w['bf'], w['cw'], w['wco']]
    in_specs = [row_spec(d_model)] + [_const_spec(a.shape) for a in weights]
    args = [x] + weights
    if sample:
        args += [state[0], state[1]]
        in_specs += [row_spec(d_conv), row_spec(d_conv)]
        out_shape = [
            jax.ShapeDtypeStruct((t, N_HEADS * LANES), BF16),
            jax.ShapeDtypeStruct((t, D_ATTN), F32),
            jax.ShapeDtypeStruct((t, D_ATTN), F32),
            jax.ShapeDtypeStruct((t, N_HEADS), F32),
            jax.ShapeDtypeStruct((t, d_model), F32),
            jax.ShapeDtypeStruct((t, d_model), F32),
            jax.ShapeDtypeStruct((t, d_conv), F32),
        ]
        out_specs = [row_spec(N_HEADS * LANES), row_spec(D_ATTN), row_spec(D_ATTN), row_spec(N_HEADS),
                     row_spec(d_model), row_spec(d_model), row_spec(d_conv)]
        scratch = [pltpu.VMEM((rows + SUBLANES, d_conv), F32)]
    else:
        consts = [w['tri'], w['sel'], w['pq'], w['pk'], w['oq'], w['ok']]
        args += consts
        in_specs += [_const_spec(a.shape) for a in consts]
        n_seq = t // seq_len
        out_shape = [
            jax.ShapeDtypeStruct((t, N_HEADS * LANES), BF16),
            jax.ShapeDtypeStruct((n_seq, D_ATTN, seq_len), F32),
            jax.ShapeDtypeStruct((n_seq, D_ATTN, seq_len), F32),
            jax.ShapeDtypeStruct((t, N_HEADS * LANES), BF16),
            jax.ShapeDtypeStruct((n_tiles, D_ATTN, rows), BF16),
            jax.ShapeDtypeStruct((n_seq, N_HEADS, seq_len), F32),
            jax.ShapeDtypeStruct((t, d_model), F32),
            jax.ShapeDtypeStruct((t, d_model), F32),
            jax.ShapeDtypeStruct((n_seq, SUBLANES, d_conv), F32),
        ]
        pos_minor = lambda height: pl.BlockSpec((1, height, rows),
                                                lambda i: (i // tiles_per_seq, 0, i % tiles_per_seq))
        out_specs = [row_spec(N_HEADS * LANES), pos_minor(D_ATTN), pos_minor(D_ATTN), row_spec(N_HEADS * LANES),
                     pl.BlockSpec((1, D_ATTN, rows), lambda i: (i, 0, 0)), pos_minor(N_HEADS),
                     row_spec(d_model), row_spec(d_model),
                     pl.BlockSpec((1, SUBLANES, d_conv), lambda i: (i // tiles_per_seq, 0, 0))]
        scratch = [pltpu.VMEM((rows + SUBLANES, d_conv), F32), pltpu.VMEM((1, LANES), F32)]
    kern = functools.partial(_inproj_kernel, rows=rows, tiles_per_seq=tiles_per_seq, sample=sample,
                             d_conv=d_conv, d_model=d_model)
    return pl.pallas_call(
        kern,
        grid=(n_tiles,),
        in_specs=in_specs,
        out_specs=out_specs,
        out_shape=out_shape,
        scratch_shapes=scratch,
        compiler_params=pltpu.CompilerParams(dimension_semantics=("arbitrary",),
                                             vmem_limit_bytes=VMEM_LIMIT_BYTES),
        name="inproj_sample" if sample else "inproj_prompt",
    )(*args)


def _attn_kernel(q_ref, k_ref, vt_ref, o_ref, m_sc, l_sc, acc_sc, *, tq, v_chunk):
    i = pl.program_id(2)
    chunks = tq // v_chunk
    for hl in range(2):
        m_sc[hl] = jnp.full((1, tq), MASK_VALUE, F32)
        l_sc[hl] = jnp.zeros((1, tq), F32)
        acc_sc[hl] = jnp.zeros((HEAD_DIM, tq), F32)

    def step(j, masked):
        start = pl.multiple_of(j * tq, tq)
        for hl in range(2):
            st = _dot_nt(k_ref[pl.ds(start, tq), hl * LANES:(hl + 1) * LANES],
                         q_ref[:, hl * LANES:(hl + 1) * LANES])
            if masked:
                key = lax.broadcasted_iota(jnp.int32, (tq, tq), 0)
                qry = lax.broadcasted_iota(jnp.int32, (tq, tq), 1)
                st = jnp.where(key <= qry, st, MASK_VALUE)
            m_prev = m_sc[hl]
            m_new = jnp.maximum(m_prev, jnp.max(st, axis=0, keepdims=True))
            alpha = jnp.exp(m_prev - m_new)
            pt = jnp.exp(st - m_new)
            l_sc[hl] = alpha * l_sc[hl] + jnp.sum(pt, axis=0, keepdims=True)
            pb = pt.astype(BF16)
            acc = alpha * acc_sc[hl]
            for c in range(chunks):
                vt = vt_ref[j * chunks + c, hl * HEAD_DIM:(hl + 1) * HEAD_DIM, :]
                acc = acc + _dot(vt, pb[c * v_chunk:(c + 1) * v_chunk, :])
            acc_sc[hl] = acc
            m_sc[hl] = m_new

    def body(j, carry):
        step(j, False)
        return carry

    lax.fori_loop(0, i, body, 0)
    step(i, True)
    ot = jnp.concatenate([acc_sc[0] / l_sc[0], acc_sc[1] / l_sc[1]], axis=0)
    o_ref[...] = ot.T.astype(BF16)


def _prompt_attention(qa, ka, vtb, *, n_seq, seq_len):
    t = qa.shape[0]
    v_chunk = vtb.shape[2]
    tq = 512
    qt = seq_len // tq
    n_pairs = N_HEADS // 2
    return pl.pallas_call(
        functools.partial(_attn_kernel, tq=tq, v_chunk=v_chunk),
        grid=(n_seq, n_pairs, qt),
        in_specs=[
            pl.BlockSpec((tq, 2 * LANES), lambda n, hp, i: (n * qt + i, hp)),
            pl.BlockSpec((seq_len, 2 * LANES), lambda n, hp, i: (n, hp)),
            pl.BlockSpec((seq_len // v_chunk, 2 * HEAD_DIM, v_chunk), lambda n, hp, i: (n, hp, 0)),
        ],
        out_specs=pl.BlockSpec((tq, LANES), lambda n, hp, i: (n * qt + i, hp)),
        out_shape=jax.ShapeDtypeStruct((t, D_ATTN), BF16),
        scratch_shapes=[pltpu.VMEM((2, 1, tq), F32), pltpu.VMEM((2, 1, tq), F32),
                        pltpu.VMEM((2, HEAD_DIM, tq), F32)],
        compiler_params=pltpu.CompilerParams(dimension_semantics=("arbitrary", "arbitrary", "arbitrary"),
                                             vmem_limit_bytes=VMEM_LIMIT_BYTES),
        name="attn_prompt",
    )(qa, ka, vtb)


def _sample_attn_kernel(pt_ref, *refs, n_pages, n_new):
    del pt_ref
    q_ref = refs[0]
    k_refs = refs[1:1 + n_pages]
    v_refs = refs[1 + n_pages:1 + 2 * n_pages]
    f_refs = refs[1 + 2 * n_pages:1 + 3 * n_pages]
    kn_ref, vn_ref, fn_ref, e_ref, suffix_ref, later_ref, tri_ref = refs[1 + 3 * n_pages:8 + 3 * n_pages]
    o_ref = refs[8 + 3 * n_pages]
    pad_sc, kn_sc, vn_sc = refs[9 + 3 * n_pages:]
    n_rows = N_HEADS * n_new
    qb = q_ref[0]
    e = e_ref[...]

    pad_sc[...] = jnp.zeros((PAGE_SIZE, LANES), F32)
    pad_sc[0:n_new, 0:N_HEADS] = fn_ref[0]
    kn_sc[...] = jnp.zeros((PAGE_SIZE, D_ATTN), F32)
    kn_sc[0:n_new, :] = kn_ref[0]
    vn_sc[...] = jnp.zeros((PAGE_SIZE, D_ATTN), F32)
    vn_sc[0:n_new, :] = vn_ref[0]
    cn = _dot_exact_rhs(tri_ref[...], pad_sc[...])
    cn_rows = _dot_nt_exact_rhs(e, cn)
    row_t = lax.broadcasted_iota(jnp.int32, (n_rows, LANES), 0) // N_HEADS
    col = lax.broadcasted_iota(jnp.int32, (n_rows, LANES), 1)
    cnq = jnp.sum(jnp.where(col == row_t, cn_rows, 0.0), axis=1, keepdims=True)
    s_new = _dot_nt(qb, kn_sc[...].astype(BF16)) + (cnq - cn_rows)
    s_new = jnp.where(col <= row_t, s_new, MASK_VALUE)

    lf = jnp.concatenate([f_refs[j][0] for j in range(n_pages)], axis=0)
    incl = _dot_exact_lhs(lf, suffix_ref[...])
    tot = jnp.broadcast_to(incl[:, 0:1], incl.shape)
    d = incl - lf + _dot_exact_rhs(later_ref[...], tot)
    s_pages = []
    for j in range(n_pages):
        bias = jnp.concatenate([d[j * N_HEADS:(j + 1) * N_HEADS, :]] * n_new, axis=0)
        s_pages.append(_dot(qb, k_refs[j][0].astype(BF16)) + bias + cnq)

    m_blk = s_new
    for s in s_pages:
        m_blk = jnp.maximum(m_blk, s)
    m = jnp.max(m_blk, axis=1, keepdims=True)
    p_new = jnp.exp(s_new - m)
    l_blk = p_new
    acc = _dot(p_new.astype(BF16), vn_sc[...].astype(BF16))
    for j in range(n_pages):
        p = jnp.exp(s_pages[j] - m)
        l_blk = l_blk + p
        acc = acc + _dot_nt(p.astype(BF16), v_refs[j][0].astype(BF16))
    o_ref[0] = acc / jnp.sum(l_blk, axis=1, keepdims=True)


def _sample_attention(qblk, cache_kt, cache_vt, cache_lft, page_table, k_new, v_new, lf_new):
    n_seq, n_rows, _ = qblk.shape
    n_new = k_new.shape[1]
    n_pages = page_table.shape[1]
    pt = page_table.reshape(-1)
    row_head = jnp.arange(n_rows) % N_HEADS
    e = (row_head[:, None] == jnp.arange(LANES)[None, :]).astype(BF16)
    pos = jnp.arange(PAGE_SIZE)
    suffix = (pos[:, None] >= pos[None, :]).astype(BF16)
    tri = (pos[None, :] <= pos[:, None]).astype(BF16)
    r = jnp.arange(n_pages * N_HEADS)
    later = ((r[:, None] % N_HEADS == r[None, :] % N_HEADS)
             & (r[None, :] // N_HEADS > r[:, None] // N_HEADS)).astype(BF16)

    def page_map(j):
        return lambda b, pt_ref: (pt_ref[b * n_pages + j], 0, 0)

    seq_map = lambda b, pt_ref: (b, 0, 0)
    const_map = lambda b, pt_ref: (0, 0)
    in_specs = [pl.BlockSpec((1, n_rows, D_ATTN), seq_map)]
    in_specs += [pl.BlockSpec((1, D_ATTN, PAGE_SIZE), page_map(j)) for j in range(n_pages)]
    in_specs += [pl.BlockSpec((1, D_ATTN, PAGE_SIZE), page_map(j)) for j in range(n_pages)]
    in_specs += [pl.BlockSpec((1, N_HEADS, PAGE_SIZE), page_map(j)) for j in range(n_pages)]
    in_specs += [pl.BlockSpec((1, n_new, D_ATTN), seq_map), pl.BlockSpec((1, n_new, D_ATTN), seq_map),
                 pl.BlockSpec((1, n_new, N_HEADS), seq_map),
                 pl.BlockSpec(e.shape, const_map), pl.BlockSpec(suffix.shape, const_map),
                 pl.BlockSpec(later.shape, const_map), pl.BlockSpec(tri.shape, const_map)]
    grid_spec = pltpu.PrefetchScalarGridSpec(
        num_scalar_prefetch=1,
        grid=(n_seq,),
        in_specs=in_specs,
        out_specs=pl.BlockSpec((1, n_rows, D_ATTN), seq_map),
        scratch_shapes=[pltpu.VMEM((PAGE_SIZE, LANES), F32),
                        pltpu.VMEM((PAGE_SIZE, D_ATTN), F32), pltpu.VMEM((PAGE_SIZE, D_ATTN), F32)],
    )
    args = ([pt, qblk] + [cache_kt] * n_pages + [cache_vt] * n_pages + [cache_lft] * n_pages
            + [k_new, v_new, lf_new, e, suffix, later, tri])
    return pl.pallas_call(
        functools.partial(_sample_attn_kernel, n_pages=n_pages, n_new=n_new),
        grid_spec=grid_spec,
        out_shape=jax.ShapeDtypeStruct((n_seq, n_rows, D_ATTN), F32),
        compiler_params=pltpu.CompilerParams(dimension_semantics=("arbitrary",),
                                             vmem_limit_bytes=VMEM_LIMIT_BYTES),
        name="attn_sample",
    )(*args)


def _route(logits):
    rows = logits.shape[0]
    lane = lax.broadcasted_iota(jnp.int32, (rows, LANES), 1).astype(F32)
    big = float(LANES)
    in_g = lane < N_GROUPS
    gl = jnp.where(in_g, logits, MASK_VALUE)
    m_g = jnp.max(gl, axis=1, keepdims=True)
    z_g = jnp.sum(jnp.exp(gl - m_g), axis=1, keepdims=True)
    p_g = 1.0 / z_g
    g_idx = jnp.min(jnp.where(in_g & (gl == m_g), lane, big), axis=1, keepdims=True)
    lo = N_GROUPS + EXPERTS_PER_GROUP * g_idx
    sel = (lane >= lo) & (lane < lo + EXPERTS_PER_GROUP)
    el = jnp.where(sel, logits, MASK_VALUE)
    m_e = jnp.max(el, axis=1, keepdims=True)
    ex = jnp.exp(el - m_e)
    ep = ex / jnp.sum(ex, axis=1, keepdims=True)
    v1 = jnp.max(jnp.where(sel, ep, -1.0), axis=1, keepdims=True)
    i1 = jnp.min(jnp.where(sel & (ep == v1), lane, big), axis=1, keepdims=True)
    sel2 = sel & (lane != i1)
    v2 = jnp.max(jnp.where(sel2, ep, -1.0), axis=1, keepdims=True)
    i2 = jnp.min(jnp.where(sel2 & (ep == v2), lane, big), axis=1, keepdims=True)
    denom = v1 + v2
    w1 = p_g * (v1 / denom)
    w2 = p_g * (v2 / denom)
    return jnp.where(lane == i1 - N_GROUPS, w1, 0.0) + jnp.where(lane == i2 - N_GROUPS, w2, 0.0)


def _post_kernel(o_ref, ap_ref, sgb_ref, x_ref, wao_ref, wo_ref, g_ref, b_ref, wrh_ref, wrl_ref,
                 x1_ref, gate_ref, *, alpha):
    yb = _dot(o_ref[...], wao_ref[...])
    mix = _dot((ap_ref[...] + sgb_ref[...] * yb).astype(BF16), wo_ref[...])
    x1 = _layer_norm(alpha * x_ref[...] + mix, g_ref[...], b_ref[...])
    x1_ref[...] = x1
    xh = x1.astype(BF16)
    xl = (x1 - xh.astype(F32)).astype(BF16)
    logits = _dot(xh, wrh_ref[...]) + _dot(xl, wrh_ref[...]) + _dot(xh, wrl_ref[...])
    gate_ref[...] = _route(logits)


def _post_attention(o, ap, sgb, x, w, *, alpha):
    t, d_model = x.shape
    rows = 512
    row_spec = lambda width: pl.BlockSpec((rows, width), lambda i: (i, 0))
    weights = [w['wao'], w['wo'], w['ln1_g'], w['ln1_b'], w['wr_hi'], w['wr_lo']]
    return pl.pallas_call(
        functools.partial(_post_kernel, alpha=alpha),
        grid=(t // rows,),
        in_specs=[row_spec(D_ATTN), row_spec(d_model), row_spec(d_model), row_spec(d_model)]
                 + [_const_spec(a.shape) for a in weights],
        out_specs=[row_spec(d_model), row_spec(LANES)],
        out_shape=[jax.ShapeDtypeStruct((t, d_model), F32), jax.ShapeDtypeStruct((t, LANES), F32)],
        compiler_params=pltpu.CompilerParams(dimension_semantics=("arbitrary",),
                                             vmem_limit_bytes=VMEM_LIMIT_BYTES),
        name="post_attn",
    )(o, ap, sgb, x, *weights)


def _moe_kernel(x1_ref, gate_ref, w1_ref, w3_ref, w2_ref, g_ref, b_ref, y_ref, xb_sc, acc_sc, *, alpha):
    e = pl.program_id(1)

    @pl.when(e == 0)
    def _():
        xb_sc[...] = x1_ref[...].astype(BF16)
        acc_sc[...] = jnp.zeros_like(acc_sc)

    xb = xb_sc[...]
    h1 = _dot(xb, w1_ref[0])
    h3 = _dot(xb, w3_ref[0])
    h = (h1 * jax.nn.sigmoid(h1)) * h3
    ye = _dot(h.astype(BF16), w2_ref[0])
    gate = gate_ref[...]
    lane = lax.broadcasted_iota(jnp.int32, gate.shape, 1)
    ge = jnp.sum(jnp.where(lane == e, gate, 0.0), axis=1, keepdims=True)
    acc_sc[...] += ge * ye

    @pl.when(e == pl.num_programs(1) - 1)
    def _():
        y_ref[...] = _layer_norm(alpha * x1_ref[...] + acc_sc[...], g_ref[...], b_ref[...])


def _experts(x1, gate, w, *, alpha):
    t, d_model = x1.shape
    n_exp, _, d_exp = w['w1'].shape
    rows = 512
    return pl.pallas_call(
        functools.partial(_moe_kernel, alpha=alpha),
        grid=(t // rows, n_exp),
        in_specs=[
            pl.BlockSpec((rows, d_model), lambda i, e: (i, 0)),
            pl.BlockSpec((rows, LANES), lambda i, e: (i, 0)),
            pl.BlockSpec((1, d_model, d_exp), lambda i, e: (e, 0, 0)),
            pl.BlockSpec((1, d_model, d_exp), lambda i, e: (e, 0, 0)),
            pl.BlockSpec((1, d_exp, d_model), lambda i, e: (e, 0, 0)),
            pl.BlockSpec((1, d_model), lambda i, e: (0, 0)),
            pl.BlockSpec((1, d_model), lambda i, e: (0, 0)),
        ],
        out_specs=pl.BlockSpec((rows, d_model), lambda i, e: (i, 0)),
        out_shape=jax.ShapeDtypeStruct((t, d_model), F32),
        scratch_shapes=[pltpu.VMEM((rows, d_model), BF16), pltpu.VMEM((rows, d_model), F32)],
        compiler_params=pltpu.CompilerParams(dimension_semantics=("arbitrary", "arbitrary"),
                                             vmem_limit_bytes=VMEM_LIMIT_BYTES),
        name="experts",
    )(x1, gate, w['w1'], w['w3'], w['w2'], w['ln2_g'], w['ln2_b'])


def _prepare_weights(w_in, b_f, conv_w, w_conv_out, w_attn_out, w_o, ln1_g, ln1_b,
                     w_group, w_router, w1, w3, w2, ln2_g, ln2_b, rows_inproj):
    d_model = w_in.shape[0]
    d_conv = w_conv_out.shape[0]
    o_q = 3 * d_conv
    o_k = o_q + D_ATTN
    o_v = o_k + D_ATTN
    o_f = o_v + D_ATTN
    o_g = o_f + N_HEADS
    w_bf = w_in.astype(BF16)
    wm = jnp.concatenate([w_bf[:, 0:o_q], w_bf[:, o_k:o_f]], axis=1)
    wq = jnp.pad(w_bf[:, o_q:o_k].reshape(d_model, N_HEADS, HEAD_DIM),
                 ((0, 0), (0, 0), (0, LANES - HEAD_DIM))).reshape(d_model, N_HEADS * LANES)
    wf = jnp.pad(w_bf[:, o_f:o_g], ((0, 0), (0, LANES - N_HEADS)))
    wg = w_bf[:, o_g:]
    bf = jnp.pad(b_f.astype(F32), (0, LANES - N_HEADS)).reshape(1, LANES)

    col = jnp.arange(N_HEADS * LANES)
    col_head, col_lane = col // LANES, col % LANES
    src = jnp.arange(D_ATTN)
    sel = ((src[:, None] // HEAD_DIM == col_head[None, :]) & (src[:, None] % HEAD_DIM == col_lane[None, :]))
    part = jnp.arange(AUG_SPLITS * LANES)
    part_k, part_h = part // LANES, part % LANES
    pq = (part_h[:, None] == col_head[None, :]) & (col_lane[None, :] == AUG_Q_C + part_k[:, None])
    pk = (part_h[:, None] == col_head[None, :]) & (col_lane[None, :] == AUG_Q_C + AUG_SPLITS + part_k[:, None])
    oq = ((col_lane >= AUG_Q_C + AUG_SPLITS) & (col_lane < AUG_Q_C + 2 * AUG_SPLITS)).astype(F32).reshape(1, -1)
    ok = ((col_lane >= AUG_Q_C) & (col_lane < AUG_Q_C + AUG_SPLITS)).astype(F32).reshape(1, -1)
    r = jnp.arange(rows_inproj)
    tri = (r[None, :] <= r[:, None]).astype(BF16)

    wr = jnp.pad(jnp.concatenate([w_group, w_router], axis=1), ((0, 0), (0, LANES - N_GROUPS - N_EXPERTS)))
    wr_hi = wr.astype(BF16)
    wr_lo = (wr - wr_hi.astype(F32)).astype(BF16)
    return dict(
        wm=wm, wq=wq, wf=wf, wg=wg, bf=bf, cw=conv_w.astype(F32), wco=w_conv_out.astype(BF16),
        tri=tri, sel=sel.astype(BF16), pq=pq.astype(BF16), pk=pk.astype(BF16), oq=oq, ok=ok,
        wao=w_attn_out.astype(BF16), wo=w_o.astype(BF16),
        ln1_g=ln1_g.reshape(1, -1), ln1_b=ln1_b.reshape(1, -1), wr_hi=wr_hi, wr_lo=wr_lo,
        w1=w1.astype(BF16), w3=w3.astype(BF16), w2=w2.astype(BF16),
        ln2_g=ln2_g.reshape(1, -1), ln2_b=ln2_b.reshape(1, -1),
    )


def kernel(x_prompt, x_sample, cache_k, cache_v, cache_logf, state_conv, page_table, w_in, b_f, conv_w,
           w_conv_out, w_attn_out, w_o, ln1_g, ln1_b, w_group, w_router, w1, w3, w2, ln2_g, ln2_b):
    depth = w_in.shape[0]
    assert depth == 1, "single-layer stack"
    n_p, s_p, d_model = x_prompt.shape
    n_s, s_s, _ = x_sample.shape
    d_conv = w_conv_out.shape[1]
    alpha = (2 * depth) ** 0.25
    w = _prepare_weights(w_in[0], b_f[0], conv_w[0], w_conv_out[0], w_attn_out[0], w_o[0], ln1_g[0], ln1_b[0],
                         w_group[0], w_router[0], w1[0], w3[0], w2[0], ln2_g[0], ln2_b[0], rows_inproj=256)

    xp = x_prompt.reshape(n_p * s_p, d_model)
    qa, kt_p, vt_p, ka, vtb, lft_p, ap, sgb, tail = _inproj(xp, w, sample=False, seq_len=s_p)
    o_p = _prompt_attention(qa, ka, vtb, n_seq=n_p, seq_len=s_p)
    x1_p, gate_p = _post_attention(o_p, ap, sgb, xp, w, alpha=alpha)
    y_p = _experts(x1_p, gate_p, w, alpha=alpha)

    xs = x_sample.reshape(n_s * s_s, d_model)
    state = state_conv[0]
    s0 = jnp.repeat(state[:, 0, :], s_s, axis=0)
    s1 = jnp.repeat(state[:, 1, :], s_s, axis=0)
    q_s, k_s, v_s, lf_s, ap_s, sgb_s, u_s = _inproj(xs, w, sample=True, seq_len=s_s, state=(s0, s1))
    q5 = q_s.reshape(n_s, s_s, N_HEADS, LANES)[..., :HEAD_DIM]
    eye_h = jnp.eye(N_HEADS, dtype=BF16)
    qblk = (q5[:, :, :, None, :] * eye_h[None, None, :, :, None]).reshape(n_s, s_s * N_HEADS, D_ATTN)
    n_pool = cache_k.shape[1]
    cache_kt = cache_k[0].transpose(0, 2, 3, 1).reshape(n_pool, D_ATTN, PAGE_SIZE)
    cache_vt = cache_v[0].transpose(0, 2, 3, 1).reshape(n_pool, D_ATTN, PAGE_SIZE)
    cache_lft = cache_logf[0].transpose(0, 2, 1)
    o_blk = _sample_attention(
        qblk, cache_kt, cache_vt, cache_lft, page_table, k_s.reshape(n_s, s_s, D_ATTN),
        v_s.reshape(n_s, s_s, D_ATTN), lf_s.reshape(n_s, s_s, N_HEADS))
    o5 = o_blk.reshape(n_s, s_s, N_HEADS, N_HEADS, HEAD_DIM)
    o_s = jnp.einsum('bthgd,hg->bthd', o5, jnp.eye(N_HEADS, dtype=F32)).reshape(n_s * s_s, D_ATTN).astype(BF16)
    x1_s, gate_s = _post_attention(o_s, ap_s, sgb_s, xs, w, alpha=alpha)
    y_s = _experts(x1_s, gate_s, w, alpha=alpha)

    return (
        y_p.reshape(n_p, s_p, d_model),
        y_s.reshape(n_s, s_s, d_model),
        kt_p.reshape(1, n_p, N_HEADS, HEAD_DIM, s_p).transpose(0, 1, 4, 2, 3),
        vt_p.reshape(1, n_p, N_HEADS, HEAD_DIM, s_p).transpose(0, 1, 4, 2, 3),
        lft_p.reshape(1, n_p, N_HEADS, s_p).transpose(0, 1, 3, 2),
        tail[:, SUBLANES - (CONV_WIDTH - 1):, :].reshape(1, n_p, CONV_WIDTH - 1, d_conv),
        k_s.reshape(1, n_s, s_s, N_HEADS, HEAD_DIM),
        v_s.reshape(1, n_s, s_s, N_HEADS, HEAD_DIM),
        lf_s.reshape(1, n_s, s_s, N_HEADS),
        u_s.reshape(n_s, s_s, d_conv)[:, s_s - (CONV_WIDTH - 1):, :].reshape(1, n_s, CONV_WIDTH - 1, d_conv),
    )
```

```python
import functools

import jax
import jax.numpy as jnp
from jax import lax
from jax.experimental import pallas as pl
from jax.experimental.pallas import tpu as pltpu

F32 = jnp.float32
BF16 = jnp.bfloat16

LANES = 128
SUBLANES = 8
VMEM_LIMIT_BYTES = 56 * 1024 * 1024

CONV_WIDTH = 3
N_HEADS = 8
HEAD_DIM = 64
D_ATTN = N_HEADS * HEAD_DIM
N_GROUPS = 4
EXPERTS_PER_GROUP = 4
N_EXPERTS = N_GROUPS * EXPERTS_PER_GROUP
LN_EPS = 1e-5
PAGE_SIZE = 128
MASK_VALUE = -1e30
AUG_Q_C = HEAD_DIM
AUG_SPLITS = 3
MOE_ROWS = 512
ROUTE_GROUP_LANE = EXPERTS_PER_GROUP
ROUTE_RANK_LANE = EXPERTS_PER_GROUP + 1
PERMUTE_CHUNK = 128
PERMUTE_UNROLL = 8


def _dot(a, b):
    return jnp.dot(a, b, preferred_element_type=F32)


def _dot_nt(a, b):
    return lax.dot_general(a, b, (((1,), (1,)), ((), ())), preferred_element_type=F32)


def _split3(x):
    h1 = x.astype(BF16)
    r1 = x - h1.astype(F32)
    h2 = r1.astype(BF16)
    r2 = r1 - h2.astype(F32)
    h3 = r2.astype(BF16)
    return h1, h2, h3


def _dot_exact_rhs(a_bf16, x):
    h1, h2, h3 = _split3(x)
    return _dot(a_bf16, h1) + _dot(a_bf16, h2) + _dot(a_bf16, h3)


def _dot_exact_lhs(x, b_bf16):
    h1, h2, h3 = _split3(x)
    return _dot(h1, b_bf16) + _dot(h2, b_bf16) + _dot(h3, b_bf16)


def _dot_nt_exact_rhs(a_bf16, x):
    h1, h2, h3 = _split3(x)
    return _dot_nt(a_bf16, h1) + _dot_nt(a_bf16, h2) + _dot_nt(a_bf16, h3)


def _log_sigmoid(x):
    return jnp.minimum(x, 0.0) - jnp.log1p(jnp.exp(-jnp.abs(x)))


def _layer_norm(r, g, b):
    mu = jnp.mean(r, axis=-1, keepdims=True)
    d = r - mu
    var = jnp.mean(d * d, axis=-1, keepdims=True)
    return d * lax.rsqrt(var + LN_EPS) * g + b


def _const_spec(shape):
    nd = len(shape)
    return pl.BlockSpec(shape, lambda *_: (0,) * nd, pipeline_mode=pl.Buffered(1))


def _inproj_kernel(*refs, rows, tiles_per_seq, sample, d_conv, d_model):
    if sample:
        (x_ref, wm_ref, wq_ref, wf_ref, wg_ref, bf_ref, cw_ref, wco_ref, s0_ref, s1_ref,
         q_ref, k_ref, v_ref, lf_ref, ap_ref, sgb_ref, u_ref, ubuf) = refs
    else:
        (x_ref, wm_ref, wq_ref, wf_ref, wg_ref, bf_ref, cw_ref, wco_ref, tri_ref, sel_ref, pq_ref, pk_ref,
         oq_ref, ok_ref,
         q_ref, k_ref, v_ref, ka_ref, vb_ref, lf_ref, ap_ref, sgb_ref, tail_ref, ubuf, ccarry) = refs
    i = pl.program_id(0)
    xb = x_ref[...].astype(BF16)

    xc = _dot(xb, wm_ref[:, 0:d_conv])
    bg = _dot(xb, wm_ref[:, d_conv:2 * d_conv])
    cg = _dot(xb, wm_ref[:, 2 * d_conv:3 * d_conv])
    u = cg * xc
    if sample:
        ubuf[0:SUBLANES, :] = jnp.zeros((SUBLANES, d_conv), F32)
    else:
        @pl.when(i % tiles_per_seq == 0)
        def _():
            ubuf[0:SUBLANES, :] = jnp.zeros((SUBLANES, d_conv), F32)
    ubuf[SUBLANES:SUBLANES + rows, :] = u
    prev1 = ubuf[SUBLANES - 1:SUBLANES - 1 + rows, :]
    prev2 = ubuf[SUBLANES - 2:SUBLANES - 2 + rows, :]
    if sample:
        pos = lax.broadcasted_iota(jnp.int32, (rows, 1), 0) % 4
        prev1 = jnp.where(pos == 0, s1_ref[...], prev1)
        prev2 = jnp.where(pos == 0, s0_ref[...], jnp.where(pos == 1, s1_ref[...], prev2))
        u_ref[...] = u
    else:
        ubuf[0:SUBLANES, :] = u[rows - SUBLANES:rows, :]

        @pl.when(i % tiles_per_seq == tiles_per_seq - 1)
        def _():
            tail_ref[0] = u[rows - SUBLANES:rows, :]
    conv = cw_ref[0:1, :] * prev2 + cw_ref[1:2, :] * prev1 + cw_ref[2:3, :] * u
    ya = _dot((bg * conv).astype(BF16), wco_ref[...])

    k = _dot(xb, wm_ref[:, 3 * d_conv:3 * d_conv + D_ATTN])
    v = _dot(xb, wm_ref[:, 3 * d_conv + D_ATTN:3 * d_conv + 2 * D_ATTN])
    qa = _dot(xb, wq_ref[...]) * (HEAD_DIM ** -0.5)
    fg = _dot(xb, wf_ref[...]) + bf_ref[...]
    lane = lax.broadcasted_iota(jnp.int32, (rows, LANES), 1)
    logf = jnp.where(lane < N_HEADS, _log_sigmoid(fg), 0.0)
    if sample:
        k_ref[...] = k
        v_ref[...] = v
        lf_ref[...] = logf[:, 0:N_HEADS]
        q_ref[...] = qa.astype(BF16)
    else:
        k_ref[0] = k.T
        vt = v.T
        v_ref[0] = vt
        vb_ref[0] = vt.astype(BF16)
        lf_ref[0] = logf.T[0:N_HEADS, :]

        @pl.when(i % tiles_per_seq == 0)
        def _():
            ccarry[...] = jnp.zeros((1, LANES), F32)
        c = _dot_exact_rhs(tri_ref[...], logf) + ccarry[...]
        ccarry[...] = c[rows - 1:rows, :]
        c1, c2, c3 = _split3(c)
        c_parts = jnp.concatenate([c1, c2, c3], axis=1)
        q_ref[...] = (qa + _dot(c_parts, pq_ref[...]) + oq_ref[...]).astype(BF16)
        ka = _dot(k.astype(BF16), sel_ref[...]) - _dot(c_parts, pk_ref[...]) + ok_ref[...]
        ka_ref[...] = ka.astype(BF16)

    ga = _dot(xb, wg_ref[:, 0:d_model])
    gb = _dot(xb, wg_ref[:, d_model:2 * d_model])
    ap_ref[...] = jax.nn.sigmoid(ga) * ya
    sgb_ref[...] = jax.nn.sigmoid(gb)


def _inproj(x, w, *, sample, seq_len, state=None):
    t, d_model = x.shape
    d_conv = w['wco'].shape[0]
    rows = 256
    tiles_per_seq = max(seq_len // rows, 1)
    n_tiles = t // rows
    row_spec = lambda width: pl.BlockSpec((rows, width), lambda i: (i, 0))
    weights = [w['wm'], w['wq'], w['wf'], w['wg'], w['bf'], w['cw'], w['wco']]
    in_specs = [row_spec(d_model)] + [_const_spec(a.shape) for a in weights]
    args = [x] + weights
    if sample:
        args += [state[0], state[1]]
        in_specs += [row_spec(d_conv), row_spec(d_conv)]
        out_shape = [
            jax.ShapeDtypeStruct((t, N_HEADS * LANES), BF16),
            jax.ShapeDtypeStruct((t, D_ATTN), F32),
            jax.ShapeDtypeStruct((t, D_ATTN), F32),
            jax.ShapeDtypeStruct((t, N_HEADS), F32),
            jax.ShapeDtypeStruct((t, d_model), F32),
            jax.ShapeDtypeStruct((t, d_model), F32),
            jax.ShapeDtypeStruct((t, d_conv), F32),
        ]
        out_specs = [row_spec(N_HEADS * LANES), row_spec(D_ATTN), row_spec(D_ATTN), row_spec(N_HEADS),
                     row_spec(d_model), row_spec(d_model), row_spec(d_conv)]
        scratch = [pltpu.VMEM((rows + SUBLANES, d_conv), F32)]
    else:
        consts = [w['tri'], w['sel'], w['pq'], w['pk'], w['oq'], w['ok']]
        args += consts
        in_specs += [_const_spec(a.shape) for a in consts]
        n_seq = t // seq_len
        out_shape = [
            jax.ShapeDtypeStruct((t, N_HEADS * LANES), BF16),
            jax.ShapeDtypeStruct((n_seq, D_ATTN, seq_len), F32),
            jax.ShapeDtypeStruct((n_seq, D_ATTN, seq_len), F32),
            jax.ShapeDtypeStruct((t, N_HEADS * LANES), BF16),
            jax.ShapeDtypeStruct((n_tiles, D_ATTN, rows), BF16),
            jax.ShapeDtypeStruct((n_seq, N_HEADS, seq_len), F32),
            jax.ShapeDtypeStruct((t, d_model), F32),
            jax.ShapeDtypeStruct((t, d_model), F32),
            jax.ShapeDtypeStruct((n_seq, SUBLANES, d_conv), F32),
        ]
        pos_minor = lambda height: pl.BlockSpec((1, height, rows),
                                                lambda i: (i // tiles_per_seq, 0, i % tiles_per_seq))
        out_specs = [row_spec(N_HEADS * LANES), pos_minor(D_ATTN), pos_minor(D_ATTN), row_spec(N_HEADS * LANES),
                     pl.BlockSpec((1, D_ATTN, rows), lambda i: (i, 0, 0)), pos_minor(N_HEADS),
                     row_spec(d_model), row_spec(d_model),
                     pl.BlockSpec((1, SUBLANES, d_conv), lambda i: (i // tiles_per_seq, 0, 0))]
        scratch = [pltpu.VMEM((rows + SUBLANES, d_conv), F32), pltpu.VMEM((1, LANES), F32)]
    kern = functools.partial(_inproj_kernel, rows=rows, tiles_per_seq=tiles_per_seq, sample=sample,
                             d_conv=d_conv, d_model=d_model)
    return pl.pallas_call(
        kern,
        grid=(n_tiles,),
        in_specs=in_specs,
        out_specs=out_specs,
        out_shape=out_shape,
        scratch_shapes=scratch,
        compiler_params=pltpu.CompilerParams(dimension_semantics=("arbitrary",),
                                             vmem_limit_bytes=VMEM_LIMIT_BYTES),
        name="inproj_sample" if sample else "inproj_prompt",
    )(*args)


def _attn_kernel(q_ref, k_ref, vt_ref, o_ref, m_sc, l_sc, acc_sc, *, tq, v_chunk, q_chunk):
    i = pl.program_id(2)
    chunks = tq // v_chunk
    for hl in range(2):
        m_sc[hl] = jnp.full((1, tq), MASK_VALUE, F32)
        l_sc[hl] = jnp.zeros((1, tq), F32)
        acc_sc[hl] = jnp.zeros((HEAD_DIM, tq), F32)

    def step(j, masked):
        start = pl.multiple_of(j * tq, tq)
        blocks = [(hl, qc) for hl in range(2) for qc in range(tq // q_chunk)]
        m_old = [m_sc[hl] for hl in range(2)]
        l_old = [l_sc[hl] for hl in range(2)]
        acc_old = [acc_sc[hl] for hl in range(2)]
        sts = [_dot_nt(k_ref[pl.ds(start, tq), hl * LANES:(hl + 1) * LANES],
                       q_ref[qc * q_chunk:(qc + 1) * q_chunk, hl * LANES:(hl + 1) * LANES]) for hl, qc in blocks]
        m_out, l_out, acc_out = [[], []], [[], []], [[], []]
        for (hl, qc), st in zip(blocks, sts):
            qs = slice(qc * q_chunk, (qc + 1) * q_chunk)
            if masked:
                key = lax.broadcasted_iota(jnp.int32, (tq, q_chunk), 0)
                qry = lax.broadcasted_iota(jnp.int32, (tq, q_chunk), 1) + qc * q_chunk
                st = jnp.where(key <= qry, st, MASK_VALUE)
            m_prev = m_old[hl][:, qs]
            m_new = jnp.maximum(m_prev, jnp.max(st, axis=0, keepdims=True))
            alpha = jnp.exp(m_prev - m_new)
            pt = jnp.exp(st - m_new)
            l_out[hl].append(alpha * l_old[hl][:, qs] + jnp.sum(pt, axis=0, keepdims=True))
            pb = pt.astype(BF16)
            acc = alpha * acc_old[hl][:, qs]
            for c in range(chunks):
                vt = vt_ref[j * chunks + c, hl * HEAD_DIM:(hl + 1) * HEAD_DIM, :]
                acc = acc + _dot(vt, pb[c * v_chunk:(c + 1) * v_chunk, :])
            acc_out[hl].append(acc)
            m_out[hl].append(m_new)
        for hl in range(2):
            m_sc[hl] = jnp.concatenate(m_out[hl], axis=1)
            l_sc[hl] = jnp.concatenate(l_out[hl], axis=1)
            acc_sc[hl] = jnp.concatenate(acc_out[hl], axis=1)

    def body(j, carry):
        step(j, False)
        return carry

    lax.fori_loop(0, i, body, 0)
    step(i, True)
    ot = jnp.concatenate([acc_sc[0] / l_sc[0], acc_sc[1] / l_sc[1]], axis=0)
    o_ref[...] = ot.T.astype(BF16)


def _prompt_attention(qa, ka, vtb, *, n_seq, seq_len):
    t = qa.shape[0]
    v_chunk = vtb.shape[2]
    tq = 512
    qt = seq_len // tq
    n_pairs = N_HEADS // 2
    return pl.pallas_call(
        functools.partial(_attn_kernel, tq=tq, v_chunk=v_chunk, q_chunk=LANES),
        grid=(n_seq, n_pairs, qt),
        in_specs=[
            pl.BlockSpec((tq, 2 * LANES), lambda n, hp, i: (n * qt + i, hp)),
            pl.BlockSpec((seq_len, 2 * LANES), lambda n, hp, i: (n, hp)),
            pl.BlockSpec((seq_len // v_chunk, 2 * HEAD_DIM, v_chunk), lambda n, hp, i: (n, hp, 0)),
        ],
        out_specs=pl.BlockSpec((tq, LANES), lambda n, hp, i: (n * qt + i, hp)),
        out_shape=jax.ShapeDtypeStruct((t, D_ATTN), BF16),
        scratch_shapes=[pltpu.VMEM((2, 1, tq), F32), pltpu.VMEM((2, 1, tq), F32),
                        pltpu.VMEM((2, HEAD_DIM, tq), F32)],
        compiler_params=pltpu.CompilerParams(dimension_semantics=("arbitrary", "arbitrary", "arbitrary"),
                                             vmem_limit_bytes=VMEM_LIMIT_BYTES),
        name="attn_prompt",
    )(qa, ka, vtb)


def _sample_attn_kernel(pt_ref, *refs, n_pages, n_new):
    del pt_ref
    q_ref = refs[0]
    k_refs = refs[1:1 + n_pages]
    v_refs = refs[1 + n_pages:1 + 2 * n_pages]
    f_refs = refs[1 + 2 * n_pages:1 + 3 * n_pages]
    kn_ref, vn_ref, fn_ref, e_ref, suffix_ref, later_ref, tri_ref = refs[1 + 3 * n_pages:8 + 3 * n_pages]
    o_ref = refs[8 + 3 * n_pages]
    pad_sc, kn_sc, vn_sc = refs[9 + 3 * n_pages:]
    n_rows = N_HEADS * n_new
    qb = q_ref[0]
    e = e_ref[...]

    pad_sc[...] = jnp.zeros((PAGE_SIZE, LANES), F32)
    pad_sc[0:n_new, 0:N_HEADS] = fn_ref[0]
    kn_sc[...] = jnp.zeros((PAGE_SIZE, D_ATTN), F32)
    kn_sc[0:n_new, :] = kn_ref[0]
    vn_sc[...] = jnp.zeros((PAGE_SIZE, D_ATTN), F32)
    vn_sc[0:n_new, :] = vn_ref[0]
    cn = _dot_exact_rhs(tri_ref[...], pad_sc[...])
    cn_rows = _dot_nt_exact_rhs(e, cn)
    row_t = lax.broadcasted_iota(jnp.int32, (n_rows, LANES), 0) // N_HEADS
    col = lax.broadcasted_iota(jnp.int32, (n_rows, LANES), 1)
    cnq = jnp.sum(jnp.where(col == row_t, cn_rows, 0.0), axis=1, keepdims=True)
    s_new = _dot_nt(qb, kn_sc[...].astype(BF16)) + (cnq - cn_rows)
    s_new = jnp.where(col <= row_t, s_new, MASK_VALUE)

    lf = jnp.concatenate([f_refs[j][0] for j in range(n_pages)], axis=0)
    incl = _dot_exact_lhs(lf, suffix_ref[...])
    tot = jnp.broadcast_to(incl[:, 0:1], incl.shape)
    d = incl - lf + _dot_exact_rhs(later_ref[...], tot)
    s_pages = []
    for j in range(n_pages):
        bias = jnp.concatenate([d[j * N_HEADS:(j + 1) * N_HEADS, :]] * n_new, axis=0)
        s_pages.append(_dot(qb, k_refs[j][0].astype(BF16)) + bias + cnq)

    m_blk = s_new
    for s in s_pages:
        m_blk = jnp.maximum(m_blk, s)
    m = jnp.max(m_blk, axis=1, keepdims=True)
    p_new = jnp.exp(s_new - m)
    l_blk = p_new
    acc = _dot(p_new.astype(BF16), vn_sc[...].astype(BF16))
    for j in range(n_pages):
        p = jnp.exp(s_pages[j] - m)
        l_blk = l_blk + p
        acc = acc + _dot_nt(p.astype(BF16), v_refs[j][0].astype(BF16))
    o_ref[0] = acc / jnp.sum(l_blk, axis=1, keepdims=True)


def _sample_attention(qblk, cache_kt, cache_vt, cache_lft, page_table, k_new, v_new, lf_new):
    n_seq, n_rows, _ = qblk.shape
    n_new = k_new.shape[1]
    n_pages = page_table.shape[1]
    pt = page_table.reshape(-1)
    row_head = jnp.arange(n_rows) % N_HEADS
    e = (row_head[:, None] == jnp.arange(LANES)[None, :]).astype(BF16)
    pos = jnp.arange(PAGE_SIZE)
    suffix = (pos[:, None] >= pos[None, :]).astype(BF16)
    tri = (pos[None, :] <= pos[:, None]).astype(BF16)
    r = jnp.arange(n_pages * N_HEADS)
    later = ((r[:, None] % N_HEADS == r[None, :] % N_HEADS)
             & (r[None, :] // N_HEADS > r[:, None] // N_HEADS)).astype(BF16)

    def page_map(j):
        return lambda b, pt_ref: (pt_ref[b * n_pages + j], 0, 0)

    seq_map = lambda b, pt_ref: (b, 0, 0)
    const_map = lambda b, pt_ref: (0, 0)
    in_specs = [pl.BlockSpec((1, n_rows, D_ATTN), seq_map)]
    in_specs += [pl.BlockSpec((1, D_ATTN, PAGE_SIZE), page_map(j)) for j in range(n_pages)]
    in_specs += [pl.BlockSpec((1, D_ATTN, PAGE_SIZE), page_map(j)) for j in range(n_pages)]
    in_specs += [pl.BlockSpec((1, N_HEADS, PAGE_SIZE), page_map(j)) for j in range(n_pages)]
    in_specs += [pl.BlockSpec((1, n_new, D_ATTN), seq_map), pl.BlockSpec((1, n_new, D_ATTN), seq_map),
                 pl.BlockSpec((1, n_new, N_HEADS), seq_map),
                 pl.BlockSpec(e.shape, const_map), pl.BlockSpec(suffix.shape, const_map),
                 pl.BlockSpec(later.shape, const_map), pl.BlockSpec(tri.shape, const_map)]
    grid_spec = pltpu.PrefetchScalarGridSpec(
        num_scalar_prefetch=1,
        grid=(n_seq,),
        in_specs=in_specs,
        out_specs=pl.BlockSpec((1, n_rows, D_ATTN), seq_map),
        scratch_shapes=[pltpu.VMEM((PAGE_SIZE, LANES), F32),
                        pltpu.VMEM((PAGE_SIZE, D_ATTN), F32), pltpu.VMEM((PAGE_SIZE, D_ATTN), F32)],
    )
    args = ([pt, qblk] + [cache_kt] * n_pages + [cache_vt] * n_pages + [cache_lft] * n_pages
            + [k_new, v_new, lf_new, e, suffix, later, tri])
    return pl.pallas_call(
        functools.partial(_sample_attn_kernel, n_pages=n_pages, n_new=n_new),
        grid_spec=grid_spec,
        out_shape=jax.ShapeDtypeStruct((n_seq, n_rows, D_ATTN), F32),
        compiler_params=pltpu.CompilerParams(dimension_semantics=("arbitrary",),
                                             vmem_limit_bytes=VMEM_LIMIT_BYTES),
        name="attn_sample",
    )(*args)


def _route(logits):
    rows = logits.shape[0]
    lane = lax.broadcasted_iota(jnp.int32, (rows, LANES), 1).astype(F32)
    big = float(LANES)
    in_g = lane < N_GROUPS
    gl = jnp.where(in_g, logits, MASK_VALUE)
    m_g = jnp.max(gl, axis=1, keepdims=True)
    z_g = jnp.sum(jnp.exp(gl - m_g), axis=1, keepdims=True)
    p_g = 1.0 / z_g
    g_idx = jnp.min(jnp.where(in_g & (gl == m_g), lane, big), axis=1, keepdims=True)
    lo = N_GROUPS + EXPERTS_PER_GROUP * g_idx
    sel = (lane >= lo) & (lane < lo + EXPERTS_PER_GROUP)
    el = jnp.where(sel, logits, MASK_VALUE)
    m_e = jnp.max(el, axis=1, keepdims=True)
    ex = jnp.exp(el - m_e)
    ep = ex / jnp.sum(ex, axis=1, keepdims=True)
    v1 = jnp.max(jnp.where(sel, ep, -1.0), axis=1, keepdims=True)
    i1 = jnp.min(jnp.where(sel & (ep == v1), lane, big), axis=1, keepdims=True)
    sel2 = sel & (lane != i1)
    v2 = jnp.max(jnp.where(sel2, ep, -1.0), axis=1, keepdims=True)
    i2 = jnp.min(jnp.where(sel2 & (ep == v2), lane, big), axis=1, keepdims=True)
    denom = v1 + v2
    w1 = p_g * (v1 / denom)
    w2 = p_g * (v2 / denom)
    gate4 = jnp.where(lane == i1 - lo, w1, 0.0) + jnp.where(lane == i2 - lo, w2, 0.0)
    return gate4, g_idx


def _post_kernel(o_ref, ap_ref, sgb_ref, x_ref, wao_ref, wo_ref, g_ref, b_ref, wrh_ref, wrl_ref, tri_ref,
                 x1e_ref, route_ref, totals_ref, count_sc, *, alpha, d_model):
    i = pl.program_id(0)
    rows = x_ref.shape[0]
    yb = _dot(o_ref[...], wao_ref[...])
    mix = _dot((ap_ref[...] + sgb_ref[...] * yb).astype(BF16), wo_ref[...])
    x1 = _layer_norm(alpha * x_ref[...] + mix, g_ref[...], b_ref[...])
    xh = x1.astype(BF16)
    xl = (x1 - xh.astype(F32)).astype(BF16)
    logits = _dot(xh, wrh_ref[...]) + _dot(xl, wrh_ref[...]) + _dot(xh, wrl_ref[...])
    gate4, g_idx = _route(logits)

    @pl.when(i == 0)
    def _():
        count_sc[...] = jnp.zeros((1, LANES), F32)
    lane = lax.broadcasted_iota(jnp.int32, (rows, LANES), 1).astype(F32)
    onehot = jnp.where(lane == g_idx, 1.0, 0.0)
    count = _dot(tri_ref[...], onehot.astype(BF16)) + count_sc[...]
    rank = jnp.sum(onehot * count, axis=1, keepdims=True) - 1.0
    count_sc[...] = count[rows - 1:rows, :]
    totals_ref[...] = jnp.broadcast_to(count[rows - 1:rows, :], (SUBLANES, LANES))
    route = (gate4 + jnp.where(lane == ROUTE_GROUP_LANE, g_idx, 0.0)
             + jnp.where(lane == ROUTE_RANK_LANE, rank, 0.0))
    route_ref[...] = route
    x1e_ref[:, 0:d_model] = x1
    x1e_ref[:, d_model:d_model + LANES] = route


def _post_attention(o, ap, sgb, x, w, *, alpha):
    t, d_model = x.shape
    rows = MOE_ROWS
    row_spec = lambda width: pl.BlockSpec((rows, width), lambda i: (i, 0))
    weights = [w['wao'], w['wo'], w['ln1_g'], w['ln1_b'], w['wr_hi'], w['wr_lo'], w['tri_moe']]
    return pl.pallas_call(
        functools.partial(_post_kernel, alpha=alpha, d_model=d_model),
        grid=(t // rows,),
        in_specs=[row_spec(D_ATTN), row_spec(d_model), row_spec(d_model), row_spec(d_model)]
                 + [_const_spec(a.shape) for a in weights],
        out_specs=[row_spec(d_model + LANES), row_spec(LANES),
                   pl.BlockSpec((SUBLANES, LANES), lambda i: (0, 0))],
        out_shape=[jax.ShapeDtypeStruct((t, d_model + LANES), F32),
                   jax.ShapeDtypeStruct((t, LANES), F32),
                   jax.ShapeDtypeStruct((SUBLANES, LANES), F32)],
        scratch_shapes=[pltpu.VMEM((1, LANES), F32)],
        compiler_params=pltpu.CompilerParams(dimension_semantics=("arbitrary",),
                                             vmem_limit_bytes=VMEM_LIMIT_BYTES),
        name="post_attn",
    )(o, ap, sgb, x, *weights)


def _row_permute_kernel(pos_ref, src_ref, *rest, n_rows, scatter):
    if scatter:
        _, dst_ref, sem = rest
    else:
        dst_ref, sem = rest

    def row_copy(t):
        p = pos_ref[t]
        if scatter:
            return pltpu.make_async_copy(src_ref.at[pl.ds(t, 1)], dst_ref.at[pl.ds(p, 1)], sem)
        return pltpu.make_async_copy(src_ref.at[pl.ds(p, 1)], dst_ref.at[pl.ds(t, 1)], sem)

    def chunk_wait():
        pltpu.make_async_copy(src_ref.at[pl.ds(0, PERMUTE_CHUNK)], dst_ref.at[pl.ds(0, PERMUTE_CHUNK)], sem).wait()

    def chunk_body(c, carry):
        def issue(r, inner):
            row_copy(c * PERMUTE_CHUNK + r).start()
            return inner
        lax.fori_loop(0, PERMUTE_CHUNK, issue, 0, unroll=PERMUTE_UNROLL)

        @pl.when(c > 0)
        def _():
            chunk_wait()
        return carry

    lax.fori_loop(0, n_rows // PERMUTE_CHUNK, chunk_body, 0)
    chunk_wait()


def _row_permute(pos, src, *, n_out, scatter):
    n_rows = pos.shape[0]
    width = src.shape[1]
    any_spec = pl.BlockSpec(memory_space=pl.ANY)
    args = [pos, src]
    in_specs = [any_spec]
    aliases = {}
    if scatter:
        args.append(jnp.zeros((n_out, width), src.dtype))
        in_specs.append(any_spec)
        aliases = {2: 0}
    grid_spec = pltpu.PrefetchScalarGridSpec(
        num_scalar_prefetch=1, grid=(1,), in_specs=in_specs, out_specs=any_spec,
        scratch_shapes=[pltpu.SemaphoreType.DMA(())])
    return pl.pallas_call(
        functools.partial(_row_permute_kernel, n_rows=n_rows, scatter=scatter),
        grid_spec=grid_spec,
        out_shape=jax.ShapeDtypeStruct((n_out, width), src.dtype),
        input_output_aliases=aliases,
        compiler_params=pltpu.CompilerParams(dimension_semantics=("arbitrary",)),
        name="row_scatter" if scatter else "row_gather",
    )(*args)


def _moe_kernel(tg_ref, nv_ref, xs_ref, w1_ref, w3_ref, w2_ref, g_ref, b_ref, y_ref, *, alpha, d_model):
    del tg_ref
    i = pl.program_id(0)

    @pl.when(i < nv_ref[0])
    def _():
        x = xs_ref[:, 0:d_model]
        route = xs_ref[:, d_model:d_model + LANES]
        xb = x.astype(BF16)
        lane = lax.broadcasted_iota(jnp.int32, route.shape, 1)
        acc = jnp.zeros_like(x)
        for e in range(EXPERTS_PER_GROUP):
            h1 = _dot(xb, w1_ref[0, e])
            h3 = _dot(xb, w3_ref[0, e])
            h = (h1 * jax.nn.sigmoid(h1)) * h3
            ye = _dot(h.astype(BF16), w2_ref[0, e])
            ge = jnp.sum(jnp.where(lane == e, route, 0.0), axis=1, keepdims=True)
            acc = acc + ge * ye
        y_ref[...] = _layer_norm(alpha * x + acc, g_ref[...], b_ref[...])

    @pl.when(i >= nv_ref[0])
    def _():
        y_ref[...] = jnp.zeros_like(y_ref)


def _experts(x1e, route, totals, w, *, alpha):
    t = x1e.shape[0]
    d_model = x1e.shape[1] - LANES
    _, _, _, d_exp = w['w1'].shape
    rows = MOE_ROWS
    n_tiles = t // rows + N_GROUPS
    group = route[:, ROUTE_GROUP_LANE].astype(jnp.int32)
    rank = route[:, ROUTE_RANK_LANE].astype(jnp.int32)
    count = totals[0, 0:N_GROUPS].astype(jnp.int32)
    tiles_g = (count + rows - 1) // rows
    tile_end = jnp.cumsum(tiles_g)
    pos = (tile_end - tiles_g)[group] * rows + rank
    n_valid = tile_end[N_GROUPS - 1:N_GROUPS]
    tile_group = jnp.minimum(jnp.sum(jnp.arange(n_tiles)[:, None] >= tile_end[None, :], axis=1),
                             N_GROUPS - 1).astype(jnp.int32)

    xs = _row_permute(pos, x1e, n_out=n_tiles * rows, scatter=True)

    live = lambda i, nv: jnp.minimum(i, nv[0] - 1)
    w_spec = lambda a: pl.BlockSpec((1,) + a.shape[1:], lambda i, tg, nv: (tg[live(i, nv)], 0, 0, 0))
    grid_spec = pltpu.PrefetchScalarGridSpec(
        num_scalar_prefetch=2,
        grid=(n_tiles,),
        in_specs=[
            pl.BlockSpec((rows, d_model + LANES), lambda i, tg, nv: (live(i, nv), 0)),
            w_spec(w['w1']), w_spec(w['w3']), w_spec(w['w2']),
            pl.BlockSpec((1, d_model), lambda i, tg, nv: (0, 0)),
            pl.BlockSpec((1, d_model), lambda i, tg, nv: (0, 0)),
        ],
        out_specs=pl.BlockSpec((rows, d_model), lambda i, tg, nv: (i, 0)),
    )
    ys = pl.pallas_call(
        functools.partial(_moe_kernel, alpha=alpha, d_model=d_model),
        grid_spec=grid_spec,
        out_shape=jax.ShapeDtypeStruct((n_tiles * rows, d_model), F32),
        compiler_params=pltpu.CompilerParams(dimension_semantics=("arbitrary",),
                                             vmem_limit_bytes=VMEM_LIMIT_BYTES),
        name="experts",
    )(tile_group, n_valid, xs, w['w1'], w['w3'], w['w2'], w['ln2_g'], w['ln2_b'])
    return _row_permute(pos, ys, n_out=t, scatter=False)


def _prepare_weights(w_in, b_f, conv_w, w_conv_out, w_attn_out, w_o, ln1_g, ln1_b,
                     w_group, w_router, w1, w3, w2, ln2_g, ln2_b, rows_inproj):
    d_model = w_in.shape[0]
    d_conv = w_conv_out.shape[0]
    o_q = 3 * d_conv
    o_k = o_q + D_ATTN
    o_v = o_k + D_ATTN
    o_f = o_v + D_ATTN
    o_g = o_f + N_HEADS
    w_bf = w_in.astype(BF16)
    wm = jnp.concatenate([w_bf[:, 0:o_q], w_bf[:, o_k:o_f]], axis=1)
    wq = jnp.pad(w_bf[:, o_q:o_k].reshape(d_model, N_HEADS, HEAD_DIM),
                 ((0, 0), (0, 0), (0, LANES - HEAD_DIM))).reshape(d_model, N_HEADS * LANES)
    wf = jnp.pad(w_bf[:, o_f:o_g], ((0, 0), (0, LANES - N_HEADS)))
    wg = w_bf[:, o_g:]
    bf = jnp.pad(b_f.astype(F32), (0, LANES - N_HEADS)).reshape(1, LANES)

    col = jnp.arange(N_HEADS * LANES)
    col_head, col_lane = col // LANES, col % LANES
    src = jnp.arange(D_ATTN)
    sel = ((src[:, None] // HEAD_DIM == col_head[None, :]) & (src[:, None] % HEAD_DIM == col_lane[None, :]))
    part = jnp.arange(AUG_SPLITS * LANES)
    part_k, part_h = part // LANES, part % LANES
    pq = (part_h[:, None] == col_head[None, :]) & (col_lane[None, :] == AUG_Q_C + part_k[:, None])
    pk = (part_h[:, None] == col_head[None, :]) & (col_lane[None, :] == AUG_Q_C + AUG_SPLITS + part_k[:, None])
    oq = ((col_lane >= AUG_Q_C + AUG_SPLITS) & (col_lane < AUG_Q_C + 2 * AUG_SPLITS)).astype(F32).reshape(1, -1)
    ok = ((col_lane >= AUG_Q_C) & (col_lane < AUG_Q_C + AUG_SPLITS)).astype(F32).reshape(1, -1)
    r = jnp.arange(rows_inproj)
    tri = (r[None, :] <= r[:, None]).astype(BF16)
    r = jnp.arange(MOE_ROWS)
    tri_moe = (r[None, :] <= r[:, None]).astype(BF16)

    wr = jnp.pad(jnp.concatenate([w_group, w_router], axis=1), ((0, 0), (0, LANES - N_GROUPS - N_EXPERTS)))
    wr_hi = wr.astype(BF16)
    wr_lo = (wr - wr_hi.astype(F32)).astype(BF16)
    return dict(
        wm=wm, wq=wq, wf=wf, wg=wg, bf=bf, cw=conv_w.astype(F32), wco=w_conv_out.astype(BF16),
        tri=tri, sel=sel.astype(BF16), pq=pq.astype(BF16), pk=pk.astype(BF16), oq=oq, ok=ok,
        wao=w_attn_out.astype(BF16), wo=w_o.astype(BF16),
        ln1_g=ln1_g.reshape(1, -1), ln1_b=ln1_b.reshape(1, -1), wr_hi=wr_hi, wr_lo=wr_lo,
        tri_moe=tri_moe,
        w1=w1.astype(BF16).reshape((N_GROUPS, EXPERTS_PER_GROUP) + w1.shape[1:]),
        w3=w3.astype(BF16).reshape((N_GROUPS, EXPERTS_PER_GROUP) + w3.shape[1:]),
        w2=w2.astype(BF16).reshape((N_GROUPS, EXPERTS_PER_GROUP) + w2.shape[1:]),
        ln2_g=ln2_g.reshape(1, -1), ln2_b=ln2_b.reshape(1, -1),
    )


def kernel(x_prompt, x_sample, cache_k, cache_v, cache_logf, state_conv, page_table, w_in, b_f, conv_w,
           w_conv_out, w_attn_out, w_o, ln1_g, ln1_b, w_group, w_router, w1, w3, w2, ln2_g, ln2_b):
    depth = w_in.shape[0]
    assert depth == 1, "single-layer stack"
    n_p, s_p, d_model = x_prompt.shape
    n_s, s_s, _ = x_sample.shape
    d_conv = w_conv_out.shape[1]
    alpha = (2 * depth) ** 0.25
    w = _prepare_weights(w_in[0], b_f[0], conv_w[0], w_conv_out[0], w_attn_out[0], w_o[0], ln1_g[0], ln1_b[0],
                         w_group[0], w_router[0], w1[0], w3[0], w2[0], ln2_g[0], ln2_b[0], rows_inproj=256)

    xp = x_prompt.reshape(n_p * s_p, d_model)
    qa, kt_p, vt_p, ka, vtb, lft_p, ap, sgb, tail = _inproj(xp, w, sample=False, seq_len=s_p)
    o_p = _prompt_attention(qa, ka, vtb, n_seq=n_p, seq_len=s_p)
    x1e_p, route_p, totals_p = _post_attention(o_p, ap, sgb, xp, w, alpha=alpha)
    y_p = _experts(x1e_p, route_p, totals_p, w, alpha=alpha)

    xs = x_sample.reshape(n_s * s_s, d_model)
    state = state_conv[0]
    s0 = jnp.repeat(state[:, 0, :], s_s, axis=0)
    s1 = jnp.repeat(state[:, 1, :], s_s, axis=0)
    q_s, k_s, v_s, lf_s, ap_s, sgb_s, u_s = _inproj(xs, w, sample=True, seq_len=s_s, state=(s0, s1))
    q5 = q_s.reshape(n_s, s_s, N_HEADS, LANES)[..., :HEAD_DIM]
    eye_h = jnp.eye(N_HEADS, dtype=BF16)
    qblk = (q5[:, :, :, None, :] * eye_h[None, None, :, :, None]).reshape(n_s, s_s * N_HEADS, D_ATTN)
    n_pool = cache_k.shape[1]
    cache_kt = cache_k[0].transpose(0, 2, 3, 1).reshape(n_pool, D_ATTN, PAGE_SIZE)
    cache_vt = cache_v[0].transpose(0, 2, 3, 1).reshape(n_pool, D_ATTN, PAGE_SIZE)
    cache_lft = cache_logf[0].transpose(0, 2, 1)
    o_blk = _sample_attention(
        qblk, cache_kt, cache_vt, cache_lft, page_table, k_s.reshape(n_s, s_s, D_ATTN),
        v_s.reshape(n_s, s_s, D_ATTN), lf_s.reshape(n_s, s_s, N_HEADS))
    o5 = o_blk.reshape(n_s, s_s, N_HEADS, N_HEADS, HEAD_DIM)
    o_s = jnp.einsum('bthgd,hg->bthd', o5, jnp.eye(N_HEADS, dtype=F32)).reshape(n_s * s_s, D_ATTN).astype(BF16)
    x1e_s, route_s, totals_s = _post_attention(o_s, ap_s, sgb_s, xs, w, alpha=alpha)
    y_s = _experts(x1e_s, route_s, totals_s, w, alpha=alpha)

    return (
        y_p.reshape(n_p, s_p, d_model),
        y_s.reshape(n_s, s_s, d_model),
        kt_p.reshape(1, n_p, N_HEADS, HEAD_DIM, s_p).transpose(0, 1, 4, 2, 3),
        vt_p.reshape(1, n_p, N_HEADS, HEAD_DIM, s_p).transpose(0, 1, 4, 2, 3),
        lft_p.reshape(1, n_p, N_HEADS, s_p).transpose(0, 1, 3, 2),
        tail[:, SUBLANES - (CONV_WIDTH - 1):, :].reshape(1, n_p, CONV_WIDTH - 1, d_conv),
        k_s.reshape(1, n_s, s_s, N_HEADS, HEAD_DIM),
        v_s.reshape(1, n_s, s_s, N_HEADS, HEAD_DIM),
        lf_s.reshape(1, n_s, s_s, N_HEADS),
        u_s.reshape(n_s, s_s, d_conv)[:, s_s - (CONV_WIDTH - 1):, :].reshape(1, n_s, CONV_WIDTH - 1, d_conv),
    )
```

```python
import functools

import jax
import jax.numpy as jnp
from jax import lax
from jax.experimental import pallas as pl
from jax.experimental.pallas import tpu as pltpu

F32 = jnp.float32
BF16 = jnp.bfloat16

LANES = 128
SUBLANES = 8
VMEM_LIMIT_BYTES = 56 * 1024 * 1024

CONV_WIDTH = 3
N_HEADS = 8
HEAD_DIM = 64
D_ATTN = N_HEADS * HEAD_DIM
N_GROUPS = 4
EXPERTS_PER_GROUP = 4
N_EXPERTS = N_GROUPS * EXPERTS_PER_GROUP
LN_EPS = 1e-5
PAGE_SIZE = 128
MASK_VALUE = -1e30
AUG_Q_C = HEAD_DIM
AUG_SPLITS = 3
MOE_ROWS = 512
ROUTE_GROUP_LANE = EXPERTS_PER_GROUP
ROUTE_RANK_LANE = EXPERTS_PER_GROUP + 1
PERMUTE_UNROLL = 8


def _dot(a, b):
    return jnp.dot(a, b, preferred_element_type=F32)


def _dot_nt(a, b):
    return lax.dot_general(a, b, (((1,), (1,)), ((), ())), preferred_element_type=F32)


def _split3(x):
    h1 = x.astype(BF16)
    r1 = x - h1.astype(F32)
    h2 = r1.astype(BF16)
    r2 = r1 - h2.astype(F32)
    h3 = r2.astype(BF16)
    return h1, h2, h3


def _dot_exact_rhs(a_bf16, x):
    h1, h2, h3 = _split3(x)
    return _dot(a_bf16, h1) + _dot(a_bf16, h2) + _dot(a_bf16, h3)


def _dot_exact_lhs(x, b_bf16):
    h1, h2, h3 = _split3(x)
    return _dot(h1, b_bf16) + _dot(h2, b_bf16) + _dot(h3, b_bf16)


def _dot_nt_exact_rhs(a_bf16, x):
    h1, h2, h3 = _split3(x)
    return _dot_nt(a_bf16, h1) + _dot_nt(a_bf16, h2) + _dot_nt(a_bf16, h3)


def _log_sigmoid(x):
    return jnp.minimum(x, 0.0) - jnp.log1p(jnp.exp(-jnp.abs(x)))


def _layer_norm(r, g, b):
    mu = jnp.mean(r, axis=-1, keepdims=True)
    d = r - mu
    var = jnp.mean(d * d, axis=-1, keepdims=True)
    return d * lax.rsqrt(var + LN_EPS) * g + b


def _const_spec(shape):
    nd = len(shape)
    return pl.BlockSpec(shape, lambda *_: (0,) * nd, pipeline_mode=pl.Buffered(1))


def _inproj_kernel(*refs, rows, tiles_per_seq, sample, d_conv, d_model):
    if sample:
        (x_ref, wm_ref, wq_ref, wf_ref, wg_ref, bf_ref, cw_ref, wco_ref, s0_ref, s1_ref,
         q_ref, k_ref, v_ref, lf_ref, ap_ref, sgb_ref, u_ref, ubuf) = refs
    else:
        (x_ref, wm_ref, wq_ref, wf_ref, wg_ref, bf_ref, cw_ref, wco_ref, tri_ref, sel_ref, pq_ref, pk_ref,
         oq_ref, ok_ref,
         q_ref, k_ref, v_ref, ka_ref, vb_ref, lf_ref, ap_ref, sgb_ref, tail_ref, ubuf, ccarry) = refs
    i = pl.program_id(0)
    xb = x_ref[...].astype(BF16)

    xc = _dot(xb, wm_ref[:, 0:d_conv])
    bg = _dot(xb, wm_ref[:, d_conv:2 * d_conv])
    cg = _dot(xb, wm_ref[:, 2 * d_conv:3 * d_conv])
    u = cg * xc
    if sample:
        ubuf[0:SUBLANES, :] = jnp.zeros((SUBLANES, d_conv), F32)
    else:
        @pl.when(i % tiles_per_seq == 0)
        def _():
            ubuf[0:SUBLANES, :] = jnp.zeros((SUBLANES, d_conv), F32)
    ubuf[SUBLANES:SUBLANES + rows, :] = u
    prev1 = ubuf[SUBLANES - 1:SUBLANES - 1 + rows, :]
    prev2 = ubuf[SUBLANES - 2:SUBLANES - 2 + rows, :]
    if sample:
        pos = lax.broadcasted_iota(jnp.int32, (rows, 1), 0) % 4
        prev1 = jnp.where(pos == 0, s1_ref[...], prev1)
        prev2 = jnp.where(pos == 0, s0_ref[...], jnp.where(pos == 1, s1_ref[...], prev2))
        u_ref[...] = u
    else:
        ubuf[0:SUBLANES, :] = u[rows - SUBLANES:rows, :]

        @pl.when(i % tiles_per_seq == tiles_per_seq - 1)
        def _():
            tail_ref[0] = u[rows - SUBLANES:rows, :]
    conv = cw_ref[0:1, :] * prev2 + cw_ref[1:2, :] * prev1 + cw_ref[2:3, :] * u
    ya = _dot((bg * conv).astype(BF16), wco_ref[...])

    k = _dot(xb, wm_ref[:, 3 * d_conv:3 * d_conv + D_ATTN])
    v = _dot(xb, wm_ref[:, 3 * d_conv + D_ATTN:3 * d_conv + 2 * D_ATTN])
    qa = _dot(xb, wq_ref[...]) * (HEAD_DIM ** -0.5)
    fg = _dot(xb, wf_ref[...]) + bf_ref[...]
    lane = lax.broadcasted_iota(jnp.int32, (rows, LANES), 1)
    logf = jnp.where(lane < N_HEADS, _log_sigmoid(fg), 0.0)
    if sample:
        k_ref[...] = k
        v_ref[...] = v
        lf_ref[...] = logf[:, 0:N_HEADS]
        q_ref[...] = qa.astype(BF16)
    else:
        k_ref[0] = k.T
        vt = v.T
        v_ref[0] = vt
        vb_ref[0] = vt.astype(BF16)
        lf_ref[0] = logf.T[0:N_HEADS, :]

        @pl.when(i % tiles_per_seq == 0)
        def _():
            ccarry[...] = jnp.zeros((1, LANES), F32)
        c = _dot_exact_rhs(tri_ref[...], logf) + ccarry[...]
        ccarry[...] = c[rows - 1:rows, :]
        c1, c2, c3 = _split3(c)
        c_parts = jnp.concatenate([c1, c2, c3], axis=1)
        q_ref[...] = (qa + _dot(c_parts, pq_ref[...]) + oq_ref[...]).astype(BF16)
        ka = _dot(k.astype(BF16), sel_ref[...]) - _dot(c_parts, pk_ref[...]) + ok_ref[...]
        ka_ref[...] = ka.astype(BF16)

    ga = _dot(xb, wg_ref[:, 0:d_model])
    gb = _dot(xb, wg_ref[:, d_model:2 * d_model])
    ap_ref[...] = jax.nn.sigmoid(ga) * ya
    sgb_ref[...] = jax.nn.sigmoid(gb)


def _inproj(x, w, *, sample, seq_len, state=None):
    t, d_model = x.shape
    d_conv = w['wco'].shape[0]
    rows = 256
    tiles_per_seq = max(seq_len // rows, 1)
    n_tiles = t // rows
    row_spec = lambda width: pl.BlockSpec((rows, width), lambda i: (i, 0))
    weights = [w['wm'], w['wq'], w['wf'], w['wg'], w['bf'], w['cw'], w['wco']]
    in_specs = [row_spec(d_model)] + [_const_spec(a.shape) for a in weights]
    args = [x] + weights
    if sample:
        args += [state[0], state[1]]
        in_specs += [row_spec(d_conv), row_spec(d_conv)]
        out_shape = [
            jax.ShapeDtypeStruct((t, N_HEADS * LANES), BF16),
            jax.ShapeDtypeStruct((t, D_ATTN), F32),
            jax.ShapeDtypeStruct((t, D_ATTN), F32),
            jax.ShapeDtypeStruct((t, N_HEADS), F32),
            jax.ShapeDtypeStruct((t, d_model), F32),
            jax.ShapeDtypeStruct((t, d_model), F32),
            jax.ShapeDtypeStruct((t, d_conv), F32),
        ]
        out_specs = [row_spec(N_HEADS * LANES), row_spec(D_ATTN), row_spec(D_ATTN), row_spec(N_HEADS),
                     row_spec(d_model), row_spec(d_model), row_spec(d_conv)]
        scratch = [pltpu.VMEM((rows + SUBLANES, d_conv), F32)]
    else:
        consts = [w['tri'], w['sel'], w['pq'], w['pk'], w['oq'], w['ok']]
        args += consts
        in_specs += [_const_spec(a.shape) for a in consts]
        n_seq = t // seq_len
        out_shape = [
            jax.ShapeDtypeStruct((t, N_HEADS * LANES), BF16),
            jax.ShapeDtypeStruct((n_seq, D_ATTN, seq_len), F32),
            jax.ShapeDtypeStruct((n_seq, D_ATTN, seq_len), F32),
            jax.ShapeDtypeStruct((t, N_HEADS * LANES), BF16),
            jax.ShapeDtypeStruct((n_tiles, D_ATTN, rows), BF16),
            jax.ShapeDtypeStruct((n_seq, N_HEADS, seq_len), F32),
            jax.ShapeDtypeStruct((t, d_model), F32),
            jax.ShapeDtypeStruct((t, d_model), F32),
            jax.ShapeDtypeStruct((n_seq, SUBLANES, d_conv), F32),
        ]
        pos_minor = lambda height: pl.BlockSpec((1, height, rows),
                                                lambda i: (i // tiles_per_seq, 0, i % tiles_per_seq))
        out_specs = [row_spec(N_HEADS * LANES), pos_minor(D_ATTN), pos_minor(D_ATTN), row_spec(N_HEADS * LANES),
                     pl.BlockSpec((1, D_ATTN, rows), lambda i: (i, 0, 0)), pos_minor(N_HEADS),
                     row_spec(d_model), row_spec(d_model),
                     pl.BlockSpec((1, SUBLANES, d_conv), lambda i: (i // tiles_per_seq, 0, 0))]
        scratch = [pltpu.VMEM((rows + SUBLANES, d_conv), F32), pltpu.VMEM((1, LANES), F32)]
    kern = functools.partial(_inproj_kernel, rows=rows, tiles_per_seq=tiles_per_seq, sample=sample,
                             d_conv=d_conv, d_model=d_model)
    return pl.pallas_call(
        kern,
        grid=(n_tiles,),
        in_specs=in_specs,
        out_specs=out_specs,
        out_shape=out_shape,
        scratch_shapes=scratch,
        compiler_params=pltpu.CompilerParams(dimension_semantics=("arbitrary",),
                                             vmem_limit_bytes=VMEM_LIMIT_BYTES),
        name="inproj_sample" if sample else "inproj_prompt",
    )(*args)


def _attn_kernel(q_ref, k_ref, vt_ref, o_ref, m_sc, l_sc, acc_sc, *, tq, v_chunk, q_chunk):
    i = pl.program_id(2)
    chunks = tq // v_chunk
    for hl in range(2):
        m_sc[hl] = jnp.full((1, tq), MASK_VALUE, F32)
        l_sc[hl] = jnp.zeros((1, tq), F32)
        acc_sc[hl] = jnp.zeros((HEAD_DIM, tq), F32)

    def step(j, masked):
        start = pl.multiple_of(j * tq, tq)
        blocks = [(hl, qc) for hl in range(2) for qc in range(tq // q_chunk)]
        m_old = [m_sc[hl] for hl in range(2)]
        l_old = [l_sc[hl] for hl in range(2)]
        acc_old = [acc_sc[hl] for hl in range(2)]
        sts = [_dot_nt(k_ref[pl.ds(start, tq), hl * LANES:(hl + 1) * LANES],
                       q_ref[qc * q_chunk:(qc + 1) * q_chunk, hl * LANES:(hl + 1) * LANES]) for hl, qc in blocks]
        m_out, l_out, acc_out = [[], []], [[], []], [[], []]
        for (hl, qc), st in zip(blocks, sts):
            qs = slice(qc * q_chunk, (qc + 1) * q_chunk)
            if masked:
                key = lax.broadcasted_iota(jnp.int32, (tq, q_chunk), 0)
                qry = lax.broadcasted_iota(jnp.int32, (tq, q_chunk), 1) + qc * q_chunk
                st = jnp.where(key <= qry, st, MASK_VALUE)
            m_prev = m_old[hl][:, qs]
            m_new = jnp.maximum(m_prev, jnp.max(st, axis=0, keepdims=True))
            alpha = jnp.exp(m_prev - m_new)
            pt = jnp.exp(st - m_new)
            l_out[hl].append(alpha * l_old[hl][:, qs] + jnp.sum(pt, axis=0, keepdims=True))
            pb = pt.astype(BF16)
            acc = alpha * acc_old[hl][:, qs]
            for c in range(chunks):
                vt = vt_ref[j * chunks + c, hl * HEAD_DIM:(hl + 1) * HEAD_DIM, :]
                acc = acc + _dot(vt, pb[c * v_chunk:(c + 1) * v_chunk, :])
            acc_out[hl].append(acc)
            m_out[hl].append(m_new)
        for hl in range(2):
            m_sc[hl] = jnp.concatenate(m_out[hl], axis=1)
            l_sc[hl] = jnp.concatenate(l_out[hl], axis=1)
            acc_sc[hl] = jnp.concatenate(acc_out[hl], axis=1)

    def body(j, carry):
        step(j, False)
        return carry

    lax.fori_loop(0, i, body, 0)
    step(i, True)
    ot = jnp.concatenate([acc_sc[0] / l_sc[0], acc_sc[1] / l_sc[1]], axis=0)
    o_ref[...] = ot.T.astype(BF16)


def _prompt_attention(qa, ka, vtb, *, n_seq, seq_len):
    t = qa.shape[0]
    v_chunk = vtb.shape[2]
    tq = 512
    qt = seq_len // tq
    n_pairs = N_HEADS // 2
    return pl.pallas_call(
        functools.partial(_attn_kernel, tq=tq, v_chunk=v_chunk, q_chunk=LANES),
        grid=(n_seq, n_pairs, qt),
        in_specs=[
            pl.BlockSpec((tq, 2 * LANES), lambda n, hp, i: (n * qt + i, hp)),
            pl.BlockSpec((seq_len, 2 * LANES), lambda n, hp, i: (n, hp)),
            pl.BlockSpec((seq_len // v_chunk, 2 * HEAD_DIM, v_chunk), lambda n, hp, i: (n, hp, 0)),
        ],
        out_specs=pl.BlockSpec((tq, LANES), lambda n, hp, i: (n * qt + i, hp)),
        out_shape=jax.ShapeDtypeStruct((t, D_ATTN), BF16),
        scratch_shapes=[pltpu.VMEM((2, 1, tq), F32), pltpu.VMEM((2, 1, tq), F32),
                        pltpu.VMEM((2, HEAD_DIM, tq), F32)],
        compiler_params=pltpu.CompilerParams(dimension_semantics=("arbitrary", "arbitrary", "arbitrary"),
                                             vmem_limit_bytes=VMEM_LIMIT_BYTES),
        name="attn_prompt",
    )(qa, ka, vtb)


def _sample_attn_kernel(pt_ref, *refs, n_pages, n_new):
    del pt_ref
    q_ref = refs[0]
    k_refs = refs[1:1 + n_pages]
    v_refs = refs[1 + n_pages:1 + 2 * n_pages]
    f_refs = refs[1 + 2 * n_pages:1 + 3 * n_pages]
    kn_ref, vn_ref, fn_ref, e_ref, suffix_ref, later_ref, tri_ref = refs[1 + 3 * n_pages:8 + 3 * n_pages]
    o_ref = refs[8 + 3 * n_pages]
    pad_sc, kn_sc, vn_sc = refs[9 + 3 * n_pages:]
    n_rows = N_HEADS * n_new
    qb = q_ref[0]
    e = e_ref[...]

    pad_sc[...] = jnp.zeros((PAGE_SIZE, LANES), F32)
    pad_sc[0:n_new, 0:N_HEADS] = fn_ref[0]
    kn_sc[...] = jnp.zeros((PAGE_SIZE, D_ATTN), F32)
    kn_sc[0:n_new, :] = kn_ref[0]
    vn_sc[...] = jnp.zeros((PAGE_SIZE, D_ATTN), F32)
    vn_sc[0:n_new, :] = vn_ref[0]
    cn = _dot_exact_rhs(tri_ref[...], pad_sc[...])
    cn_rows = _dot_nt_exact_rhs(e, cn)
    row_t = lax.broadcasted_iota(jnp.int32, (n_rows, LANES), 0) // N_HEADS
    col = lax.broadcasted_iota(jnp.int32, (n_rows, LANES), 1)
    cnq = jnp.sum(jnp.where(col == row_t, cn_rows, 0.0), axis=1, keepdims=True)
    s_new = _dot_nt(qb, kn_sc[...].astype(BF16)) + (cnq - cn_rows)
    s_new = jnp.where(col <= row_t, s_new, MASK_VALUE)

    lf = jnp.concatenate([f_refs[j][0] for j in range(n_pages)], axis=0)
    incl = _dot_exact_lhs(lf, suffix_ref[...])
    tot = jnp.broadcast_to(incl[:, 0:1], incl.shape)
    d = incl - lf + _dot_exact_rhs(later_ref[...], tot)
    s_pages = []
    for j in range(n_pages):
        bias = jnp.concatenate([d[j * N_HEADS:(j + 1) * N_HEADS, :]] * n_new, axis=0)
        s_pages.append(_dot(qb, k_refs[j][0].astype(BF16)) + bias + cnq)

    m_blk = s_new
    for s in s_pages:
        m_blk = jnp.maximum(m_blk, s)
    m = jnp.max(m_blk, axis=1, keepdims=True)
    p_new = jnp.exp(s_new - m)
    l_blk = p_new
    acc = _dot(p_new.astype(BF16), vn_sc[...].astype(BF16))
    for j in range(n_pages):
        p = jnp.exp(s_pages[j] - m)
        l_blk = l_blk + p
        acc = acc + _dot_nt(p.astype(BF16), v_refs[j][0].astype(BF16))
    o_ref[0] = acc / jnp.sum(l_blk, axis=1, keepdims=True)


def _sample_attention(qblk, cache_kt, cache_vt, cache_lft, page_table, k_new, v_new, lf_new):
    n_seq, n_rows, _ = qblk.shape
    n_new = k_new.shape[1]
    n_pages = page_table.shape[1]
    pt = page_table.reshape(-1)
    row_head = jnp.arange(n_rows) % N_HEADS
    e = (row_head[:, None] == jnp.arange(LANES)[None, :]).astype(BF16)
    pos = jnp.arange(PAGE_SIZE)
    suffix = (pos[:, None] >= pos[None, :]).astype(BF16)
    tri = (pos[None, :] <= pos[:, None]).astype(BF16)
    r = jnp.arange(n_pages * N_HEADS)
    later = ((r[:, None] % N_HEADS == r[None, :] % N_HEADS)
             & (r[None, :] // N_HEADS > r[:, None] // N_HEADS)).astype(BF16)

    def page_map(j):
        return lambda b, pt_ref: (pt_ref[b * n_pages + j], 0, 0)

    seq_map = lambda b, pt_ref: (b, 0, 0)
    const_map = lambda b, pt_ref: (0, 0)
    in_specs = [pl.BlockSpec((1, n_rows, D_ATTN), seq_map)]
    in_specs += [pl.BlockSpec((1, D_ATTN, PAGE_SIZE), page_map(j)) for j in range(n_pages)]
    in_specs += [pl.BlockSpec((1, D_ATTN, PAGE_SIZE), page_map(j)) for j in range(n_pages)]
    in_specs += [pl.BlockSpec((1, N_HEADS, PAGE_SIZE), page_map(j)) for j in range(n_pages)]
    in_specs += [pl.BlockSpec((1, n_new, D_ATTN), seq_map), pl.BlockSpec((1, n_new, D_ATTN), seq_map),
                 pl.BlockSpec((1, n_new, N_HEADS), seq_map),
                 pl.BlockSpec(e.shape, const_map), pl.BlockSpec(suffix.shape, const_map),
                 pl.BlockSpec(later.shape, const_map), pl.BlockSpec(tri.shape, const_map)]
    grid_spec = pltpu.PrefetchScalarGridSpec(
        num_scalar_prefetch=1,
        grid=(n_seq,),
        in_specs=in_specs,
        out_specs=pl.BlockSpec((1, n_rows, D_ATTN), seq_map),
        scratch_shapes=[pltpu.VMEM((PAGE_SIZE, LANES), F32),
                        pltpu.VMEM((PAGE_SIZE, D_ATTN), F32), pltpu.VMEM((PAGE_SIZE, D_ATTN), F32)],
    )
    args = ([pt, qblk] + [cache_kt] * n_pages + [cache_vt] * n_pages + [cache_lft] * n_pages
            + [k_new, v_new, lf_new, e, suffix, later, tri])
    return pl.pallas_call(
        functools.partial(_sample_attn_kernel, n_pages=n_pages, n_new=n_new),
        grid_spec=grid_spec,
        out_shape=jax.ShapeDtypeStruct((n_seq, n_rows, D_ATTN), F32),
        compiler_params=pltpu.CompilerParams(dimension_semantics=("arbitrary",),
                                             vmem_limit_bytes=VMEM_LIMIT_BYTES),
        name="attn_sample",
    )(*args)


def _route(logits):
    rows = logits.shape[0]
    lane = lax.broadcasted_iota(jnp.int32, (rows, LANES), 1).astype(F32)
    big = float(LANES)
    in_g = lane < N_GROUPS
    gl = jnp.where(in_g, logits, MASK_VALUE)
    m_g = jnp.max(gl, axis=1, keepdims=True)
    z_g = jnp.sum(jnp.exp(gl - m_g), axis=1, keepdims=True)
    p_g = 1.0 / z_g
    g_idx = jnp.min(jnp.where(in_g & (gl == m_g), lane, big), axis=1, keepdims=True)
    lo = N_GROUPS + EXPERTS_PER_GROUP * g_idx
    sel = (lane >= lo) & (lane < lo + EXPERTS_PER_GROUP)
    el = jnp.where(sel, logits, MASK_VALUE)
    m_e = jnp.max(el, axis=1, keepdims=True)
    ex = jnp.exp(el - m_e)
    ep = ex / jnp.sum(ex, axis=1, keepdims=True)
    v1 = jnp.max(jnp.where(sel, ep, -1.0), axis=1, keepdims=True)
    i1 = jnp.min(jnp.where(sel & (ep == v1), lane, big), axis=1, keepdims=True)
    sel2 = sel & (lane != i1)
    v2 = jnp.max(jnp.where(sel2, ep, -1.0), axis=1, keepdims=True)
    i2 = jnp.min(jnp.where(sel2 & (ep == v2), lane, big), axis=1, keepdims=True)
    denom = v1 + v2
    w1 = p_g * (v1 / denom)
    w2 = p_g * (v2 / denom)
    gate4 = jnp.where(lane == i1 - lo, w1, 0.0) + jnp.where(lane == i2 - lo, w2, 0.0)
    return gate4, g_idx


def _post_kernel(o_ref, ap_ref, sgb_ref, x_ref, wao_ref, wo_ref, g_ref, b_ref, wrh_ref, wrl_ref, tri_ref,
                 x1e_ref, route_ref, totals_ref, count_sc, *, alpha, d_model):
    i = pl.program_id(0)
    rows = x_ref.shape[0]
    yb = _dot(o_ref[...], wao_ref[...])
    mix = _dot((ap_ref[...] + sgb_ref[...] * yb).astype(BF16), wo_ref[...])
    x1 = _layer_norm(alpha * x_ref[...] + mix, g_ref[...], b_ref[...])
    xh = x1.astype(BF16)
    xl = (x1 - xh.astype(F32)).astype(BF16)
    logits = _dot(xh, wrh_ref[...]) + _dot(xl, wrh_ref[...]) + _dot(xh, wrl_ref[...])
    gate4, g_idx = _route(logits)

    @pl.when(i == 0)
    def _():
        count_sc[...] = jnp.zeros((1, LANES), F32)
    lane = lax.broadcasted_iota(jnp.int32, (rows, LANES), 1).astype(F32)
    onehot = jnp.where(lane == g_idx, 1.0, 0.0)
    count = _dot(tri_ref[...], onehot.astype(BF16)) + count_sc[...]
    rank = jnp.sum(onehot * count, axis=1, keepdims=True) - 1.0
    count_sc[...] = count[rows - 1:rows, :]
    totals_ref[...] = jnp.broadcast_to(count[rows - 1:rows, :], (SUBLANES, LANES))
    route = (gate4 + jnp.where(lane == ROUTE_GROUP_LANE, g_idx, 0.0)
             + jnp.where(lane == ROUTE_RANK_LANE, rank, 0.0))
    route_ref[...] = route
    x1e_ref[:, 0:d_model] = x1
    x1e_ref[:, d_model:d_model + LANES] = route


def _post_attention(o, ap, sgb, x, w, *, alpha):
    t, d_model = x.shape
    rows = MOE_ROWS
    row_spec = lambda width: pl.BlockSpec((rows, width), lambda i: (i, 0))
    weights = [w['wao'], w['wo'], w['ln1_g'], w['ln1_b'], w['wr_hi'], w['wr_lo'], w['tri_moe']]
    return pl.pallas_call(
        functools.partial(_post_kernel, alpha=alpha, d_model=d_model),
        grid=(t // rows,),
        in_specs=[row_spec(D_ATTN), row_spec(d_model), row_spec(d_model), row_spec(d_model)]
                 + [_const_spec(a.shape) for a in weights],
        out_specs=[row_spec(d_model + LANES), row_spec(LANES),
                   pl.BlockSpec((SUBLANES, LANES), lambda i: (0, 0))],
        out_shape=[jax.ShapeDtypeStruct((t, d_model + LANES), F32),
                   jax.ShapeDtypeStruct((t, LANES), F32),
                   jax.ShapeDtypeStruct((SUBLANES, LANES), F32)],
        scratch_shapes=[pltpu.VMEM((1, LANES), F32)],
        compiler_params=pltpu.CompilerParams(dimension_semantics=("arbitrary",),
                                             vmem_limit_bytes=VMEM_LIMIT_BYTES),
        name="post_attn",
    )(o, ap, sgb, x, *weights)


def _row_permute_kernel(pos_ref, src_ref, *rest, rows, scatter):
    if scatter:
        _, dst_ref, sem = rest
    else:
        dst_ref, sem = rest
    base = pl.program_id(0) * rows

    def issue(r, carry):
        p = pos_ref[base + r]
        if scatter:
            pltpu.make_async_copy(src_ref.at[pl.ds(r, 1)], dst_ref.at[pl.ds(p, 1)], sem).start()
        else:
            pltpu.make_async_copy(src_ref.at[pl.ds(p, 1)], dst_ref.at[pl.ds(r, 1)], sem).start()
        return carry

    lax.fori_loop(0, rows, issue, 0, unroll=PERMUTE_UNROLL)
    if scatter:
        pltpu.make_async_copy(src_ref, dst_ref.at[pl.ds(0, rows)], sem).wait()
    else:
        pltpu.make_async_copy(src_ref.at[pl.ds(0, rows)], dst_ref, sem).wait()


def _row_permute(pos, src, *, n_out, scatter):
    n_rows = pos.shape[0]
    width = src.shape[1]
    rows = MOE_ROWS
    any_spec = pl.BlockSpec(memory_space=pl.ANY)
    tile_spec = pl.BlockSpec((rows, width), lambda i, pos_ref: (i, 0))
    args = [pos, src]
    aliases = {}
    if scatter:
        args.append(jnp.zeros((n_out, width), src.dtype))
        in_specs, out_specs = [tile_spec, any_spec], any_spec
        aliases = {2: 0}
    else:
        in_specs, out_specs = [any_spec], tile_spec
    grid_spec = pltpu.PrefetchScalarGridSpec(
        num_scalar_prefetch=1, grid=(n_rows // rows,), in_specs=in_specs, out_specs=out_specs,
        scratch_shapes=[pltpu.SemaphoreType.DMA(())])
    return pl.pallas_call(
        functools.partial(_row_permute_kernel, rows=rows, scatter=scatter),
        grid_spec=grid_spec,
        out_shape=jax.ShapeDtypeStruct((n_out, width), src.dtype),
        input_output_aliases=aliases,
        compiler_params=pltpu.CompilerParams(dimension_semantics=("arbitrary",)),
        name="row_scatter" if scatter else "row_gather",
    )(*args)


def _moe_kernel(tg_ref, nv_ref, xs_ref, w1_ref, w3_ref, w2_ref, g_ref, b_ref, y_ref, *, alpha, d_model):
    del tg_ref
    i = pl.program_id(0)

    @pl.when(i < nv_ref[0])
    def _():
        x = xs_ref[:, 0:d_model]
        route = xs_ref[:, d_model:d_model + LANES]
        xb = x.astype(BF16)
        lane = lax.broadcasted_iota(jnp.int32, route.shape, 1)
        acc = jnp.zeros_like(x)
        for e in range(EXPERTS_PER_GROUP):
            h1 = _dot(xb, w1_ref[0, e])
            h3 = _dot(xb, w3_ref[0, e])
            h = (h1 * jax.nn.sigmoid(h1)) * h3
            ye = _dot(h.astype(BF16), w2_ref[0, e])
            ge = jnp.sum(jnp.where(lane == e, route, 0.0), axis=1, keepdims=True)
            acc = acc + ge * ye
        y_ref[...] = _layer_norm(alpha * x + acc, g_ref[...], b_ref[...])

    @pl.when(i >= nv_ref[0])
    def _():
        y_ref[...] = jnp.zeros_like(y_ref)


def _experts(x1e, route, totals, w, *, alpha):
    t = x1e.shape[0]
    d_model = x1e.shape[1] - LANES
    _, _, _, d_exp = w['w1'].shape
    rows = MOE_ROWS
    n_tiles = t // rows + N_GROUPS
    group = route[:, ROUTE_GROUP_LANE].astype(jnp.int32)
    rank = route[:, ROUTE_RANK_LANE].astype(jnp.int32)
    count = totals[0, 0:N_GROUPS].astype(jnp.int32)
    tiles_g = (count + rows - 1) // rows
    tile_end = jnp.cumsum(tiles_g)
    pos = (tile_end - tiles_g)[group] * rows + rank
    n_valid = tile_end[N_GROUPS - 1:N_GROUPS]
    tile_group = jnp.minimum(jnp.sum(jnp.arange(n_tiles)[:, None] >= tile_end[None, :], axis=1),
                             N_GROUPS - 1).astype(jnp.int32)

    xs = _row_permute(pos, x1e, n_out=n_tiles * rows, scatter=True)

    live = lambda i, nv: jnp.minimum(i, nv[0] - 1)
    w_spec = lambda a: pl.BlockSpec((1,) + a.shape[1:], lambda i, tg, nv: (tg[live(i, nv)], 0, 0, 0))
    grid_spec = pltpu.PrefetchScalarGridSpec(
        num_scalar_prefetch=2,
        grid=(n_tiles,),
        in_specs=[
            pl.BlockSpec((rows, d_model + LANES), lambda i, tg, nv: (live(i, nv), 0)),
            w_spec(w['w1']), w_spec(w['w3']), w_spec(w['w2']),
            pl.BlockSpec((1, d_model), lambda i, tg, nv: (0, 0)),
            pl.BlockSpec((1, d_model), lambda i, tg, nv: (0, 0)),
        ],
        out_specs=pl.BlockSpec((rows, d_model), lambda i, tg, nv: (i, 0)),
    )
    ys = pl.pallas_call(
        functools.partial(_moe_kernel, alpha=alpha, d_model=d_model),
        grid_spec=grid_spec,
        out_shape=jax.ShapeDtypeStruct((n_tiles * rows, d_model), F32),
        compiler_params=pltpu.CompilerParams(dimension_semantics=("arbitrary",),
                                             vmem_limit_bytes=VMEM_LIMIT_BYTES),
        name="experts",
    )(tile_group, n_valid, xs, w['w1'], w['w3'], w['w2'], w['ln2_g'], w['ln2_b'])
    return _row_permute(pos, ys, n_out=t, scatter=False)


def _prepare_weights(w_in, b_f, conv_w, w_conv_out, w_attn_out, w_o, ln1_g, ln1_b,
                     w_group, w_router, w1, w3, w2, ln2_g, ln2_b, rows_inproj):
    d_model = w_in.shape[0]
    d_conv = w_conv_out.shape[0]
    o_q = 3 * d_conv
    o_k = o_q + D_ATTN
    o_v = o_k + D_ATTN
    o_f = o_v + D_ATTN
    o_g = o_f + N_HEADS
    w_bf = w_in.astype(BF16)
    wm = jnp.concatenate([w_bf[:, 0:o_q], w_bf[:, o_k:o_f]], axis=1)
    wq = jnp.pad(w_bf[:, o_q:o_k].reshape(d_model, N_HEADS, HEAD_DIM),
                 ((0, 0), (0, 0), (0, LANES - HEAD_DIM))).reshape(d_model, N_HEADS * LANES)
    wf = jnp.pad(w_bf[:, o_f:o_g], ((0, 0), (0, LANES - N_HEADS)))
    wg = w_bf[:, o_g:]
    bf = jnp.pad(b_f.astype(F32), (0, LANES - N_HEADS)).reshape(1, LANES)

    col = jnp.arange(N_HEADS * LANES)
    col_head, col_lane = col // LANES, col % LANES
    src = jnp.arange(D_ATTN)
    sel = ((src[:, None] // HEAD_DIM == col_head[None, :]) & (src[:, None] % HEAD_DIM == col_lane[None, :]))
    part = jnp.arange(AUG_SPLITS * LANES)
    part_k, part_h = part // LANES, part % LANES
    pq = (part_h[:, None] == col_head[None, :]) & (col_lane[None, :] == AUG_Q_C + part_k[:, None])
    pk = (part_h[:, None] == col_head[None, :]) & (col_lane[None, :] == AUG_Q_C + AUG_SPLITS + part_k[:, None])
    oq = ((col_lane >= AUG_Q_C + AUG_SPLITS) & (col_lane < AUG_Q_C + 2 * AUG_SPLITS)).astype(F32).reshape(1, -1)
    ok = ((col_lane >= AUG_Q_C) & (col_lane < AUG_Q_C + AUG_SPLITS)).astype(F32).reshape(1, -1)
    r = jnp.arange(rows_inproj)
    tri = (r[None, :] <= r[:, None]).astype(BF16)
    r = jnp.arange(MOE_ROWS)
    tri_moe = (r[None, :] <= r[:, None]).astype(BF16)

    wr = jnp.pad(jnp.concatenate([w_group, w_router], axis=1), ((0, 0), (0, LANES - N_GROUPS - N_EXPERTS)))
    wr_hi = wr.astype(BF16)
    wr_lo = (wr - wr_hi.astype(F32)).astype(BF16)
    return dict(
        wm=wm, wq=wq, wf=wf, wg=wg, bf=bf, cw=conv_w.astype(F32), wco=w_conv_out.astype(BF16),
        tri=tri, sel=sel.astype(BF16), pq=pq.astype(BF16), pk=pk.astype(BF16), oq=oq, ok=ok,
        wao=w_attn_out.astype(BF16), wo=w_o.astype(BF16),
        ln1_g=ln1_g.reshape(1, -1), ln1_b=ln1_b.reshape(1, -1), wr_hi=wr_hi, wr_lo=wr_lo,
        tri_moe=tri_moe,
        w1=w1.astype(BF16).reshape((N_GROUPS, EXPERTS_PER_GROUP) + w1.shape[1:]),
        w3=w3.astype(BF16).reshape((N_GROUPS, EXPERTS_PER_GROUP) + w3.shape[1:]),
        w2=w2.astype(BF16).reshape((N_GROUPS, EXPERTS_PER_GROUP) + w2.shape[1:]),
        ln2_g=ln2_g.reshape(1, -1), ln2_b=ln2_b.reshape(1, -1),
    )


def kernel(x_prompt, x_sample, cache_k, cache_v, cache_logf, state_conv, page_table, w_in, b_f, conv_w,
           w_conv_out, w_attn_out, w_o, ln1_g, ln1_b, w_group, w_router, w1, w3, w2, ln2_g, ln2_b):
    depth = w_in.shape[0]
    assert depth == 1, "single-layer stack"
    n_p, s_p, d_model = x_prompt.shape
    n_s, s_s, _ = x_sample.shape
    d_conv = w_conv_out.shape[1]
    alpha = (2 * depth) ** 0.25
    w = _prepare_weights(w_in[0], b_f[0], conv_w[0], w_conv_out[0], w_attn_out[0], w_o[0], ln1_g[0], ln1_b[0],
                         w_group[0], w_router[0], w1[0], w3[0], w2[0], ln2_g[0], ln2_b[0], rows_inproj=256)

    xp = x_prompt.reshape(n_p * s_p, d_model)
    qa, kt_p, vt_p, ka, vtb, lft_p, ap, sgb, tail = _inproj(xp, w, sample=False, seq_len=s_p)
    o_p = _prompt_attention(qa, ka, vtb, n_seq=n_p, seq_len=s_p)
    x1e_p, route_p, totals_p = _post_attention(o_p, ap, sgb, xp, w, alpha=alpha)
    y_p = _experts(x1e_p, route_p, totals_p, w, alpha=alpha)

    xs = x_sample.reshape(n_s * s_s, d_model)
    state = state_conv[0]
    s0 = jnp.repeat(state[:, 0, :], s_s, axis=0)
    s1 = jnp.repeat(state[:, 1, :], s_s, axis=0)
    q_s, k_s, v_s, lf_s, ap_s, sgb_s, u_s = _inproj(xs, w, sample=True, seq_len=s_s, state=(s0, s1))
    q5 = q_s.reshape(n_s, s_s, N_HEADS, LANES)[..., :HEAD_DIM]
    eye_h = jnp.eye(N_HEADS, dtype=BF16)
    qblk = (q5[:, :, :, None, :] * eye_h[None, None, :, :, None]).reshape(n_s, s_s * N_HEADS, D_ATTN)
    n_pool = cache_k.shape[1]
    cache_kt = cache_k[0].transpose(0, 2, 3, 1).reshape(n_pool, D_ATTN, PAGE_SIZE)
    cache_vt = cache_v[0].transpose(0, 2, 3, 1).reshape(n_pool, D_ATTN, PAGE_SIZE)
    cache_lft = cache_logf[0].transpose(0, 2, 1)
    o_blk = _sample_attention(
        qblk, cache_kt, cache_vt, cache_lft, page_table, k_s.reshape(n_s, s_s, D_ATTN),
        v_s.reshape(n_s, s_s, D_ATTN), lf_s.reshape(n_s, s_s, N_HEADS))
    o5 = o_blk.reshape(n_s, s_s, N_HEADS, N_HEADS, HEAD_DIM)
    o_s = jnp.einsum('bthgd,hg->bthd', o5, jnp.eye(N_HEADS, dtype=F32)).reshape(n_s * s_s, D_ATTN).astype(BF16)
    x1e_s, route_s, totals_s = _post_attention(o_s, ap_s, sgb_s, xs, w, alpha=alpha)
    y_s = _experts(x1e_s, route_s, totals_s, w, alpha=alpha)

    return (
        y_p.reshape(n_p, s_p, d_model),
        y_s.reshape(n_s, s_s, d_model),
        kt_p.reshape(1, n_p, N_HEADS, HEAD_DIM, s_p).transpose(0, 1, 4, 2, 3),
        vt_p.reshape(1, n_p, N_HEADS, HEAD_DIM, s_p).transpose(0, 1, 4, 2, 3),
        lft_p.reshape(1, n_p, N_HEADS, s_p).transpose(0, 1, 3, 2),
        tail[:, SUBLANES - (CONV_WIDTH - 1):, :].reshape(1, n_p, CONV_WIDTH - 1, d_conv),
        k_s.reshape(1, n_s, s_s, N_HEADS, HEAD_DIM),
        v_s.reshape(1, n_s, s_s, N_HEADS, HEAD_DIM),
        lf_s.reshape(1, n_s, s_s, N_HEADS),
        u_s.reshape(n_s, s_s, d_conv)[:, s_s - (CONV_WIDTH - 1):, :].reshape(1, n_s, CONV_WIDTH - 1, d_conv),
    )
```

```python
import functools

import jax
import jax.numpy as jnp
from jax import lax
from jax.experimental import pallas as pl
from jax.experimental.pallas import tpu as pltpu

F32 = jnp.float32
BF16 = jnp.bfloat16

LANES = 128
SUBLANES = 8
VMEM_LIMIT_BYTES = 56 * 1024 * 1024

CONV_WIDTH = 3
N_HEADS = 8
HEAD_DIM = 64
D_ATTN = N_HEADS * HEAD_DIM
N_GROUPS = 4
EXPERTS_PER_GROUP = 4
N_EXPERTS = N_GROUPS * EXPERTS_PER_GROUP
LN_EPS = 1e-5
PAGE_SIZE = 128
MASK_VALUE = -1e30
AUG_Q_C = HEAD_DIM
AUG_SPLITS = 3
MOE_ROWS = 512
ROUTE_GROUP_LANE = EXPERTS_PER_GROUP
ROUTE_RANK_LANE = EXPERTS_PER_GROUP + 1
PERMUTE_UNROLL = 8
V_PAD_ROWS = 16
V_ROWS = HEAD_DIM + V_PAD_ROWS


def _dot(a, b):
    return jnp.dot(a, b, preferred_element_type=F32)


def _dot_nt(a, b):
    return lax.dot_general(a, b, (((1,), (1,)), ((), ())), preferred_element_type=F32)


def _split3(x):
    h1 = x.astype(BF16)
    r1 = x - h1.astype(F32)
    h2 = r1.astype(BF16)
    r2 = r1 - h2.astype(F32)
    h3 = r2.astype(BF16)
    return h1, h2, h3


def _dot_exact_rhs(a_bf16, x):
    h1, h2, h3 = _split3(x)
    return _dot(a_bf16, h1) + _dot(a_bf16, h2) + _dot(a_bf16, h3)


def _dot_exact_lhs(x, b_bf16):
    h1, h2, h3 = _split3(x)
    return _dot(h1, b_bf16) + _dot(h2, b_bf16) + _dot(h3, b_bf16)


def _dot_nt_exact_rhs(a_bf16, x):
    h1, h2, h3 = _split3(x)
    return _dot_nt(a_bf16, h1) + _dot_nt(a_bf16, h2) + _dot_nt(a_bf16, h3)


def _log_sigmoid(x):
    return jnp.minimum(x, 0.0) - jnp.log1p(jnp.exp(-jnp.abs(x)))


def _layer_norm(r, g, b):
    mu = jnp.mean(r, axis=-1, keepdims=True)
    d = r - mu
    var = jnp.mean(d * d, axis=-1, keepdims=True)
    return d * lax.rsqrt(var + LN_EPS) * g + b


def _const_spec(shape):
    nd = len(shape)
    return pl.BlockSpec(shape, lambda *_: (0,) * nd, pipeline_mode=pl.Buffered(1))


def _inproj_kernel(*refs, rows, tiles_per_seq, sample, d_conv, d_model):
    if sample:
        (x_ref, wm_ref, wf_ref, wg_ref, bf_ref, cw_ref, wco_ref, s0_ref, s1_ref,
         q_ref, k_ref, v_ref, lf_ref, ap_ref, sgb_ref, u_ref, ubuf) = refs
    else:
        (x_ref, wm_ref, wf_ref, wg_ref, bf_ref, cw_ref, wco_ref, tri_ref, pq_ref, pk_ref,
         oq_ref, ok_ref,
         q_ref, k_ref, v_ref, ka_ref, vb_ref, lf_ref, ap_ref, sgb_ref, tail_ref, ubuf, ccarry) = refs
    i = pl.program_id(0)
    xb = x_ref[...].astype(BF16)

    xc = _dot(xb, wm_ref[:, 0:d_conv])
    bg = _dot(xb, wm_ref[:, d_conv:2 * d_conv])
    cg = _dot(xb, wm_ref[:, 2 * d_conv:3 * d_conv])
    u = cg * xc
    if sample:
        ubuf[0:SUBLANES, :] = jnp.zeros((SUBLANES, d_conv), F32)
    else:
        @pl.when(i % tiles_per_seq == 0)
        def _():
            ubuf[0:SUBLANES, :] = jnp.zeros((SUBLANES, d_conv), F32)
    ubuf[SUBLANES:SUBLANES + rows, :] = u
    prev1 = ubuf[SUBLANES - 1:SUBLANES - 1 + rows, :]
    prev2 = ubuf[SUBLANES - 2:SUBLANES - 2 + rows, :]
    if sample:
        pos = lax.broadcasted_iota(jnp.int32, (rows, 1), 0) % 4
        prev1 = jnp.where(pos == 0, s1_ref[...], prev1)
        prev2 = jnp.where(pos == 0, s0_ref[...], jnp.where(pos == 1, s1_ref[...], prev2))
        u_ref[...] = u
    else:
        ubuf[0:SUBLANES, :] = u[rows - SUBLANES:rows, :]

        @pl.when(i % tiles_per_seq == tiles_per_seq - 1)
        def _():
            tail_ref[0] = u[rows - SUBLANES:rows, :]
    conv = cw_ref[0:1, :] * prev2 + cw_ref[1:2, :] * prev1 + cw_ref[2:3, :] * u
    ya = _dot((bg * conv).astype(BF16), wco_ref[...])

    q = _dot(xb, wm_ref[:, 3 * d_conv:3 * d_conv + D_ATTN]) * (HEAD_DIM ** -0.5)
    k = _dot(xb, wm_ref[:, 3 * d_conv + D_ATTN:3 * d_conv + 2 * D_ATTN])
    v = _dot(xb, wm_ref[:, 3 * d_conv + 2 * D_ATTN:3 * d_conv + 3 * D_ATTN])
    fg = _dot(xb, wf_ref[...]) + bf_ref[...]
    lane = lax.broadcasted_iota(jnp.int32, (rows, LANES), 1)
    logf = jnp.where(lane < N_HEADS, _log_sigmoid(fg), 0.0)
    if sample:
        k_ref[...] = k
        v_ref[...] = v
        lf_ref[...] = logf[:, 0:N_HEADS]
        q_ref[...] = q.astype(BF16)
    else:
        k_ref[0] = k.T
        vt = v.T
        v_ref[0] = vt
        lf_ref[0] = logf.T[0:N_HEADS, :]
        ones_blk = jnp.where(lax.broadcasted_iota(jnp.int32, (V_PAD_ROWS, rows), 0) == 0, 1.0, 0.0)
        vb_ref[0] = jnp.concatenate(
            [blk for h in range(N_HEADS) for blk in (vt[h * HEAD_DIM:(h + 1) * HEAD_DIM, :], ones_blk)],
            axis=0).astype(BF16)

        @pl.when(i % tiles_per_seq == 0)
        def _():
            ccarry[...] = jnp.zeros((1, LANES), F32)
        c = _dot_exact_rhs(tri_ref[...], logf) + ccarry[...]
        ccarry[...] = c[rows - 1:rows, :]
        c1, c2, c3 = _split3(c)
        c_parts = (c1.astype(F32) + pltpu.roll(c2.astype(F32), N_HEADS, 1)
                   + pltpu.roll(c3.astype(F32), 2 * N_HEADS, 1)).astype(BF16)
        aug_q = _dot(c_parts, pq_ref[...]) + oq_ref[...]
        aug_k = ok_ref[...] - _dot(c_parts, pk_ref[...])

        def spread_heads(x, aug):
            groups = []
            for h in range(N_HEADS):
                slab = x[:, (h // 2) * LANES:(h // 2 + 1) * LANES]
                if h % 2:
                    slab = pltpu.roll(slab, HEAD_DIM, 1)
                groups.append(jnp.where(lane < HEAD_DIM, slab, aug[:, h * LANES:(h + 1) * LANES]))
            return jnp.concatenate(groups, axis=1).astype(BF16)

        q_ref[...] = spread_heads(q, aug_q)
        ka_ref[...] = spread_heads(k, aug_k)

    ga = _dot(xb, wg_ref[:, 0:d_model])
    gb = _dot(xb, wg_ref[:, d_model:2 * d_model])
    ap_ref[...] = jax.nn.sigmoid(ga) * ya
    sgb_ref[...] = jax.nn.sigmoid(gb)


def _inproj(x, w, *, sample, seq_len, state=None):
    t, d_model = x.shape
    d_conv = w['wco'].shape[0]
    rows = 256
    tiles_per_seq = max(seq_len // rows, 1)
    n_tiles = t // rows
    row_spec = lambda width: pl.BlockSpec((rows, width), lambda i: (i, 0))
    weights = [w['wm'], w['wf'], w['wg'], w['bf'], w['cw'], w['wco']]
    in_specs = [row_spec(d_model)] + [_const_spec(a.shape) for a in weights]
    args = [x] + weights
    if sample:
        args += [state[0], state[1]]
        in_specs += [row_spec(d_conv), row_spec(d_conv)]
        out_shape = [
            jax.ShapeDtypeStruct((t, D_ATTN), BF16),
            jax.ShapeDtypeStruct((t, D_ATTN), F32),
            jax.ShapeDtypeStruct((t, D_ATTN), F32),
            jax.ShapeDtypeStruct((t, N_HEADS), F32),
            jax.ShapeDtypeStruct((t, d_model), F32),
            jax.ShapeDtypeStruct((t, d_model), F32),
            jax.ShapeDtypeStruct((t, d_conv), F32),
        ]
        out_specs = [row_spec(D_ATTN), row_spec(D_ATTN), row_spec(D_ATTN), row_spec(N_HEADS),
                     row_spec(d_model), row_spec(d_model), row_spec(d_conv)]
        scratch = [pltpu.VMEM((rows + SUBLANES, d_conv), F32)]
    else:
        consts = [w['tri'], w['pq'], w['pk'], w['oq'], w['ok']]
        args += consts
        in_specs += [_const_spec(a.shape) for a in consts]
        n_seq = t // seq_len
        out_shape = [
            jax.ShapeDtypeStruct((t, N_HEADS * LANES), BF16),
            jax.ShapeDtypeStruct((n_seq, D_ATTN, seq_len), F32),
            jax.ShapeDtypeStruct((n_seq, D_ATTN, seq_len), F32),
            jax.ShapeDtypeStruct((t, N_HEADS * LANES), BF16),
            jax.ShapeDtypeStruct((n_tiles, N_HEADS * V_ROWS, rows), BF16),
            jax.ShapeDtypeStruct((n_seq, N_HEADS, seq_len), F32),
            jax.ShapeDtypeStruct((t, d_model), F32),
            jax.ShapeDtypeStruct((t, d_model), F32),
            jax.ShapeDtypeStruct((n_seq, SUBLANES, d_conv), F32),
        ]
        pos_minor = lambda height: pl.BlockSpec((1, height, rows),
                                                lambda i: (i // tiles_per_seq, 0, i % tiles_per_seq))
        out_specs = [row_spec(N_HEADS * LANES), pos_minor(D_ATTN), pos_minor(D_ATTN), row_spec(N_HEADS * LANES),
                     pl.BlockSpec((1, N_HEADS * V_ROWS, rows), lambda i: (i, 0, 0)), pos_minor(N_HEADS),
                     row_spec(d_model), row_spec(d_model),
                     pl.BlockSpec((1, SUBLANES, d_conv), lambda i: (i // tiles_per_seq, 0, 0))]
        scratch = [pltpu.VMEM((rows + SUBLANES, d_conv), F32), pltpu.VMEM((1, LANES), F32)]
    kern = functools.partial(_inproj_kernel, rows=rows, tiles_per_seq=tiles_per_seq, sample=sample,
                             d_conv=d_conv, d_model=d_model)
    return pl.pallas_call(
        kern,
        grid=(n_tiles,),
        in_specs=in_specs,
        out_specs=out_specs,
        out_shape=out_shape,
        scratch_shapes=scratch,
        compiler_params=pltpu.CompilerParams(dimension_semantics=("arbitrary",),
                                             vmem_limit_bytes=VMEM_LIMIT_BYTES),
        name="inproj_sample" if sample else "inproj_prompt",
    )(*args)


def _attn_kernel(q_ref, k_ref, vt_ref, o_ref, m_sc, acc_sc, *, tq, v_chunk, q_chunk):
    i = pl.program_id(2)
    chunks = tq // v_chunk
    for hl in range(2):
        m_sc[hl] = jnp.full((1, tq), MASK_VALUE, F32)
        acc_sc[hl] = jnp.zeros((V_ROWS, tq), F32)

    def step(j, masked):
        start = pl.multiple_of(j * tq, tq)
        blocks = [(hl, qc) for hl in range(2) for qc in range(tq // q_chunk)]
        m_old = [m_sc[hl] for hl in range(2)]
        acc_old = [acc_sc[hl] for hl in range(2)]
        sts = [_dot_nt(k_ref[pl.ds(start, tq), hl * LANES:(hl + 1) * LANES],
                       q_ref[qc * q_chunk:(qc + 1) * q_chunk, hl * LANES:(hl + 1) * LANES]) for hl, qc in blocks]
        m_out, acc_out = [[], []], [[], []]
        for (hl, qc), st in zip(blocks, sts):
            qs = slice(qc * q_chunk, (qc + 1) * q_chunk)
            if masked:
                key = lax.broadcasted_iota(jnp.int32, (tq, q_chunk), 0)
                qry = lax.broadcasted_iota(jnp.int32, (tq, q_chunk), 1) + qc * q_chunk
                st = jnp.where(key <= qry, st, MASK_VALUE)
            m_prev = m_old[hl][:, qs]
            m_new = jnp.maximum(m_prev, jnp.max(st, axis=0, keepdims=True))
            alpha = jnp.exp(m_prev - m_new)
            pb = jnp.exp(st - m_new).astype(BF16)
            acc = alpha * acc_old[hl][:, qs]
            for c in range(chunks):
                vt = vt_ref[j * chunks + c, hl * V_ROWS:(hl + 1) * V_ROWS, :]
                acc = acc + _dot(vt, pb[c * v_chunk:(c + 1) * v_chunk, :])
            acc_out[hl].append(acc)
            m_out[hl].append(m_new)
        for hl in range(2):
            m_sc[hl] = jnp.concatenate(m_out[hl], axis=1)
            acc_sc[hl] = jnp.concatenate(acc_out[hl], axis=1)

    def body(j, carry):
        step(j, False)
        return carry

    lax.fori_loop(0, i, body, 0)
    step(i, True)
    ot = jnp.concatenate([acc_sc[hl, 0:HEAD_DIM, :] / acc_sc[hl, HEAD_DIM:HEAD_DIM + 1, :] for hl in range(2)],
                         axis=0)
    o_ref[...] = ot.T.astype(BF16)


def _prompt_attention(qa, ka, vtb, *, n_seq, seq_len):
    t = qa.shape[0]
    v_chunk = vtb.shape[2]
    tq = 512
    qt = seq_len // tq
    n_pairs = N_HEADS // 2
    return pl.pallas_call(
        functools.partial(_attn_kernel, tq=tq, v_chunk=v_chunk, q_chunk=LANES),
        grid=(n_seq, n_pairs, qt),
        in_specs=[
            pl.BlockSpec((tq, 2 * LANES), lambda n, hp, i: (n * qt + i, hp)),
            pl.BlockSpec((seq_len, 2 * LANES), lambda n, hp, i: (n, hp)),
            pl.BlockSpec((seq_len // v_chunk, 2 * V_ROWS, v_chunk), lambda n, hp, i: (n, hp, 0)),
        ],
        out_specs=pl.BlockSpec((tq, LANES), lambda n, hp, i: (n * qt + i, hp)),
        out_shape=jax.ShapeDtypeStruct((t, D_ATTN), BF16),
        scratch_shapes=[pltpu.VMEM((2, 1, tq), F32), pltpu.VMEM((2, V_ROWS, tq), F32)],
        compiler_params=pltpu.CompilerParams(dimension_semantics=("arbitrary", "arbitrary", "arbitrary"),
                                             vmem_limit_bytes=VMEM_LIMIT_BYTES),
        name="attn_prompt",
    )(qa, ka, vtb)


def _sample_attn_kernel(pt_ref, *refs, n_pages, n_new):
    del pt_ref
    q_ref = refs[0]
    k_refs = refs[1:1 + n_pages]
    v_refs = refs[1 + n_pages:1 + 2 * n_pages]
    f_refs = refs[1 + 2 * n_pages:1 + 3 * n_pages]
    kn_ref, vn_ref, fn_ref, e_ref, suffix_ref, later_ref, tri_ref = refs[1 + 3 * n_pages:8 + 3 * n_pages]
    o_ref = refs[8 + 3 * n_pages]
    pad_sc, kn_sc, vn_sc = refs[9 + 3 * n_pages:]
    n_rows = N_HEADS * n_new
    qb = q_ref[0]
    e = e_ref[...]

    pad_sc[...] = jnp.zeros((PAGE_SIZE, LANES), F32)
    pad_sc[0:n_new, 0:N_HEADS] = fn_ref[0]
    kn_sc[...] = jnp.zeros((PAGE_SIZE, D_ATTN), F32)
    kn_sc[0:n_new, :] = kn_ref[0]
    vn_sc[...] = jnp.zeros((PAGE_SIZE, D_ATTN), F32)
    vn_sc[0:n_new, :] = vn_ref[0]
    cn = _dot_exact_rhs(tri_ref[...], pad_sc[...])
    cn_rows = _dot_nt_exact_rhs(e, cn)
    row_t = lax.broadcasted_iota(jnp.int32, (n_rows, LANES), 0) // N_HEADS
    col = lax.broadcasted_iota(jnp.int32, (n_rows, LANES), 1)
    cnq = jnp.sum(jnp.where(col == row_t, cn_rows, 0.0), axis=1, keepdims=True)
    s_new = _dot_nt(qb, kn_sc[...].astype(BF16)) + (cnq - cn_rows)
    s_new = jnp.where(col <= row_t, s_new, MASK_VALUE)

    lf = jnp.concatenate([f_refs[j][0] for j in range(n_pages)], axis=0)
    incl = _dot_exact_lhs(lf, suffix_ref[...])
    tot = jnp.broadcast_to(incl[:, 0:1], incl.shape)
    d = incl - lf + _dot_exact_rhs(later_ref[...], tot)
    s_pages = []
    for j in range(n_pages):
        bias = jnp.concatenate([d[j * N_HEADS:(j + 1) * N_HEADS, :]] * n_new, axis=0)
        s_pages.append(_dot(qb, k_refs[j][0].astype(BF16)) + bias + cnq)

    m_blk = s_new
    for s in s_pages:
        m_blk = jnp.maximum(m_blk, s)
    m = jnp.max(m_blk, axis=1, keepdims=True)
    p_new = jnp.exp(s_new - m)
    l_blk = p_new
    acc = _dot(p_new.astype(BF16), vn_sc[...].astype(BF16))
    for j in range(n_pages):
        p = jnp.exp(s_pages[j] - m)
        l_blk = l_blk + p
        acc = acc + _dot_nt(p.astype(BF16), v_refs[j][0].astype(BF16))
    o_ref[0] = acc / jnp.sum(l_blk, axis=1, keepdims=True)


def _sample_attention(qblk, cache_kt, cache_vt, cache_lft, page_table, k_new, v_new, lf_new):
    n_seq, n_rows, _ = qblk.shape
    n_new = k_new.shape[1]
    n_pages = page_table.shape[1]
    pt = page_table.reshape(-1)
    row_head = jnp.arange(n_rows) % N_HEADS
    e = (row_head[:, None] == jnp.arange(LANES)[None, :]).astype(BF16)
    pos = jnp.arange(PAGE_SIZE)
    suffix = (pos[:, None] >= pos[None, :]).astype(BF16)
    tri = (pos[None, :] <= pos[:, None]).astype(BF16)
    r = jnp.arange(n_pages * N_HEADS)
    later = ((r[:, None] % N_HEADS == r[None, :] % N_HEADS)
             & (r[None, :] // N_HEADS > r[:, None] // N_HEADS)).astype(BF16)

    def page_map(j):
        return lambda b, pt_ref: (pt_ref[b * n_pages + j], 0, 0)

    seq_map = lambda b, pt_ref: (b, 0, 0)
    const_map = lambda b, pt_ref: (0, 0)
    in_specs = [pl.BlockSpec((1, n_rows, D_ATTN), seq_map)]
    in_specs += [pl.BlockSpec((1, D_ATTN, PAGE_SIZE), page_map(j)) for j in range(n_pages)]
    in_specs += [pl.BlockSpec((1, D_ATTN, PAGE_SIZE), page_map(j)) for j in range(n_pages)]
    in_specs += [pl.BlockSpec((1, N_HEADS, PAGE_SIZE), page_map(j)) for j in range(n_pages)]
    in_specs += [pl.BlockSpec((1, n_new, D_ATTN), seq_map), pl.BlockSpec((1, n_new, D_ATTN), seq_map),
                 pl.BlockSpec((1, n_new, N_HEADS), seq_map),
                 pl.BlockSpec(e.shape, const_map), pl.BlockSpec(suffix.shape, const_map),
                 pl.BlockSpec(later.shape, const_map), pl.BlockSpec(tri.shape, const_map)]
    grid_spec = pltpu.PrefetchScalarGridSpec(
        num_scalar_prefetch=1,
        grid=(n_seq,),
        in_specs=in_specs,
        out_specs=pl.BlockSpec((1, n_rows, D_ATTN), seq_map),
        scratch_shapes=[pltpu.VMEM((PAGE_SIZE, LANES), F32),
                        pltpu.VMEM((PAGE_SIZE, D_ATTN), F32), pltpu.VMEM((PAGE_SIZE, D_ATTN), F32)],
    )
    args = ([pt, qblk] + [cache_kt] * n_pages + [cache_vt] * n_pages + [cache_lft] * n_pages
            + [k_new, v_new, lf_new, e, suffix, later, tri])
    return pl.pallas_call(
        functools.partial(_sample_attn_kernel, n_pages=n_pages, n_new=n_new),
        grid_spec=grid_spec,
        out_shape=jax.ShapeDtypeStruct((n_seq, n_rows, D_ATTN), F32),
        compiler_params=pltpu.CompilerParams(dimension_semantics=("arbitrary",),
                                             vmem_limit_bytes=VMEM_LIMIT_BYTES),
        name="attn_sample",
    )(*args)


def _route(logits):
    rows = logits.shape[0]
    lane = lax.broadcasted_iota(jnp.int32, (rows, LANES), 1).astype(F32)
    big = float(LANES)
    in_g = lane < N_GROUPS
    gl = jnp.where(in_g, logits, MASK_VALUE)
    m_g = jnp.max(gl, axis=1, keepdims=True)
    z_g = jnp.sum(jnp.exp(gl - m_g), axis=1, keepdims=True)
    p_g = 1.0 / z_g
    g_idx = jnp.min(jnp.where(in_g & (gl == m_g), lane, big), axis=1, keepdims=True)
    lo = N_GROUPS + EXPERTS_PER_GROUP * g_idx
    sel = (lane >= lo) & (lane < lo + EXPERTS_PER_GROUP)
    el = jnp.where(sel, logits, MASK_VALUE)
    m_e = jnp.max(el, axis=1, keepdims=True)
    ex = jnp.exp(el - m_e)
    ep = ex / jnp.sum(ex, axis=1, keepdims=True)
    v1 = jnp.max(jnp.where(sel, ep, -1.0), axis=1, keepdims=True)
    i1 = jnp.min(jnp.where(sel & (ep == v1), lane, big), axis=1, keepdims=True)
    sel2 = sel & (lane != i1)
    v2 = jnp.max(jnp.where(sel2, ep, -1.0), axis=1, keepdims=True)
    i2 = jnp.min(jnp.where(sel2 & (ep == v2), lane, big), axis=1, keepdims=True)
    denom = v1 + v2
    w1 = p_g * (v1 / denom)
    w2 = p_g * (v2 / denom)
    gate4 = jnp.where(lane == i1 - lo, w1, 0.0) + jnp.where(lane == i2 - lo, w2, 0.0)
    return gate4, g_idx


def _post_kernel(o_ref, ap_ref, sgb_ref, x_ref, wao_ref, wo_ref, g_ref, b_ref, wrh_ref, wrl_ref, tri_ref,
                 x1e_ref, route_ref, totals_ref, count_sc, *, alpha, d_model):
    i = pl.program_id(0)
    rows = x_ref.shape[0]
    yb = _dot(o_ref[...], wao_ref[...])
    mix = _dot((ap_ref[...] + sgb_ref[...] * yb).astype(BF16), wo_ref[...])
    x1 = _layer_norm(alpha * x_ref[...] + mix, g_ref[...], b_ref[...])
    xh = x1.astype(BF16)
    xl = (x1 - xh.astype(F32)).astype(BF16)
    logits = _dot(xh, wrh_ref[...]) + _dot(xl, wrh_ref[...]) + _dot(xh, wrl_ref[...])
    gate4, g_idx = _route(logits)

    @pl.when(i == 0)
    def _():
        count_sc[...] = jnp.zeros((1, LANES), F32)
    lane = lax.broadcasted_iota(jnp.int32, (rows, LANES), 1).astype(F32)
    onehot = jnp.where(lane == g_idx, 1.0, 0.0)
    count = _dot(tri_ref[...], onehot.astype(BF16)) + count_sc[...]
    rank = jnp.sum(onehot * count, axis=1, keepdims=True) - 1.0
    count_sc[...] = count[rows - 1:rows, :]
    totals_ref[...] = jnp.broadcast_to(count[rows - 1:rows, :], (SUBLANES, LANES))
    route = (gate4 + jnp.where(lane == ROUTE_GROUP_LANE, g_idx, 0.0)
             + jnp.where(lane == ROUTE_RANK_LANE, rank, 0.0))
    route_ref[...] = route
    x1e_ref[:, 0:d_model] = x1
    x1e_ref[:, d_model:d_model + LANES] = route


def _post_attention(o, ap, sgb, x, w, *, alpha):
    t, d_model = x.shape
    rows = MOE_ROWS
    row_spec = lambda width: pl.BlockSpec((rows, width), lambda i: (i, 0))
    weights = [w['wao'], w['wo'], w['ln1_g'], w['ln1_b'], w['wr_hi'], w['wr_lo'], w['tri_moe']]
    return pl.pallas_call(
        functools.partial(_post_kernel, alpha=alpha, d_model=d_model),
        grid=(t // rows,),
        in_specs=[row_spec(D_ATTN), row_spec(d_model), row_spec(d_model), row_spec(d_model)]
                 + [_const_spec(a.shape) for a in weights],
        out_specs=[row_spec(d_model + LANES), row_spec(LANES),
                   pl.BlockSpec((SUBLANES, LANES), lambda i: (0, 0))],
        out_shape=[jax.ShapeDtypeStruct((t, d_model + LANES), F32),
                   jax.ShapeDtypeStruct((t, LANES), F32),
                   jax.ShapeDtypeStruct((SUBLANES, LANES), F32)],
        scratch_shapes=[pltpu.VMEM((1, LANES), F32)],
        compiler_params=pltpu.CompilerParams(dimension_semantics=("arbitrary",),
                                             vmem_limit_bytes=VMEM_LIMIT_BYTES),
        name="post_attn",
    )(o, ap, sgb, x, *weights)


def _row_permute_kernel(pos_ref, src_ref, *rest, rows, scatter):
    if scatter:
        _, dst_ref, sem = rest
    else:
        dst_ref, sem = rest
    base = pl.program_id(0) * rows

    def issue(r, carry):
        p = pos_ref[base + r]
        if scatter:
            pltpu.make_async_copy(src_ref.at[pl.ds(r, 1)], dst_ref.at[pl.ds(p, 1)], sem).start()
        else:
            pltpu.make_async_copy(src_ref.at[pl.ds(p, 1)], dst_ref.at[pl.ds(r, 1)], sem).start()
        return carry

    lax.fori_loop(0, rows, issue, 0, unroll=PERMUTE_UNROLL)
    if scatter:
        pltpu.make_async_copy(src_ref, dst_ref.at[pl.ds(0, rows)], sem).wait()
    else:
        pltpu.make_async_copy(src_ref.at[pl.ds(0, rows)], dst_ref, sem).wait()


def _row_permute(pos, src, *, n_out, scatter):
    n_rows = pos.shape[0]
    width = src.shape[1]
    rows = MOE_ROWS
    any_spec = pl.BlockSpec(memory_space=pl.ANY)
    tile_spec = pl.BlockSpec((rows, width), lambda i, pos_ref: (i, 0))
    args = [pos, src]
    aliases = {}
    if scatter:
        args.append(jnp.zeros((n_out, width), src.dtype))
        in_specs, out_specs = [tile_spec, any_spec], any_spec
        aliases = {2: 0}
    else:
        in_specs, out_specs = [any_spec], tile_spec
    grid_spec = pltpu.PrefetchScalarGridSpec(
        num_scalar_prefetch=1, grid=(n_rows // rows,), in_specs=in_specs, out_specs=out_specs,
        scratch_shapes=[pltpu.SemaphoreType.DMA(())])
    return pl.pallas_call(
        functools.partial(_row_permute_kernel, rows=rows, scatter=scatter),
        grid_spec=grid_spec,
        out_shape=jax.ShapeDtypeStruct((n_out, width), src.dtype),
        input_output_aliases=aliases,
        compiler_params=pltpu.CompilerParams(dimension_semantics=("arbitrary",)),
        name="row_scatter" if scatter else "row_gather",
    )(*args)


def _moe_kernel(tg_ref, nv_ref, xs_ref, w1_ref, w3_ref, w2_ref, g_ref, b_ref, y_ref, *, alpha, d_model):
    del tg_ref
    i = pl.program_id(0)

    @pl.when(i < nv_ref[0])
    def _():
        x = xs_ref[:, 0:d_model]
        route = xs_ref[:, d_model:d_model + LANES]
        xb = x.astype(BF16)
        lane = lax.broadcasted_iota(jnp.int32, route.shape, 1)
        acc = jnp.zeros_like(x)
        for e in range(EXPERTS_PER_GROUP):
            h1 = _dot(xb, w1_ref[0, e])
            h3 = _dot(xb, w3_ref[0, e])
            h = (h1 * jax.nn.sigmoid(h1)) * h3
            ye = _dot(h.astype(BF16), w2_ref[0, e])
            ge = jnp.sum(jnp.where(lane == e, route, 0.0), axis=1, keepdims=True)
            acc = acc + ge * ye
        y_ref[...] = _layer_norm(alpha * x + acc, g_ref[...], b_ref[...])

    @pl.when(i >= nv_ref[0])
    def _():
        y_ref[...] = jnp.zeros_like(y_ref)


def _experts(x1e, route, totals, w, *, alpha):
    t = x1e.shape[0]
    d_model = x1e.shape[1] - LANES
    _, _, _, d_exp = w['w1'].shape
    rows = MOE_ROWS
    n_tiles = t // rows + N_GROUPS
    group = route[:, ROUTE_GROUP_LANE].astype(jnp.int32)
    rank = route[:, ROUTE_RANK_LANE].astype(jnp.int32)
    count = totals[0, 0:N_GROUPS].astype(jnp.int32)
    tiles_g = (count + rows - 1) // rows
    tile_end = jnp.cumsum(tiles_g)
    pos = (tile_end - tiles_g)[group] * rows + rank
    n_valid = tile_end[N_GROUPS - 1:N_GROUPS]
    tile_group = jnp.minimum(jnp.sum(jnp.arange(n_tiles)[:, None] >= tile_end[None, :], axis=1),
                             N_GROUPS - 1).astype(jnp.int32)

    xs = _row_permute(pos, x1e, n_out=n_tiles * rows, scatter=True)

    live = lambda i, nv: jnp.minimum(i, nv[0] - 1)
    w_spec = lambda a: pl.BlockSpec((1,) + a.shape[1:], lambda i, tg, nv: (tg[live(i, nv)], 0, 0, 0))
    grid_spec = pltpu.PrefetchScalarGridSpec(
        num_scalar_prefetch=2,
        grid=(n_tiles,),
        in_specs=[
            pl.BlockSpec((rows, d_model + LANES), lambda i, tg, nv: (live(i, nv), 0)),
            w_spec(w['w1']), w_spec(w['w3']), w_spec(w['w2']),
            pl.BlockSpec((1, d_model), lambda i, tg, nv: (0, 0)),
            pl.BlockSpec((1, d_model), lambda i, tg, nv: (0, 0)),
        ],
        out_specs=pl.BlockSpec((rows, d_model), lambda i, tg, nv: (i, 0)),
    )
    ys = pl.pallas_call(
        functools.partial(_moe_kernel, alpha=alpha, d_model=d_model),
        grid_spec=grid_spec,
        out_shape=jax.ShapeDtypeStruct((n_tiles * rows, d_model), F32),
        compiler_params=pltpu.CompilerParams(dimension_semantics=("arbitrary",),
                                             vmem_limit_bytes=VMEM_LIMIT_BYTES),
        name="experts",
    )(tile_group, n_valid, xs, w['w1'], w['w3'], w['w2'], w['ln2_g'], w['ln2_b'])
    return _row_permute(pos, ys, n_out=t, scatter=False)


def _prepare_weights(w_in, b_f, conv_w, w_conv_out, w_attn_out, w_o, ln1_g, ln1_b,
                     w_group, w_router, w1, w3, w2, ln2_g, ln2_b, rows_inproj):
    d_model = w_in.shape[0]
    d_conv = w_conv_out.shape[0]
    o_q = 3 * d_conv
    o_k = o_q + D_ATTN
    o_v = o_k + D_ATTN
    o_f = o_v + D_ATTN
    o_g = o_f + N_HEADS
    w_bf = w_in.astype(BF16)
    wm = w_bf[:, 0:o_f]
    wf = jnp.pad(w_bf[:, o_f:o_g], ((0, 0), (0, LANES - N_HEADS)))
    wg = w_bf[:, o_g:]
    bf = jnp.pad(b_f.astype(F32), (0, LANES - N_HEADS)).reshape(1, LANES)

    col = jnp.arange(N_HEADS * LANES)
    col_head, col_lane = col // LANES, col % LANES
    part = jnp.arange(LANES)
    part_k, part_h = part // N_HEADS, part % N_HEADS
    live = (part_k[:, None] < AUG_SPLITS) & (part_h[:, None] == col_head[None, :])
    pq = live & (col_lane[None, :] == AUG_Q_C + part_k[:, None])
    pk = live & (col_lane[None, :] == AUG_Q_C + AUG_SPLITS + part_k[:, None])
    oq = ((col_lane >= AUG_Q_C + AUG_SPLITS) & (col_lane < AUG_Q_C + 2 * AUG_SPLITS)).astype(F32).reshape(1, -1)
    ok = ((col_lane >= AUG_Q_C) & (col_lane < AUG_Q_C + AUG_SPLITS)).astype(F32).reshape(1, -1)
    r = jnp.arange(rows_inproj)
    tri = (r[None, :] <= r[:, None]).astype(BF16)
    r = jnp.arange(MOE_ROWS)
    tri_moe = (r[None, :] <= r[:, None]).astype(BF16)

    wr = jnp.pad(jnp.concatenate([w_group, w_router], axis=1), ((0, 0), (0, LANES - N_GROUPS - N_EXPERTS)))
    wr_hi = wr.astype(BF16)
    wr_lo = (wr - wr_hi.astype(F32)).astype(BF16)
    return dict(
        wm=wm, wf=wf, wg=wg, bf=bf, cw=conv_w.astype(F32), wco=w_conv_out.astype(BF16),
        tri=tri, pq=pq.astype(BF16), pk=pk.astype(BF16), oq=oq, ok=ok,
        wao=w_attn_out.astype(BF16), wo=w_o.astype(BF16),
        ln1_g=ln1_g.reshape(1, -1), ln1_b=ln1_b.reshape(1, -1), wr_hi=wr_hi, wr_lo=wr_lo,
        tri_moe=tri_moe,
        w1=w1.astype(BF16).reshape((N_GROUPS, EXPERTS_PER_GROUP) + w1.shape[1:]),
        w3=w3.astype(BF16).reshape((N_GROUPS, EXPERTS_PER_GROUP) + w3.shape[1:]),
        w2=w2.astype(BF16).reshape((N_GROUPS, EXPERTS_PER_GROUP) + w2.shape[1:]),
        ln2_g=ln2_g.reshape(1, -1), ln2_b=ln2_b.reshape(1, -1),
    )


def kernel(x_prompt, x_sample, cache_k, cache_v, cache_logf, state_conv, page_table, w_in, b_f, conv_w,
           w_conv_out, w_attn_out, w_o, ln1_g, ln1_b, w_group, w_router, w1, w3, w2, ln2_g, ln2_b):
    depth = w_in.shape[0]
    assert depth == 1, "single-layer stack"
    n_p, s_p, d_model = x_prompt.shape
    n_s, s_s, _ = x_sample.shape
    d_conv = w_conv_out.shape[1]
    alpha = (2 * depth) ** 0.25
    w = _prepare_weights(w_in[0], b_f[0], conv_w[0], w_conv_out[0], w_attn_out[0], w_o[0], ln1_g[0], ln1_b[0],
                         w_group[0], w_router[0], w1[0], w3[0], w2[0], ln2_g[0], ln2_b[0], rows_inproj=256)

    xp = x_prompt.reshape(n_p * s_p, d_model)
    qa, kt_p, vt_p, ka, vtb, lft_p, ap, sgb, tail = _inproj(xp, w, sample=False, seq_len=s_p)
    o_p = _prompt_attention(qa, ka, vtb, n_seq=n_p, seq_len=s_p)
    x1e_p, route_p, totals_p = _post_attention(o_p, ap, sgb, xp, w, alpha=alpha)
    y_p = _experts(x1e_p, route_p, totals_p, w, alpha=alpha)

    xs = x_sample.reshape(n_s * s_s, d_model)
    state = state_conv[0]
    s0 = jnp.repeat(state[:, 0, :], s_s, axis=0)
    s1 = jnp.repeat(state[:, 1, :], s_s, axis=0)
    q_s, k_s, v_s, lf_s, ap_s, sgb_s, u_s = _inproj(xs, w, sample=True, seq_len=s_s, state=(s0, s1))
    q5 = q_s.reshape(n_s, s_s, N_HEADS, HEAD_DIM)
    eye_h = jnp.eye(N_HEADS, dtype=BF16)
    qblk = (q5[:, :, :, None, :] * eye_h[None, None, :, :, None]).reshape(n_s, s_s * N_HEADS, D_ATTN)
    n_pool = cache_k.shape[1]
    cache_kt = cache_k[0].transpose(0, 2, 3, 1).reshape(n_pool, D_ATTN, PAGE_SIZE)
    cache_vt = cache_v[0].transpose(0, 2, 3, 1).reshape(n_pool, D_ATTN, PAGE_SIZE)
    cache_lft = cache_logf[0].transpose(0, 2, 1)
    o_blk = _sample_attention(
        qblk, cache_kt, cache_vt, cache_lft, page_table, k_s.reshape(n_s, s_s, D_ATTN),
        v_s.reshape(n_s, s_s, D_ATTN), lf_s.reshape(n_s, s_s, N_HEADS))
    o5 = o_blk.reshape(n_s, s_s, N_HEADS, N_HEADS, HEAD_DIM)
    o_s = jnp.einsum('bthgd,hg->bthd', o5, jnp.eye(N_HEADS, dtype=F32)).reshape(n_s * s_s, D_ATTN).astype(BF16)
    x1e_s, route_s, totals_s = _post_attention(o_s, ap_s, sgb_s, xs, w, alpha=alpha)
    y_s = _experts(x1e_s, route_s, totals_s, w, alpha=alpha)

    return (
        y_p.reshape(n_p, s_p, d_model),
        y_s.reshape(n_s, s_s, d_model),
        kt_p.reshape(1, n_p, N_HEADS, HEAD_DIM, s_p).transpose(0, 1, 4, 2, 3),
        vt_p.reshape(1, n_p, N_HEADS, HEAD_DIM, s_p).transpose(0, 1, 4, 2, 3),
        lft_p.reshape(1, n_p, N_HEADS, s_p).transpose(0, 1, 3, 2),
        tail[:, SUBLANES - (CONV_WIDTH - 1):, :].reshape(1, n_p, CONV_WIDTH - 1, d_conv),
        k_s.reshape(1, n_s, s_s, N_HEADS, HEAD_DIM),
        v_s.reshape(1, n_s, s_s, N_HEADS, HEAD_DIM),
        lf_s.reshape(1, n_s, s_s, N_HEADS),
        u_s.reshape(n_s, s_s, d_conv)[:, s_s - (CONV_WIDTH - 1):, :].reshape(1, n_s, CONV_WIDTH - 1, d_conv),
    )
```

```python
import functools

import jax
import jax.numpy as jnp
from jax import lax
from jax.experimental import pallas as pl
from jax.experimental.pallas import tpu as pltpu

F32 = jnp.float32
BF16 = jnp.bfloat16

LANES = 128
SUBLANES = 8
VMEM_LIMIT_BYTES = 56 * 1024 * 1024

CONV_WIDTH = 3
N_HEADS = 8
HEAD_DIM = 64
D_ATTN = N_HEADS * HEAD_DIM
N_GROUPS = 4
EXPERTS_PER_GROUP = 4
N_EXPERTS = N_GROUPS * EXPERTS_PER_GROUP
LN_EPS = 1e-5
PAGE_SIZE = 128
MASK_VALUE = -1e30
AUG_Q_C = HEAD_DIM
AUG_SPLITS = 3
MOE_ROWS = 512
ROUTE_GROUP_LANE = EXPERTS_PER_GROUP
ROUTE_RANK_LANE = EXPERTS_PER_GROUP + 1
V_PAD_ROWS = 16
V_ROWS = HEAD_DIM + V_PAD_ROWS


def _dot(a, b):
    return jnp.dot(a, b, preferred_element_type=F32)


def _dot_nt(a, b):
    return lax.dot_general(a, b, (((1,), (1,)), ((), ())), preferred_element_type=F32)


def _split3(x):
    h1 = x.astype(BF16)
    r1 = x - h1.astype(F32)
    h2 = r1.astype(BF16)
    r2 = r1 - h2.astype(F32)
    h3 = r2.astype(BF16)
    return h1, h2, h3


def _dot_exact_rhs(a_bf16, x):
    h1, h2, h3 = _split3(x)
    return _dot(a_bf16, h1) + _dot(a_bf16, h2) + _dot(a_bf16, h3)


def _dot_exact_lhs(x, b_bf16):
    h1, h2, h3 = _split3(x)
    return _dot(h1, b_bf16) + _dot(h2, b_bf16) + _dot(h3, b_bf16)


def _dot_nt_exact_rhs(a_bf16, x):
    h1, h2, h3 = _split3(x)
    return _dot_nt(a_bf16, h1) + _dot_nt(a_bf16, h2) + _dot_nt(a_bf16, h3)


def _log_sigmoid(x):
    return jnp.minimum(x, 0.0) - jnp.log1p(jnp.exp(-jnp.abs(x)))


def _layer_norm(r, g, b):
    mu = jnp.mean(r, axis=-1, keepdims=True)
    d = r - mu
    var = jnp.mean(d * d, axis=-1, keepdims=True)
    return d * lax.rsqrt(var + LN_EPS) * g + b


def _const_spec(shape):
    nd = len(shape)
    return pl.BlockSpec(shape, lambda *_: (0,) * nd, pipeline_mode=pl.Buffered(1))


def _inproj_kernel(*refs, rows, tiles_per_seq, sample, d_conv, d_model):
    if sample:
        (x_ref, wm_ref, wf_ref, wg_ref, bf_ref, cw_ref, wco_ref, s0_ref, s1_ref,
         q_ref, k_ref, v_ref, lf_ref, ap_ref, sgb_ref, u_ref, ubuf) = refs
    else:
        (x_ref, wm_ref, wf_ref, wg_ref, bf_ref, cw_ref, wco_ref, tri_ref, pq_ref, pk_ref,
         oq_ref, ok_ref,
         q_ref, k_ref, v_ref, ka_ref, vb_ref, lf_ref, ap_ref, sgb_ref, tail_ref, ubuf, ccarry) = refs
    i = pl.program_id(0)
    xb = x_ref[...].astype(BF16)

    xc = _dot(xb, wm_ref[:, 0:d_conv])
    bg = _dot(xb, wm_ref[:, d_conv:2 * d_conv])
    cg = _dot(xb, wm_ref[:, 2 * d_conv:3 * d_conv])
    u = cg * xc
    if sample:
        ubuf[0:SUBLANES, :] = jnp.zeros((SUBLANES, d_conv), F32)
    else:
        @pl.when(i % tiles_per_seq == 0)
        def _():
            ubuf[0:SUBLANES, :] = jnp.zeros((SUBLANES, d_conv), F32)
    ubuf[SUBLANES:SUBLANES + rows, :] = u
    prev1 = ubuf[SUBLANES - 1:SUBLANES - 1 + rows, :]
    prev2 = ubuf[SUBLANES - 2:SUBLANES - 2 + rows, :]
    if sample:
        pos = lax.broadcasted_iota(jnp.int32, (rows, 1), 0) % 4
        prev1 = jnp.where(pos == 0, s1_ref[...], prev1)
        prev2 = jnp.where(pos == 0, s0_ref[...], jnp.where(pos == 1, s1_ref[...], prev2))
        u_ref[...] = u
    else:
        ubuf[0:SUBLANES, :] = u[rows - SUBLANES:rows, :]

        @pl.when(i % tiles_per_seq == tiles_per_seq - 1)
        def _():
            tail_ref[0] = u[rows - SUBLANES:rows, :]
    conv = cw_ref[0:1, :] * prev2 + cw_ref[1:2, :] * prev1 + cw_ref[2:3, :] * u
    ya = _dot((bg * conv).astype(BF16), wco_ref[...])

    q = _dot(xb, wm_ref[:, 3 * d_conv:3 * d_conv + D_ATTN]) * (HEAD_DIM ** -0.5)
    k = _dot(xb, wm_ref[:, 3 * d_conv + D_ATTN:3 * d_conv + 2 * D_ATTN])
    v = _dot(xb, wm_ref[:, 3 * d_conv + 2 * D_ATTN:3 * d_conv + 3 * D_ATTN])
    fg = _dot(xb, wf_ref[...]) + bf_ref[...]
    lane = lax.broadcasted_iota(jnp.int32, (rows, LANES), 1)
    logf = jnp.where(lane < N_HEADS, _log_sigmoid(fg), 0.0)
    if sample:
        k_ref[...] = k
        v_ref[...] = v
        lf_ref[...] = logf[:, 0:N_HEADS]
        q_ref[...] = q.astype(BF16)
    else:
        k_ref[0] = k.T
        vt = v.T
        v_ref[0] = vt
        lf_ref[0] = logf.T[0:N_HEADS, :]
        ones_blk = jnp.where(lax.broadcasted_iota(jnp.int32, (V_PAD_ROWS, rows), 0) == 0, 1.0, 0.0)
        vb_ref[0] = jnp.concatenate(
            [blk for h in range(N_HEADS) for blk in (vt[h * HEAD_DIM:(h + 1) * HEAD_DIM, :], ones_blk)],
            axis=0).astype(BF16)

        @pl.when(i % tiles_per_seq == 0)
        def _():
            ccarry[...] = jnp.zeros((1, LANES), F32)
        c = _dot_exact_rhs(tri_ref[...], logf) + ccarry[...]
        ccarry[...] = c[rows - 1:rows, :]
        c1, c2, c3 = _split3(c)
        c_parts = (c1.astype(F32) + pltpu.roll(c2.astype(F32), N_HEADS, 1)
                   + pltpu.roll(c3.astype(F32), 2 * N_HEADS, 1)).astype(BF16)
        aug_q = _dot(c_parts, pq_ref[...]) + oq_ref[...]
        aug_k = ok_ref[...] - _dot(c_parts, pk_ref[...])

        def spread_heads(x, aug):
            groups = []
            for h in range(N_HEADS):
                slab = x[:, (h // 2) * LANES:(h // 2 + 1) * LANES]
                if h % 2:
                    slab = pltpu.roll(slab, HEAD_DIM, 1)
                groups.append(jnp.where(lane < HEAD_DIM, slab, aug[:, h * LANES:(h + 1) * LANES]))
            return jnp.concatenate(groups, axis=1).astype(BF16)

        q_ref[...] = spread_heads(q, aug_q)
        ka_ref[...] = spread_heads(k, aug_k)

    ga = _dot(xb, wg_ref[:, 0:d_model])
    gb = _dot(xb, wg_ref[:, d_model:2 * d_model])
    ap_ref[...] = jax.nn.sigmoid(ga) * ya
    sgb_ref[...] = jax.nn.sigmoid(gb)


def _inproj(x, w, *, sample, seq_len, state=None):
    t, d_model = x.shape
    d_conv = w['wco'].shape[0]
    rows = 256
    tiles_per_seq = max(seq_len // rows, 1)
    n_tiles = t // rows
    row_spec = lambda width: pl.BlockSpec((rows, width), lambda i: (i, 0))
    weights = [w['wm'], w['wf'], w['wg'], w['bf'], w['cw'], w['wco']]
    in_specs = [row_spec(d_model)] + [_const_spec(a.shape) for a in weights]
    args = [x] + weights
    if sample:
        args += [state[0], state[1]]
        in_specs += [row_spec(d_conv), row_spec(d_conv)]
        out_shape = [
            jax.ShapeDtypeStruct((t, D_ATTN), BF16),
            jax.ShapeDtypeStruct((t, D_ATTN), F32),
            jax.ShapeDtypeStruct((t, D_ATTN), F32),
            jax.ShapeDtypeStruct((t, N_HEADS), F32),
            jax.ShapeDtypeStruct((t, d_model), F32),
            jax.ShapeDtypeStruct((t, d_model), F32),
            jax.ShapeDtypeStruct((t, d_conv), F32),
        ]
        out_specs = [row_spec(D_ATTN), row_spec(D_ATTN), row_spec(D_ATTN), row_spec(N_HEADS),
                     row_spec(d_model), row_spec(d_model), row_spec(d_conv)]
        scratch = [pltpu.VMEM((rows + SUBLANES, d_conv), F32)]
    else:
        consts = [w['tri'], w['pq'], w['pk'], w['oq'], w['ok']]
        args += consts
        in_specs += [_const_spec(a.shape) for a in consts]
        n_seq = t // seq_len
        out_shape = [
            jax.ShapeDtypeStruct((t, N_HEADS * LANES), BF16),
            jax.ShapeDtypeStruct((n_seq, D_ATTN, seq_len), F32),
            jax.ShapeDtypeStruct((n_seq, D_ATTN, seq_len), F32),
            jax.ShapeDtypeStruct((t, N_HEADS * LANES), BF16),
            jax.ShapeDtypeStruct((n_tiles, N_HEADS * V_ROWS, rows), BF16),
            jax.ShapeDtypeStruct((n_seq, N_HEADS, seq_len), F32),
            jax.ShapeDtypeStruct((t, d_model), F32),
            jax.ShapeDtypeStruct((t, d_model), F32),
            jax.ShapeDtypeStruct((n_seq, SUBLANES, d_conv), F32),
        ]
        pos_minor = lambda height: pl.BlockSpec((1, height, rows),
                                                lambda i: (i // tiles_per_seq, 0, i % tiles_per_seq))
        out_specs = [row_spec(N_HEADS * LANES), pos_minor(D_ATTN), pos_minor(D_ATTN), row_spec(N_HEADS * LANES),
                     pl.BlockSpec((1, N_HEADS * V_ROWS, rows), lambda i: (i, 0, 0)), pos_minor(N_HEADS),
                     row_spec(d_model), row_spec(d_model),
                     pl.BlockSpec((1, SUBLANES, d_conv), lambda i: (i // tiles_per_seq, 0, 0))]
        scratch = [pltpu.VMEM((rows + SUBLANES, d_conv), F32), pltpu.VMEM((1, LANES), F32)]
    kern = functools.partial(_inproj_kernel, rows=rows, tiles_per_seq=tiles_per_seq, sample=sample,
                             d_conv=d_conv, d_model=d_model)
    return pl.pallas_call(
        kern,
        grid=(n_tiles,),
        in_specs=in_specs,
        out_specs=out_specs,
        out_shape=out_shape,
        scratch_shapes=scratch,
        compiler_params=pltpu.CompilerParams(dimension_semantics=("arbitrary",),
                                             vmem_limit_bytes=VMEM_LIMIT_BYTES),
        name="inproj_sample" if sample else "inproj_prompt",
    )(*args)


def _attn_kernel(q_ref, k_ref, vt_ref, o_ref, m_sc, acc_sc, *, tq, v_chunk, q_chunk):
    i = pl.program_id(2)
    chunks = tq // v_chunk
    for hl in range(2):
        m_sc[hl] = jnp.full((1, tq), MASK_VALUE, F32)
        acc_sc[hl] = jnp.zeros((V_ROWS, tq), F32)

    def step(j, masked):
        start = pl.multiple_of(j * tq, tq)
        blocks = [(hl, qc) for hl in range(2) for qc in range(tq // q_chunk)]
        m_old = [m_sc[hl] for hl in range(2)]
        acc_old = [acc_sc[hl] for hl in range(2)]
        sts = [_dot_nt(k_ref[pl.ds(start, tq), hl * LANES:(hl + 1) * LANES],
                       q_ref[qc * q_chunk:(qc + 1) * q_chunk, hl * LANES:(hl + 1) * LANES]) for hl, qc in blocks]
        m_out, acc_out = [[], []], [[], []]
        for (hl, qc), st in zip(blocks, sts):
            qs = slice(qc * q_chunk, (qc + 1) * q_chunk)
            if masked:
                key = lax.broadcasted_iota(jnp.int32, (tq, q_chunk), 0)
                qry = lax.broadcasted_iota(jnp.int32, (tq, q_chunk), 1) + qc * q_chunk
                st = jnp.where(key <= qry, st, MASK_VALUE)
            m_prev = m_old[hl][:, qs]
            m_new = jnp.maximum(m_prev, jnp.max(st, axis=0, keepdims=True))
            alpha = jnp.exp(m_prev - m_new)
            pb = jnp.exp(st - m_new).astype(BF16)
            acc = alpha * acc_old[hl][:, qs]
            for c in range(chunks):
                vt = vt_ref[j * chunks + c, hl * V_ROWS:(hl + 1) * V_ROWS, :]
                acc = acc + _dot(vt, pb[c * v_chunk:(c + 1) * v_chunk, :])
            acc_out[hl].append(acc)
            m_out[hl].append(m_new)
        for hl in range(2):
            m_sc[hl] = jnp.concatenate(m_out[hl], axis=1)
            acc_sc[hl] = jnp.concatenate(acc_out[hl], axis=1)

    def body(j, carry):
        step(j, False)
        return carry

    lax.fori_loop(0, i, body, 0)
    step(i, True)
    ot = jnp.concatenate([acc_sc[hl, 0:HEAD_DIM, :] / acc_sc[hl, HEAD_DIM:HEAD_DIM + 1, :] for hl in range(2)],
                         axis=0)
    o_ref[...] = ot.T.astype(BF16)


def _prompt_attention(qa, ka, vtb, *, n_seq, seq_len):
    t = qa.shape[0]
    v_chunk = vtb.shape[2]
    tq = 512
    qt = seq_len // tq
    n_pairs = N_HEADS // 2
    return pl.pallas_call(
        functools.partial(_attn_kernel, tq=tq, v_chunk=v_chunk, q_chunk=LANES),
        grid=(n_seq, n_pairs, qt),
        in_specs=[
            pl.BlockSpec((tq, 2 * LANES), lambda n, hp, i: (n * qt + i, hp)),
            pl.BlockSpec((seq_len, 2 * LANES), lambda n, hp, i: (n, hp)),
            pl.BlockSpec((seq_len // v_chunk, 2 * V_ROWS, v_chunk), lambda n, hp, i: (n, hp, 0)),
        ],
        out_specs=pl.BlockSpec((tq, LANES), lambda n, hp, i: (n * qt + i, hp)),
        out_shape=jax.ShapeDtypeStruct((t, D_ATTN), BF16),
        scratch_shapes=[pltpu.VMEM((2, 1, tq), F32), pltpu.VMEM((2, V_ROWS, tq), F32)],
        compiler_params=pltpu.CompilerParams(dimension_semantics=("arbitrary", "arbitrary", "arbitrary"),
                                             vmem_limit_bytes=VMEM_LIMIT_BYTES),
        name="attn_prompt",
    )(qa, ka, vtb)


def _sample_attn_kernel(pt_ref, q_ref, kn_ref, vn_ref, fn_ref, e_ref, suffix_ref, later_ref, tri_ref,
                        kt_hbm, vt_hbm, lft_hbm, o_ref, k_buf, v_buf, f_buf, sems, pad_sc, kn_sc, vn_sc,
                        *, n_pages, n_new):
    b = pl.program_id(0)
    slot = b % 2
    pools = ((kt_hbm, k_buf), (vt_hbm, v_buf), (lft_hbm, f_buf))

    def start_pages(seq, to_slot):
        for j in range(n_pages):
            page = pt_ref[seq * n_pages + j]
            for a, (hbm, buf) in enumerate(pools):
                pltpu.make_async_copy(hbm.at[page], buf.at[to_slot, j], sems.at[a, to_slot]).start()

    @pl.when(b == 0)
    def _():
        start_pages(0, 0)

    @pl.when(b + 1 < pl.num_programs(0))
    def _():
        start_pages(b + 1, 1 - slot)

    for a, (hbm, buf) in enumerate(pools):
        pltpu.make_async_copy(hbm.at[pl.ds(0, n_pages)], buf.at[slot], sems.at[a, slot]).wait()

    n_rows = N_HEADS * n_new
    qb = q_ref[0]
    e = e_ref[...]

    pad_sc[...] = jnp.zeros((PAGE_SIZE, LANES), F32)
    pad_sc[0:n_new, 0:N_HEADS] = fn_ref[0]
    kn_sc[...] = jnp.zeros((PAGE_SIZE, D_ATTN), F32)
    kn_sc[0:n_new, :] = kn_ref[0]
    vn_sc[...] = jnp.zeros((PAGE_SIZE, D_ATTN), F32)
    vn_sc[0:n_new, :] = vn_ref[0]
    cn = _dot_exact_rhs(tri_ref[...], pad_sc[...])
    cn_rows = _dot_nt_exact_rhs(e, cn)
    row_t = lax.broadcasted_iota(jnp.int32, (n_rows, LANES), 0) // N_HEADS
    col = lax.broadcasted_iota(jnp.int32, (n_rows, LANES), 1)
    cnq = jnp.sum(jnp.where(col == row_t, cn_rows, 0.0), axis=1, keepdims=True)
    s_new = _dot_nt(qb, kn_sc[...].astype(BF16)) + (cnq - cn_rows)
    s_new = jnp.where(col <= row_t, s_new, MASK_VALUE)

    lf = f_buf[slot].reshape(n_pages * N_HEADS, PAGE_SIZE)
    incl = _dot_exact_lhs(lf, suffix_ref[...])
    tot = jnp.broadcast_to(incl[:, 0:1], incl.shape)
    d = incl - lf + _dot_exact_rhs(later_ref[...], tot)
    s_pages = []
    for j in range(n_pages):
        bias = jnp.concatenate([d[j * N_HEADS:(j + 1) * N_HEADS, :]] * n_new, axis=0)
        s_pages.append(_dot(qb, k_buf[slot, j].astype(BF16)) + bias + cnq)

    m_blk = s_new
    for s in s_pages:
        m_blk = jnp.maximum(m_blk, s)
    m = jnp.max(m_blk, axis=1, keepdims=True)
    p_new = jnp.exp(s_new - m)
    l_blk = p_new
    acc = _dot(p_new.astype(BF16), vn_sc[...].astype(BF16))
    for j in range(n_pages):
        p = jnp.exp(s_pages[j] - m)
        l_blk = l_blk + p
        acc = acc + _dot_nt(p.astype(BF16), v_buf[slot, j].astype(BF16))
    o_ref[0] = acc / jnp.sum(l_blk, axis=1, keepdims=True)


def _sample_attention(qblk, cache_kt, cache_vt, cache_lft, page_table, k_new, v_new, lf_new):
    n_seq, n_rows, _ = qblk.shape
    n_new = k_new.shape[1]
    n_pages = page_table.shape[1]
    pt = page_table.reshape(-1)
    row_head = jnp.arange(n_rows) % N_HEADS
    e = (row_head[:, None] == jnp.arange(LANES)[None, :]).astype(BF16)
    pos = jnp.arange(PAGE_SIZE)
    suffix = (pos[:, None] >= pos[None, :]).astype(BF16)
    tri = (pos[None, :] <= pos[:, None]).astype(BF16)
    r = jnp.arange(n_pages * N_HEADS)
    later = ((r[:, None] % N_HEADS == r[None, :] % N_HEADS)
             & (r[None, :] // N_HEADS > r[:, None] // N_HEADS)).astype(BF16)

    seq_map = lambda b, pt_ref: (b, 0, 0)
    const_map = lambda b, pt_ref: (0, 0)
    any_spec = pl.BlockSpec(memory_space=pl.ANY)
    in_specs = [pl.BlockSpec((1, n_rows, D_ATTN), seq_map),
                pl.BlockSpec((1, n_new, D_ATTN), seq_map), pl.BlockSpec((1, n_new, D_ATTN), seq_map),
                pl.BlockSpec((1, n_new, N_HEADS), seq_map),
                pl.BlockSpec(e.shape, const_map), pl.BlockSpec(suffix.shape, const_map),
                pl.BlockSpec(later.shape, const_map), pl.BlockSpec(tri.shape, const_map),
                any_spec, any_spec, any_spec]
    grid_spec = pltpu.PrefetchScalarGridSpec(
        num_scalar_prefetch=1,
        grid=(n_seq,),
        in_specs=in_specs,
        out_specs=pl.BlockSpec((1, n_rows, D_ATTN), seq_map),
        scratch_shapes=[pltpu.VMEM((2, n_pages, D_ATTN, PAGE_SIZE), F32),
                        pltpu.VMEM((2, n_pages, D_ATTN, PAGE_SIZE), F32),
                        pltpu.VMEM((2, n_pages, N_HEADS, PAGE_SIZE), F32),
                        pltpu.SemaphoreType.DMA((3, 2)),
                        pltpu.VMEM((PAGE_SIZE, LANES), F32),
                        pltpu.VMEM((PAGE_SIZE, D_ATTN), F32), pltpu.VMEM((PAGE_SIZE, D_ATTN), F32)],
    )
    args = [pt, qblk, k_new, v_new, lf_new, e, suffix, later, tri, cache_kt, cache_vt, cache_lft]
    return pl.pallas_call(
        functools.partial(_sample_attn_kernel, n_pages=n_pages, n_new=n_new),
        grid_spec=grid_spec,
        out_shape=jax.ShapeDtypeStruct((n_seq, n_rows, D_ATTN), F32),
        compiler_params=pltpu.CompilerParams(dimension_semantics=("arbitrary",),
                                             vmem_limit_bytes=VMEM_LIMIT_BYTES),
        name="attn_sample",
    )(*args)


def _route(logits):
    rows = logits.shape[0]
    lane = lax.broadcasted_iota(jnp.int32, (rows, LANES), 1).astype(F32)
    big = float(LANES)
    in_g = lane < N_GROUPS
    gl = jnp.where(in_g, logits, MASK_VALUE)
    m_g = jnp.max(gl, axis=1, keepdims=True)
    z_g = jnp.sum(jnp.exp(gl - m_g), axis=1, keepdims=True)
    p_g = 1.0 / z_g
    g_idx = jnp.min(jnp.where(in_g & (gl == m_g), lane, big), axis=1, keepdims=True)
    lo = N_GROUPS + EXPERTS_PER_GROUP * g_idx
    sel = (lane >= lo) & (lane < lo + EXPERTS_PER_GROUP)
    el = jnp.where(sel, logits, MASK_VALUE)
    m_e = jnp.max(el, axis=1, keepdims=True)
    ex = jnp.exp(el - m_e)
    ep = ex / jnp.sum(ex, axis=1, keepdims=True)
    v1 = jnp.max(jnp.where(sel, ep, -1.0), axis=1, keepdims=True)
    i1 = jnp.min(jnp.where(sel & (ep == v1), lane, big), axis=1, keepdims=True)
    sel2 = sel & (lane != i1)
    v2 = jnp.max(jnp.where(sel2, ep, -1.0), axis=1, keepdims=True)
    i2 = jnp.min(jnp.where(sel2 & (ep == v2), lane, big), axis=1, keepdims=True)
    denom = v1 + v2
    w1 = p_g * (v1 / denom)
    w2 = p_g * (v2 / denom)
    gate4 = jnp.where(lane == i1 - lo, w1, 0.0) + jnp.where(lane == i2 - lo, w2, 0.0)
    return gate4, g_idx


def _post_kernel(o_ref, ap_ref, sgb_ref, x_ref, wao_ref, wo_ref, g_ref, b_ref, wrh_ref, wrl_ref, tri_ref,
                 x1e_ref, route_ref, totals_ref, count_sc, *, alpha, d_model):
    i = pl.program_id(0)
    rows = x_ref.shape[0]
    yb = _dot(o_ref[...], wao_ref[...])
    mix = _dot((ap_ref[...] + sgb_ref[...] * yb).astype(BF16), wo_ref[...])
    x1 = _layer_norm(alpha * x_ref[...] + mix, g_ref[...], b_ref[...])
    xh = x1.astype(BF16)
    xl = (x1 - xh.astype(F32)).astype(BF16)
    logits = _dot(xh, wrh_ref[...]) + _dot(xl, wrh_ref[...]) + _dot(xh, wrl_ref[...])
    gate4, g_idx = _route(logits)

    @pl.when(i == 0)
    def _():
        count_sc[...] = jnp.zeros((1, LANES), F32)
    lane = lax.broadcasted_iota(jnp.int32, (rows, LANES), 1).astype(F32)
    onehot = jnp.where(lane == g_idx, 1.0, 0.0)
    count = _dot(tri_ref[...], onehot.astype(BF16)) + count_sc[...]
    rank = jnp.sum(onehot * count, axis=1, keepdims=True) - 1.0
    count_sc[...] = count[rows - 1:rows, :]
    totals_ref[...] = jnp.broadcast_to(count[rows - 1:rows, :], (SUBLANES, LANES))
    route = (gate4 + jnp.where(lane == ROUTE_GROUP_LANE, g_idx, 0.0)
             + jnp.where(lane == ROUTE_RANK_LANE, rank, 0.0))
    route_ref[...] = route
    x1e_ref[:, 0:d_model] = x1
    x1e_ref[:, d_model:d_model + LANES] = route


def _post_attention(o, ap, sgb, x, w, *, alpha):
    t, d_model = x.shape
    rows = MOE_ROWS
    row_spec = lambda width: pl.BlockSpec((rows, width), lambda i: (i, 0))
    weights = [w['wao'], w['wo'], w['ln1_g'], w['ln1_b'], w['wr_hi'], w['wr_lo'], w['tri_moe']]
    return pl.pallas_call(
        functools.partial(_post_kernel, alpha=alpha, d_model=d_model),
        grid=(t // rows,),
        in_specs=[row_spec(D_ATTN), row_spec(d_model), row_spec(d_model), row_spec(d_model)]
                 + [_const_spec(a.shape) for a in weights],
        out_specs=[row_spec(d_model + LANES), row_spec(LANES),
                   pl.BlockSpec((SUBLANES, LANES), lambda i: (0, 0))],
        out_shape=[jax.ShapeDtypeStruct((t, d_model + LANES), F32),
                   jax.ShapeDtypeStruct((t, LANES), F32),
                   jax.ShapeDtypeStruct((SUBLANES, LANES), F32)],
        scratch_shapes=[pltpu.VMEM((1, LANES), F32)],
        compiler_params=pltpu.CompilerParams(dimension_semantics=("arbitrary",),
                                             vmem_limit_bytes=VMEM_LIMIT_BYTES),
        name="post_attn",
    )(o, ap, sgb, x, *weights)


def _row_permute_kernel(pos_ref, src_ref, *rest, rows, scatter):
    if scatter:
        _, dst_ref, sem = rest
    else:
        dst_ref, sem = rest
    base = pl.program_id(0) * rows

    def issue(r, carry):
        p = pos_ref[base + r]
        if scatter:
            pltpu.make_async_copy(src_ref.at[pl.ds(r, 1)], dst_ref.at[pl.ds(p, 1)], sem).start()
        else:
            pltpu.make_async_copy(src_ref.at[pl.ds(p, 1)], dst_ref.at[pl.ds(r, 1)], sem).start()
        return carry

    lax.fori_loop(0, rows, issue, 0, unroll=True)
    if scatter:
        pltpu.make_async_copy(src_ref, dst_ref.at[pl.ds(0, rows)], sem).wait()
    else:
        pltpu.make_async_copy(src_ref.at[pl.ds(0, rows)], dst_ref, sem).wait()


def _row_permute(pos, src, *, n_out, scatter):
    n_rows = pos.shape[0]
    width = src.shape[1]
    rows = MOE_ROWS
    any_spec = pl.BlockSpec(memory_space=pl.ANY)
    tile_spec = pl.BlockSpec((rows, width), lambda i, pos_ref: (i, 0))
    args = [pos, src]
    aliases = {}
    if scatter:
        args.append(jnp.zeros((n_out, width), src.dtype))
        in_specs, out_specs = [tile_spec, any_spec], any_spec
        aliases = {2: 0}
    else:
        in_specs, out_specs = [any_spec], tile_spec
    grid_spec = pltpu.PrefetchScalarGridSpec(
        num_scalar_prefetch=1, grid=(n_rows // rows,), in_specs=in_specs, out_specs=out_specs,
        scratch_shapes=[pltpu.SemaphoreType.DMA(())])
    return pl.pallas_call(
        functools.partial(_row_permute_kernel, rows=rows, scatter=scatter),
        grid_spec=grid_spec,
        out_shape=jax.ShapeDtypeStruct((n_out, width), src.dtype),
        input_output_aliases=aliases,
        compiler_params=pltpu.CompilerParams(dimension_semantics=("arbitrary",)),
        name="row_scatter" if scatter else "row_gather",
    )(*args)


def _moe_kernel(tg_ref, nv_ref, xs_ref, w1_ref, w3_ref, w2_ref, g_ref, b_ref, y_ref, *, alpha, d_model):
    del tg_ref
    i = pl.program_id(0)

    @pl.when(i < nv_ref[0])
    def _():
        x = xs_ref[:, 0:d_model]
        route = xs_ref[:, d_model:d_model + LANES]
        xb = x.astype(BF16)
        lane = lax.broadcasted_iota(jnp.int32, route.shape, 1)
        acc = jnp.zeros_like(x)
        for e in range(EXPERTS_PER_GROUP):
            h1 = _dot(xb, w1_ref[0, e])
            h3 = _dot(xb, w3_ref[0, e])
            h = (h1 * jax.nn.sigmoid(h1)) * h3
            ye = _dot(h.astype(BF16), w2_ref[0, e])
            ge = jnp.sum(jnp.where(lane == e, route, 0.0), axis=1, keepdims=True)
            acc = acc + ge * ye
        y_ref[...] = _layer_norm(alpha * x + acc, g_ref[...], b_ref[...])

    @pl.when(i >= nv_ref[0])
    def _():
        y_ref[...] = jnp.zeros_like(y_ref)


def _experts(x1e, route, totals, w, *, alpha):
    t = x1e.shape[0]
    d_model = x1e.shape[1] - LANES
    _, _, _, d_exp = w['w1'].shape
    rows = MOE_ROWS
    n_tiles = t // rows + N_GROUPS
    group = route[:, ROUTE_GROUP_LANE].astype(jnp.int32)
    rank = route[:, ROUTE_RANK_LANE].astype(jnp.int32)
    count = totals[0, 0:N_GROUPS].astype(jnp.int32)
    tiles_g = (count + rows - 1) // rows
    tile_end = jnp.cumsum(tiles_g)
    pos = (tile_end - tiles_g)[group] * rows + rank
    n_valid = tile_end[N_GROUPS - 1:N_GROUPS]
    tile_group = jnp.minimum(jnp.sum(jnp.arange(n_tiles)[:, None] >= tile_end[None, :], axis=1),
                             N_GROUPS - 1).astype(jnp.int32)

    xs = _row_permute(pos, x1e, n_out=n_tiles * rows, scatter=True)

    live = lambda i, nv: jnp.minimum(i, nv[0] - 1)
    w_spec = lambda a: pl.BlockSpec((1,) + a.shape[1:], lambda i, tg, nv: (tg[live(i, nv)], 0, 0, 0))
    grid_spec = pltpu.PrefetchScalarGridSpec(
        num_scalar_prefetch=2,
        grid=(n_tiles,),
        in_specs=[
            pl.BlockSpec((rows, d_model + LANES), lambda i, tg, nv: (live(i, nv), 0)),
            w_spec(w['w1']), w_spec(w['w3']), w_spec(w['w2']),
            pl.BlockSpec((1, d_model), lambda i, tg, nv: (0, 0)),
            pl.BlockSpec((1, d_model), lambda i, tg, nv: (0, 0)),
        ],
        out_specs=pl.BlockSpec((rows, d_model), lambda i, tg, nv: (i, 0)),
    )
    ys = pl.pallas_call(
        functools.partial(_moe_kernel, alpha=alpha, d_model=d_model),
        grid_spec=grid_spec,
        out_shape=jax.ShapeDtypeStruct((n_tiles * rows, d_model), F32),
        compiler_params=pltpu.CompilerParams(dimension_semantics=("arbitrary",),
                                             vmem_limit_bytes=VMEM_LIMIT_BYTES),
        name="experts",
    )(tile_group, n_valid, xs, w['w1'], w['w3'], w['w2'], w['ln2_g'], w['ln2_b'])
    return _row_permute(pos, ys, n_out=t, scatter=False)


def _prepare_weights(w_in, b_f, conv_w, w_conv_out, w_attn_out, w_o, ln1_g, ln1_b,
                     w_group, w_router, w1, w3, w2, ln2_g, ln2_b, rows_inproj):
    d_model = w_in.shape[0]
    d_conv = w_conv_out.shape[0]
    o_q = 3 * d_conv
    o_k = o_q + D_ATTN
    o_v = o_k + D_ATTN
    o_f = o_v + D_ATTN
    o_g = o_f + N_HEADS
    w_bf = w_in.astype(BF16)
    wm = w_bf[:, 0:o_f]
    wf = jnp.pad(w_bf[:, o_f:o_g], ((0, 0), (0, LANES - N_HEADS)))
    wg = w_bf[:, o_g:]
    bf = jnp.pad(b_f.astype(F32), (0, LANES - N_HEADS)).reshape(1, LANES)

    col = jnp.arange(N_HEADS * LANES)
    col_head, col_lane = col // LANES, col % LANES
    part = jnp.arange(LANES)
    part_k, part_h = part // N_HEADS, part % N_HEADS
    live = (part_k[:, None] < AUG_SPLITS) & (part_h[:, None] == col_head[None, :])
    pq = live & (col_lane[None, :] == AUG_Q_C + part_k[:, None])
    pk = live & (col_lane[None, :] == AUG_Q_C + AUG_SPLITS + part_k[:, None])
    oq = ((col_lane >= AUG_Q_C + AUG_SPLITS) & (col_lane < AUG_Q_C + 2 * AUG_SPLITS)).astype(F32).reshape(1, -1)
    ok = ((col_lane >= AUG_Q_C) & (col_lane < AUG_Q_C + AUG_SPLITS)).astype(F32).reshape(1, -1)
    r = jnp.arange(rows_inproj)
    tri = (r[None, :] <= r[:, None]).astype(BF16)
    r = jnp.arange(MOE_ROWS)
    tri_moe = (r[None, :] <= r[:, None]).astype(BF16)

    wr = jnp.pad(jnp.concatenate([w_group, w_router], axis=1), ((0, 0), (0, LANES - N_GROUPS - N_EXPERTS)))
    wr_hi = wr.astype(BF16)
    wr_lo = (wr - wr_hi.astype(F32)).astype(BF16)
    return dict(
        wm=wm, wf=wf, wg=wg, bf=bf, cw=conv_w.astype(F32), wco=w_conv_out.astype(BF16),
        tri=tri, pq=pq.astype(BF16), pk=pk.astype(BF16), oq=oq, ok=ok,
        wao=w_attn_out.astype(BF16), wo=w_o.astype(BF16),
        ln1_g=ln1_g.reshape(1, -1), ln1_b=ln1_b.reshape(1, -1), wr_hi=wr_hi, wr_lo=wr_lo,
        tri_moe=tri_moe,
        w1=w1.astype(BF16).reshape((N_GROUPS, EXPERTS_PER_GROUP) + w1.shape[1:]),
        w3=w3.astype(BF16).reshape((N_GROUPS, EXPERTS_PER_GROUP) + w3.shape[1:]),
        w2=w2.astype(BF16).reshape((N_GROUPS, EXPERTS_PER_GROUP) + w2.shape[1:]),
        ln2_g=ln2_g.reshape(1, -1), ln2_b=ln2_b.reshape(1, -1),
    )


def kernel(x_prompt, x_sample, cache_k, cache_v, cache_logf, state_conv, page_table, w_in, b_f, conv_w,
           w_conv_out, w_attn_out, w_o, ln1_g, ln1_b, w_group, w_router, w1, w3, w2, ln2_g, ln2_b):
    depth = w_in.shape[0]
    assert depth == 1, "single-layer stack"
    n_p, s_p, d_model = x_prompt.shape
    n_s, s_s, _ = x_sample.shape
    d_conv = w_conv_out.shape[1]
    alpha = (2 * depth) ** 0.25
    w = _prepare_weights(w_in[0], b_f[0], conv_w[0], w_conv_out[0], w_attn_out[0], w_o[0], ln1_g[0], ln1_b[0],
                         w_group[0], w_router[0], w1[0], w3[0], w2[0], ln2_g[0], ln2_b[0], rows_inproj=256)

    xp = x_prompt.reshape(n_p * s_p, d_model)
    qa, kt_p, vt_p, ka, vtb, lft_p, ap, sgb, tail = _inproj(xp, w, sample=False, seq_len=s_p)
    o_p = _prompt_attention(qa, ka, vtb, n_seq=n_p, seq_len=s_p)
    x1e_p, route_p, totals_p = _post_attention(o_p, ap, sgb, xp, w, alpha=alpha)
    y_p = _experts(x1e_p, route_p, totals_p, w, alpha=alpha)

    xs = x_sample.reshape(n_s * s_s, d_model)
    state = state_conv[0]
    s0 = jnp.repeat(state[:, 0, :], s_s, axis=0)
    s1 = jnp.repeat(state[:, 1, :], s_s, axis=0)
    q_s, k_s, v_s, lf_s, ap_s, sgb_s, u_s = _inproj(xs, w, sample=True, seq_len=s_s, state=(s0, s1))
    q5 = q_s.reshape(n_s, s_s, N_HEADS, HEAD_DIM)
    eye_h = jnp.eye(N_HEADS, dtype=BF16)
    qblk = (q5[:, :, :, None, :] * eye_h[None, None, :, :, None]).reshape(n_s, s_s * N_HEADS, D_ATTN)
    n_pool = cache_k.shape[1]
    cache_kt = cache_k[0].transpose(0, 2, 3, 1).reshape(n_pool, D_ATTN, PAGE_SIZE)
    cache_vt = cache_v[0].transpose(0, 2, 3, 1).reshape(n_pool, D_ATTN, PAGE_SIZE)
    cache_lft = cache_logf[0].transpose(0, 2, 1)
    o_blk = _sample_attention(
        qblk, cache_kt, cache_vt, cache_lft, page_table, k_s.reshape(n_s, s_s, D_ATTN),
        v_s.reshape(n_s, s_s, D_ATTN), lf_s.reshape(n_s, s_s, N_HEADS))
    o5 = o_blk.reshape(n_s, s_s, N_HEADS, N_HEADS, HEAD_DIM)
    o_s = jnp.einsum('bthgd,hg->bthd', o5, jnp.eye(N_HEADS, dtype=F32)).reshape(n_s * s_s, D_ATTN).astype(BF16)
    x1e_s, route_s, totals_s = _post_attention(o_s, ap_s, sgb_s, xs, w, alpha=alpha)
    y_s = _experts(x1e_s, route_s, totals_s, w, alpha=alpha)

    return (
        y_p.reshape(n_p, s_p, d_model),
        y_s.reshape(n_s, s_s, d_model),
        kt_p.reshape(1, n_p, N_HEADS, HEAD_DIM, s_p).transpose(0, 1, 4, 2, 3),
        vt_p.reshape(1, n_p, N_HEADS, HEAD_DIM, s_p).transpose(0, 1, 4, 2, 3),
        lft_p.reshape(1, n_p, N_HEADS, s_p).transpose(0, 1, 3, 2),
        tail[:, SUBLANES - (CONV_WIDTH - 1):, :].reshape(1, n_p, CONV_WIDTH - 1, d_conv),
        k_s.reshape(1, n_s, s_s, N_HEADS, HEAD_DIM),
        v_s.reshape(1, n_s, s_s, N_HEADS, HEAD_DIM),
        lf_s.reshape(1, n_s, s_s, N_HEADS),
        u_s.reshape(n_s, s_s, d_conv)[:, s_s - (CONV_WIDTH - 1):, :].reshape(1, n_s, CONV_WIDTH - 1, d_conv),
    )
```

```python
import functools

import jax
import jax.numpy as jnp
from jax import lax
from jax.experimental import pallas as pl
from jax.experimental.pallas import tpu as pltpu

F32 = jnp.float32
BF16 = jnp.bfloat16

LANES = 128
SUBLANES = 8
VMEM_LIMIT_BYTES = 56 * 1024 * 1024

CONV_WIDTH = 3
N_HEADS = 8
HEAD_DIM = 64
D_ATTN = N_HEADS * HEAD_DIM
N_GROUPS = 4
EXPERTS_PER_GROUP = 4
N_EXPERTS = N_GROUPS * EXPERTS_PER_GROUP
LN_EPS = 1e-5
PAGE_SIZE = 128
MASK_VALUE = -1e30
AUG_Q_C = HEAD_DIM
AUG_SPLITS = 3
MOE_ROWS = 512
ROUTE_GROUP_LANE = EXPERTS_PER_GROUP
ROUTE_RANK_LANE = EXPERTS_PER_GROUP + 1
V_PAD_ROWS = 16
V_ROWS = HEAD_DIM + V_PAD_ROWS


def _dot(a, b):
    return jnp.dot(a, b, preferred_element_type=F32)


def _dot_nt(a, b):
    return lax.dot_general(a, b, (((1,), (1,)), ((), ())), preferred_element_type=F32)


def _split3(x):
    h1 = x.astype(BF16)
    r1 = x - h1.astype(F32)
    h2 = r1.astype(BF16)
    r2 = r1 - h2.astype(F32)
    h3 = r2.astype(BF16)
    return h1, h2, h3


def _dot_exact_rhs(a_bf16, x):
    h1, h2, h3 = _split3(x)
    return _dot(a_bf16, h1) + _dot(a_bf16, h2) + _dot(a_bf16, h3)


def _dot_exact_lhs(x, b_bf16):
    h1, h2, h3 = _split3(x)
    return _dot(h1, b_bf16) + _dot(h2, b_bf16) + _dot(h3, b_bf16)


def _dot_nt_exact_rhs(a_bf16, x):
    h1, h2, h3 = _split3(x)
    return _dot_nt(a_bf16, h1) + _dot_nt(a_bf16, h2) + _dot_nt(a_bf16, h3)


def _log_sigmoid(x):
    return jnp.minimum(x, 0.0) - jnp.log1p(jnp.exp(-jnp.abs(x)))


def _layer_norm(r, g, b):
    mu = jnp.mean(r, axis=-1, keepdims=True)
    d = r - mu
    var = jnp.mean(d * d, axis=-1, keepdims=True)
    return d * lax.rsqrt(var + LN_EPS) * g + b


def _const_spec(shape):
    nd = len(shape)
    return pl.BlockSpec(shape, lambda *_: (0,) * nd, pipeline_mode=pl.Buffered(1))


def _inproj_kernel(*refs, rows, tiles_per_seq, sample, d_conv, d_model):
    if sample:
        (x_ref, wm_ref, wf_ref, wg_ref, bf_ref, cw_ref, wco_ref, s0_ref, s1_ref,
         q_ref, k_ref, v_ref, lf_ref, ap_ref, sgb_ref, u_ref, ubuf) = refs
    else:
        (x_ref, wm_ref, wf_ref, wg_ref, bf_ref, cw_ref, wco_ref, tri_ref, pq_ref, pk_ref,
         oq_ref, ok_ref,
         q_ref, k_ref, v_ref, ka_ref, vb_ref, lf_ref, ap_ref, sgb_ref, tail_ref, ubuf, ccarry) = refs
    i = pl.program_id(0)
    xb = x_ref[...].astype(BF16)

    xc = _dot(xb, wm_ref[:, 0:d_conv])
    bg = _dot(xb, wm_ref[:, d_conv:2 * d_conv])
    cg = _dot(xb, wm_ref[:, 2 * d_conv:3 * d_conv])
    u = cg * xc
    if sample:
        ubuf[0:SUBLANES, :] = jnp.zeros((SUBLANES, d_conv), F32)
    else:
        @pl.when(i % tiles_per_seq == 0)
        def _():
            ubuf[0:SUBLANES, :] = jnp.zeros((SUBLANES, d_conv), F32)
    ubuf[SUBLANES:SUBLANES + rows, :] = u
    prev1 = ubuf[SUBLANES - 1:SUBLANES - 1 + rows, :]
    prev2 = ubuf[SUBLANES - 2:SUBLANES - 2 + rows, :]
    if sample:
        pos = lax.broadcasted_iota(jnp.int32, (rows, 1), 0) % 4
        prev1 = jnp.where(pos == 0, s1_ref[...], prev1)
        prev2 = jnp.where(pos == 0, s0_ref[...], jnp.where(pos == 1, s1_ref[...], prev2))
        u_ref[...] = u
    else:
        ubuf[0:SUBLANES, :] = u[rows - SUBLANES:rows, :]

        @pl.when(i % tiles_per_seq == tiles_per_seq - 1)
        def _():
            tail_ref[0] = u[rows - SUBLANES:rows, :]
    conv = cw_ref[0:1, :] * prev2 + cw_ref[1:2, :] * prev1 + cw_ref[2:3, :] * u
    ya = _dot((bg * conv).astype(BF16), wco_ref[...])

    q = _dot(xb, wm_ref[:, 3 * d_conv:3 * d_conv + D_ATTN]) * (HEAD_DIM ** -0.5)
    k = _dot(xb, wm_ref[:, 3 * d_conv + D_ATTN:3 * d_conv + 2 * D_ATTN])
    v = _dot(xb, wm_ref[:, 3 * d_conv + 2 * D_ATTN:3 * d_conv + 3 * D_ATTN])
    fg = _dot(xb, wf_ref[...]) + bf_ref[...]
    lane = lax.broadcasted_iota(jnp.int32, (rows, LANES), 1)
    logf = jnp.where(lane < N_HEADS, _log_sigmoid(fg), 0.0)
    if sample:
        k_ref[...] = k
        v_ref[...] = v
        lf_ref[...] = logf[:, 0:N_HEADS]
        q_ref[...] = q.astype(BF16)
    else:
        k_ref[0] = k.T
        vt = v.T
        v_ref[0] = vt
        lf_ref[0] = logf.T[0:N_HEADS, :]
        ones_blk = jnp.where(lax.broadcasted_iota(jnp.int32, (V_PAD_ROWS, rows), 0) == 0, 1.0, 0.0)
        vb_ref[0] = jnp.concatenate(
            [blk for h in range(N_HEADS) for blk in (vt[h * HEAD_DIM:(h + 1) * HEAD_DIM, :], ones_blk)],
            axis=0).astype(BF16)

        @pl.when(i % tiles_per_seq == 0)
        def _():
            ccarry[...] = jnp.zeros((1, LANES), F32)
        c = _dot_exact_rhs(tri_ref[...], logf) + ccarry[...]
        ccarry[...] = c[rows - 1:rows, :]
        c1, c2, c3 = _split3(c)
        c_parts = (c1.astype(F32) + pltpu.roll(c2.astype(F32), N_HEADS, 1)
                   + pltpu.roll(c3.astype(F32), 2 * N_HEADS, 1)).astype(BF16)
        aug_q = _dot(c_parts, pq_ref[...]) + oq_ref[...]
        aug_k = ok_ref[...] - _dot(c_parts, pk_ref[...])

        def spread_heads(x, aug):
            groups = []
            for h in range(N_HEADS):
                slab = x[:, (h // 2) * LANES:(h // 2 + 1) * LANES]
                if h % 2:
                    slab = pltpu.roll(slab, HEAD_DIM, 1)
                groups.append(jnp.where(lane < HEAD_DIM, slab, aug[:, h * LANES:(h + 1) * LANES]))
            return jnp.concatenate(groups, axis=1).astype(BF16)

        q_ref[...] = spread_heads(q, aug_q)
        ka_ref[...] = spread_heads(k, aug_k)

    ga = _dot(xb, wg_ref[:, 0:d_model])
    gb = _dot(xb, wg_ref[:, d_model:2 * d_model])
    ap_ref[...] = jax.nn.sigmoid(ga) * ya
    sgb_ref[...] = jax.nn.sigmoid(gb)


def _inproj(x, w, *, sample, seq_len, state=None):
    t, d_model = x.shape
    d_conv = w['wco'].shape[0]
    rows = 256
    tiles_per_seq = max(seq_len // rows, 1)
    n_tiles = t // rows
    row_spec = lambda width: pl.BlockSpec((rows, width), lambda i: (i, 0))
    weights = [w['wm'], w['wf'], w['wg'], w['bf'], w['cw'], w['wco']]
    in_specs = [row_spec(d_model)] + [_const_spec(a.shape) for a in weights]
    args = [x] + weights
    if sample:
        args += [state[0], state[1]]
        in_specs += [row_spec(d_conv), row_spec(d_conv)]
        out_shape = [
            jax.ShapeDtypeStruct((t, D_ATTN), BF16),
            jax.ShapeDtypeStruct((t, D_ATTN), F32),
            jax.ShapeDtypeStruct((t, D_ATTN), F32),
            jax.ShapeDtypeStruct((t, N_HEADS), F32),
            jax.ShapeDtypeStruct((t, d_model), F32),
            jax.ShapeDtypeStruct((t, d_model), F32),
            jax.ShapeDtypeStruct((t, d_conv), F32),
        ]
        out_specs = [row_spec(D_ATTN), row_spec(D_ATTN), row_spec(D_ATTN), row_spec(N_HEADS),
                     row_spec(d_model), row_spec(d_model), row_spec(d_conv)]
        scratch = [pltpu.VMEM((rows + SUBLANES, d_conv), F32)]
    else:
        consts = [w['tri'], w['pq'], w['pk'], w['oq'], w['ok']]
        args += consts
        in_specs += [_const_spec(a.shape) for a in consts]
        n_seq = t // seq_len
        out_shape = [
            jax.ShapeDtypeStruct((t, N_HEADS * LANES), BF16),
            jax.ShapeDtypeStruct((n_seq, D_ATTN, seq_len), F32),
            jax.ShapeDtypeStruct((n_seq, D_ATTN, seq_len), F32),
            jax.ShapeDtypeStruct((t, N_HEADS * LANES), BF16),
            jax.ShapeDtypeStruct((n_tiles, N_HEADS * V_ROWS, rows), BF16),
            jax.ShapeDtypeStruct((n_seq, N_HEADS, seq_len), F32),
            jax.ShapeDtypeStruct((t, d_model), F32),
            jax.ShapeDtypeStruct((t, d_model), F32),
            jax.ShapeDtypeStruct((n_seq, SUBLANES, d_conv), F32),
        ]
        pos_minor = lambda height: pl.BlockSpec((1, height, rows),
                                                lambda i: (i // tiles_per_seq, 0, i % tiles_per_seq))
        out_specs = [row_spec(N_HEADS * LANES), pos_minor(D_ATTN), pos_minor(D_ATTN), row_spec(N_HEADS * LANES),
                     pl.BlockSpec((1, N_HEADS * V_ROWS, rows), lambda i: (i, 0, 0)), pos_minor(N_HEADS),
                     row_spec(d_model), row_spec(d_model),
                     pl.BlockSpec((1, SUBLANES, d_conv), lambda i: (i // tiles_per_seq, 0, 0))]
        scratch = [pltpu.VMEM((rows + SUBLANES, d_conv), F32), pltpu.VMEM((1, LANES), F32)]
    kern = functools.partial(_inproj_kernel, rows=rows, tiles_per_seq=tiles_per_seq, sample=sample,
                             d_conv=d_conv, d_model=d_model)
    return pl.pallas_call(
        kern,
        grid=(n_tiles,),
        in_specs=in_specs,
        out_specs=out_specs,
        out_shape=out_shape,
        scratch_shapes=scratch,
        compiler_params=pltpu.CompilerParams(dimension_semantics=("arbitrary",),
                                             vmem_limit_bytes=VMEM_LIMIT_BYTES),
        name="inproj_sample" if sample else "inproj_prompt",
    )(*args)


def _prompt_tile(i, q_ref, k_ref, vt_ref, o_ref, m_sc, acc_sc, *, tq, v_chunk, q_chunk):
    chunks = tq // v_chunk
    for hl in range(2):
        m_sc[hl] = jnp.full((1, tq), MASK_VALUE, F32)
        acc_sc[hl] = jnp.zeros((V_ROWS, tq), F32)

    def step(j, masked):
        start = pl.multiple_of(j * tq, tq)
        blocks = [(hl, qc) for hl in range(2) for qc in range(tq // q_chunk)]
        m_old = [m_sc[hl] for hl in range(2)]
        acc_old = [acc_sc[hl] for hl in range(2)]
        sts = [_dot_nt(k_ref[pl.ds(start, tq), hl * LANES:(hl + 1) * LANES],
                       q_ref[qc * q_chunk:(qc + 1) * q_chunk, hl * LANES:(hl + 1) * LANES]) for hl, qc in blocks]
        m_out, acc_out = [[], []], [[], []]
        for (hl, qc), st in zip(blocks, sts):
            qs = slice(qc * q_chunk, (qc + 1) * q_chunk)
            if masked:
                key = lax.broadcasted_iota(jnp.int32, (tq, q_chunk), 0)
                qry = lax.broadcasted_iota(jnp.int32, (tq, q_chunk), 1) + qc * q_chunk
                st = jnp.where(key <= qry, st, MASK_VALUE)
            m_prev = m_old[hl][:, qs]
            m_new = jnp.maximum(m_prev, jnp.max(st, axis=0, keepdims=True))
            alpha = jnp.exp(m_prev - m_new)
            pb = jnp.exp(st - m_new).astype(BF16)
            acc = alpha * acc_old[hl][:, qs]
            for c in range(chunks):
                vt = vt_ref[j * chunks + c, hl * V_ROWS:(hl + 1) * V_ROWS, :]
                acc = acc + _dot(vt, pb[c * v_chunk:(c + 1) * v_chunk, :])
            acc_out[hl].append(acc)
            m_out[hl].append(m_new)
        for hl in range(2):
            m_sc[hl] = jnp.concatenate(m_out[hl], axis=1)
            acc_sc[hl] = jnp.concatenate(acc_out[hl], axis=1)

    def body(j, carry):
        step(j, False)
        return carry

    lax.fori_loop(0, i, body, 0)
    step(i, True)
    ot = jnp.concatenate([acc_sc[hl, 0:HEAD_DIM, :] / acc_sc[hl, HEAD_DIM:HEAD_DIM + 1, :] for hl in range(2)],
                         axis=0)
    o_ref[...] = ot.T.astype(BF16)


def _sample_sequence(slot, q_ref, kn_ref, vn_ref, fn_ref, e_ref, suffix_ref, later_ref, tri_ref, o_ref,
                     k_buf, v_buf, f_buf, pad_sc, kn_sc, vn_sc, *, n_pages, n_new):
    n_rows = N_HEADS * n_new
    qb = q_ref[0]
    e = e_ref[...]

    pad_sc[...] = jnp.zeros((PAGE_SIZE, LANES), F32)
    pad_sc[0:n_new, 0:N_HEADS] = fn_ref[0]
    kn_sc[...] = jnp.zeros((PAGE_SIZE, D_ATTN), F32)
    kn_sc[0:n_new, :] = kn_ref[0]
    vn_sc[...] = jnp.zeros((PAGE_SIZE, D_ATTN), F32)
    vn_sc[0:n_new, :] = vn_ref[0]
    cn = _dot_exact_rhs(tri_ref[...], pad_sc[...])
    cn_rows = _dot_nt_exact_rhs(e, cn)
    row_t = lax.broadcasted_iota(jnp.int32, (n_rows, LANES), 0) // N_HEADS
    col = lax.broadcasted_iota(jnp.int32, (n_rows, LANES), 1)
    cnq = jnp.sum(jnp.where(col == row_t, cn_rows, 0.0), axis=1, keepdims=True)
    s_new = _dot_nt(qb, kn_sc[...].astype(BF16)) + (cnq - cn_rows)
    s_new = jnp.where(col <= row_t, s_new, MASK_VALUE)

    lf = f_buf[slot].reshape(n_pages * N_HEADS, PAGE_SIZE)
    incl = _dot_exact_lhs(lf, suffix_ref[...])
    tot = jnp.broadcast_to(incl[:, 0:1], incl.shape)
    d = incl - lf + _dot_exact_rhs(later_ref[...], tot)
    s_pages = []
    for j in range(n_pages):
        bias = jnp.concatenate([d[j * N_HEADS:(j + 1) * N_HEADS, :]] * n_new, axis=0)
        s_pages.append(_dot(qb, k_buf[slot, j].astype(BF16)) + bias + cnq)

    m_blk = s_new
    for s in s_pages:
        m_blk = jnp.maximum(m_blk, s)
    m = jnp.max(m_blk, axis=1, keepdims=True)
    p_new = jnp.exp(s_new - m)
    l_blk = p_new
    acc = _dot(p_new.astype(BF16), vn_sc[...].astype(BF16))
    for j in range(n_pages):
        p = jnp.exp(s_pages[j] - m)
        l_blk = l_blk + p
        acc = acc + _dot_nt(p.astype(BF16), v_buf[slot, j].astype(BF16))
    o_ref[0] = acc / jnp.sum(l_blk, axis=1, keepdims=True)


def _attention_kernel(pt_ref, q_ref, k_ref, vt_ref, qs_ref, kn_ref, vn_ref, fn_ref, e_ref, suffix_ref, later_ref,
                      tri_ref, kt_hbm, vtc_hbm, lft_hbm, o_ref, os_ref, m_sc, acc_sc, k_buf, v_buf, f_buf, sems,
                      pad_sc, kn_sc, vn_sc, *, tq, v_chunk, q_chunk, n_pages, n_new):
    step = (pl.program_id(0) * pl.num_programs(1) + pl.program_id(1)) * pl.num_programs(2) + pl.program_id(2)
    n_steps = pl.num_programs(0) * pl.num_programs(1) * pl.num_programs(2)
    slot = step % 2
    pools = ((kt_hbm, k_buf), (vtc_hbm, v_buf), (lft_hbm, f_buf))

    def start_pages(seq, to_slot):
        for j in range(n_pages):
            page = pt_ref[seq * n_pages + j]
            for a, (hbm, buf) in enumerate(pools):
                pltpu.make_async_copy(hbm.at[page], buf.at[to_slot, j], sems.at[a, to_slot]).start()

    @pl.when(step == 0)
    def _():
        start_pages(0, 0)

    @pl.when(step + 1 < n_steps)
    def _():
        start_pages(step + 1, 1 - slot)

    _prompt_tile(pl.program_id(2), q_ref, k_ref, vt_ref, o_ref, m_sc, acc_sc, tq=tq, v_chunk=v_chunk,
                 q_chunk=q_chunk)

    for a, (hbm, buf) in enumerate(pools):
        pltpu.make_async_copy(hbm.at[pl.ds(0, n_pages)], buf.at[slot], sems.at[a, slot]).wait()
    _sample_sequence(slot, qs_ref, kn_ref, vn_ref, fn_ref, e_ref, suffix_ref, later_ref, tri_ref, os_ref,
                     k_buf, v_buf, f_buf, pad_sc, kn_sc, vn_sc, n_pages=n_pages, n_new=n_new)


def _attention(qa, ka, vtb, qblk, cache_kt, cache_vt, cache_lft, page_table, k_new, v_new, lf_new,
               *, n_prompt_seq, seq_len):
    t = qa.shape[0]
    v_chunk = vtb.shape[2]
    tq = 512
    qt = seq_len // tq
    n_pairs = N_HEADS // 2
    n_seq, n_rows, _ = qblk.shape
    assert n_seq == n_prompt_seq * n_pairs * qt, "one sample sequence per prompt attention tile"
    n_new = k_new.shape[1]
    n_pages = page_table.shape[1]
    pt = page_table.reshape(-1)
    row_head = jnp.arange(n_rows) % N_HEADS
    e = (row_head[:, None] == jnp.arange(LANES)[None, :]).astype(BF16)
    pos = jnp.arange(PAGE_SIZE)
    suffix = (pos[:, None] >= pos[None, :]).astype(BF16)
    tri = (pos[None, :] <= pos[:, None]).astype(BF16)
    r = jnp.arange(n_pages * N_HEADS)
    later = ((r[:, None] % N_HEADS == r[None, :] % N_HEADS)
             & (r[None, :] // N_HEADS > r[:, None] // N_HEADS)).astype(BF16)

    tile_q = lambda n, hp, i, pt_ref: (n * qt + i, hp)
    seq_map = lambda n, hp, i, pt_ref: ((n * n_pairs + hp) * qt + i, 0, 0)
    const_map = lambda n, hp, i, pt_ref: (0, 0)
    any_spec = pl.BlockSpec(memory_space=pl.ANY)
    in_specs = [pl.BlockSpec((tq, 2 * LANES), tile_q),
                pl.BlockSpec((seq_len, 2 * LANES), lambda n, hp, i, pt_ref: (n, hp)),
                pl.BlockSpec((seq_len // v_chunk, 2 * V_ROWS, v_chunk), lambda n, hp, i, pt_ref: (n, hp, 0)),
                pl.BlockSpec((1, n_rows, D_ATTN), seq_map),
                pl.BlockSpec((1, n_new, D_ATTN), seq_map), pl.BlockSpec((1, n_new, D_ATTN), seq_map),
                pl.BlockSpec((1, n_new, N_HEADS), seq_map),
                pl.BlockSpec(e.shape, const_map), pl.BlockSpec(suffix.shape, const_map),
                pl.BlockSpec(later.shape, const_map), pl.BlockSpec(tri.shape, const_map),
                any_spec, any_spec, any_spec]
    grid_spec = pltpu.PrefetchScalarGridSpec(
        num_scalar_prefetch=1,
        grid=(n_prompt_seq, n_pairs, qt),
        in_specs=in_specs,
        out_specs=[pl.BlockSpec((tq, LANES), tile_q), pl.BlockSpec((1, n_rows, D_ATTN), seq_map)],
        scratch_shapes=[pltpu.VMEM((2, 1, tq), F32), pltpu.VMEM((2, V_ROWS, tq), F32),
                        pltpu.VMEM((2, n_pages, D_ATTN, PAGE_SIZE), F32),
                        pltpu.VMEM((2, n_pages, D_ATTN, PAGE_SIZE), F32),
                        pltpu.VMEM((2, n_pages, N_HEADS, PAGE_SIZE), F32),
                        pltpu.SemaphoreType.DMA((3, 2)),
                        pltpu.VMEM((PAGE_SIZE, LANES), F32),
                        pltpu.VMEM((PAGE_SIZE, D_ATTN), F32), pltpu.VMEM((PAGE_SIZE, D_ATTN), F32)],
    )
    args = [pt, qa, ka, vtb, qblk, k_new, v_new, lf_new, e, suffix, later, tri, cache_kt, cache_vt, cache_lft]
    return pl.pallas_call(
        functools.partial(_attention_kernel, tq=tq, v_chunk=v_chunk, q_chunk=LANES, n_pages=n_pages, n_new=n_new),
        grid_spec=grid_spec,
        out_shape=[jax.ShapeDtypeStruct((t, D_ATTN), BF16), jax.ShapeDtypeStruct((n_seq, n_rows, D_ATTN), F32)],
        compiler_params=pltpu.CompilerParams(dimension_semantics=("arbitrary", "arbitrary", "arbitrary"),
                                             vmem_limit_bytes=VMEM_LIMIT_BYTES),
        name="attention",
    )(*args)


def _route(logits):
    rows = logits.shape[0]
    lane = lax.broadcasted_iota(jnp.int32, (rows, LANES), 1).astype(F32)
    big = float(LANES)
    in_g = lane < N_GROUPS
    gl = jnp.where(in_g, logits, MASK_VALUE)
    m_g = jnp.max(gl, axis=1, keepdims=True)
    z_g = jnp.sum(jnp.exp(gl - m_g), axis=1, keepdims=True)
    p_g = 1.0 / z_g
    g_idx = jnp.min(jnp.where(in_g & (gl == m_g), lane, big), axis=1, keepdims=True)
    lo = N_GROUPS + EXPERTS_PER_GROUP * g_idx
    sel = (lane >= lo) & (lane < lo + EXPERTS_PER_GROUP)
    el = jnp.where(sel, logits, MASK_VALUE)
    m_e = jnp.max(el, axis=1, keepdims=True)
    ex = jnp.exp(el - m_e)
    ep = ex / jnp.sum(ex, axis=1, keepdims=True)
    v1 = jnp.max(jnp.where(sel, ep, -1.0), axis=1, keepdims=True)
    i1 = jnp.min(jnp.where(sel & (ep == v1), lane, big), axis=1, keepdims=True)
    sel2 = sel & (lane != i1)
    v2 = jnp.max(jnp.where(sel2, ep, -1.0), axis=1, keepdims=True)
    i2 = jnp.min(jnp.where(sel2 & (ep == v2), lane, big), axis=1, keepdims=True)
    denom = v1 + v2
    w1 = p_g * (v1 / denom)
    w2 = p_g * (v2 / denom)
    gate4 = jnp.where(lane == i1 - lo, w1, 0.0) + jnp.where(lane == i2 - lo, w2, 0.0)
    return gate4, g_idx


def _post_kernel(o_ref, ap_ref, sgb_ref, x_ref, wao_ref, wo_ref, g_ref, b_ref, wrh_ref, wrl_ref, tri_ref,
                 x1e_ref, route_ref, totals_ref, count_sc, *, alpha, d_model):
    i = pl.program_id(0)
    rows = x_ref.shape[0]
    yb = _dot(o_ref[...], wao_ref[...])
    mix = _dot((ap_ref[...] + sgb_ref[...] * yb).astype(BF16), wo_ref[...])
    x1 = _layer_norm(alpha * x_ref[...] + mix, g_ref[...], b_ref[...])
    xh = x1.astype(BF16)
    xl = (x1 - xh.astype(F32)).astype(BF16)
    logits = _dot(xh, wrh_ref[...]) + _dot(xl, wrh_ref[...]) + _dot(xh, wrl_ref[...])
    gate4, g_idx = _route(logits)

    @pl.when(i == 0)
    def _():
        count_sc[...] = jnp.zeros((1, LANES), F32)
    lane = lax.broadcasted_iota(jnp.int32, (rows, LANES), 1).astype(F32)
    onehot = jnp.where(lane == g_idx, 1.0, 0.0)
    count = _dot(tri_ref[...], onehot.astype(BF16)) + count_sc[...]
    rank = jnp.sum(onehot * count, axis=1, keepdims=True) - 1.0
    count_sc[...] = count[rows - 1:rows, :]
    totals_ref[...] = jnp.broadcast_to(count[rows - 1:rows, :], (SUBLANES, LANES))
    route = (gate4 + jnp.where(lane == ROUTE_GROUP_LANE, g_idx, 0.0)
             + jnp.where(lane == ROUTE_RANK_LANE, rank, 0.0))
    route_ref[...] = route
    x1e_ref[:, 0:d_model] = x1
    x1e_ref[:, d_model:d_model + LANES] = route


def _post_attention(o, ap, sgb, x, w, *, alpha):
    t, d_model = x.shape
    rows = MOE_ROWS
    row_spec = lambda width: pl.BlockSpec((rows, width), lambda i: (i, 0))
    weights = [w['wao'], w['wo'], w['ln1_g'], w['ln1_b'], w['wr_hi'], w['wr_lo'], w['tri_moe']]
    return pl.pallas_call(
        functools.partial(_post_kernel, alpha=alpha, d_model=d_model),
        grid=(t // rows,),
        in_specs=[row_spec(D_ATTN), row_spec(d_model), row_spec(d_model), row_spec(d_model)]
                 + [_const_spec(a.shape) for a in weights],
        out_specs=[row_spec(d_model + LANES), row_spec(LANES),
                   pl.BlockSpec((SUBLANES, LANES), lambda i: (0, 0))],
        out_shape=[jax.ShapeDtypeStruct((t, d_model + LANES), F32),
                   jax.ShapeDtypeStruct((t, LANES), F32),
                   jax.ShapeDtypeStruct((SUBLANES, LANES), F32)],
        scratch_shapes=[pltpu.VMEM((1, LANES), F32)],
        compiler_params=pltpu.CompilerParams(dimension_semantics=("arbitrary",),
                                             vmem_limit_bytes=VMEM_LIMIT_BYTES),
        name="post_attn",
    )(o, ap, sgb, x, *weights)


def _row_permute_kernel(pos_ref, src_ref, *rest, rows, scatter):
    if scatter:
        _, dst_ref, sem = rest
    else:
        dst_ref, sem = rest
    base = pl.program_id(0) * rows

    def issue(r, carry):
        p = pos_ref[base + r]
        if scatter:
            pltpu.make_async_copy(src_ref.at[pl.ds(r, 1)], dst_ref.at[pl.ds(p, 1)], sem).start()
        else:
            pltpu.make_async_copy(src_ref.at[pl.ds(p, 1)], dst_ref.at[pl.ds(r, 1)], sem).start()
        return carry

    lax.fori_loop(0, rows, issue, 0, unroll=True)
    if scatter:
        pltpu.make_async_copy(src_ref, dst_ref.at[pl.ds(0, rows)], sem).wait()
    else:
        pltpu.make_async_copy(src_ref.at[pl.ds(0, rows)], dst_ref, sem).wait()


def _row_permute(pos, src, *, n_out, scatter):
    n_rows = pos.shape[0]
    width = src.shape[1]
    rows = MOE_ROWS
    any_spec = pl.BlockSpec(memory_space=pl.ANY)
    tile_spec = pl.BlockSpec((rows, width), lambda i, pos_ref: (i, 0))
    args = [pos, src]
    aliases = {}
    if scatter:
        args.append(jnp.zeros((n_out, width), src.dtype))
        in_specs, out_specs = [tile_spec, any_spec], any_spec
        aliases = {2: 0}
    else:
        in_specs, out_specs = [any_spec], tile_spec
    grid_spec = pltpu.PrefetchScalarGridSpec(
        num_scalar_prefetch=1, grid=(n_rows // rows,), in_specs=in_specs, out_specs=out_specs,
        scratch_shapes=[pltpu.SemaphoreType.DMA(())])
    return pl.pallas_call(
        functools.partial(_row_permute_kernel, rows=rows, scatter=scatter),
        grid_spec=grid_spec,
        out_shape=jax.ShapeDtypeStruct((n_out, width), src.dtype),
        input_output_aliases=aliases,
        compiler_params=pltpu.CompilerParams(dimension_semantics=("arbitrary",)),
        name="row_scatter" if scatter else "row_gather",
    )(*args)


def _moe_kernel(tg_ref, nv_ref, xs_ref, w1_ref, w3_ref, w2_ref, g_ref, b_ref, y_ref, *, alpha, d_model):
    del tg_ref
    i = pl.program_id(0)

    @pl.when(i < nv_ref[0])
    def _():
        x = xs_ref[:, 0:d_model]
        route = xs_ref[:, d_model:d_model + LANES]
        xb = x.astype(BF16)
        lane = lax.broadcasted_iota(jnp.int32, route.shape, 1)
        acc = jnp.zeros_like(x)
        for e in range(EXPERTS_PER_GROUP):
            h1 = _dot(xb, w1_ref[0, e])
            h3 = _dot(xb, w3_ref[0, e])
            h = (h1 * jax.nn.sigmoid(h1)) * h3
            ye = _dot(h.astype(BF16), w2_ref[0, e])
            ge = jnp.sum(jnp.where(lane == e, route, 0.0), axis=1, keepdims=True)
            acc = acc + ge * ye
        y_ref[...] = _layer_norm(alpha * x + acc, g_ref[...], b_ref[...])

    @pl.when(i >= nv_ref[0])
    def _():
        y_ref[...] = jnp.zeros_like(y_ref)


def _experts(x1e, route, totals, w, *, alpha):
    t = x1e.shape[0]
    d_model = x1e.shape[1] - LANES
    _, _, _, d_exp = w['w1'].shape
    rows = MOE_ROWS
    n_tiles = t // rows + N_GROUPS
    group = route[:, ROUTE_GROUP_LANE].astype(jnp.int32)
    rank = route[:, ROUTE_RANK_LANE].astype(jnp.int32)
    count = totals[0, 0:N_GROUPS].astype(jnp.int32)
    tiles_g = (count + rows - 1) // rows
    tile_end = jnp.cumsum(tiles_g)
    pos = (tile_end - tiles_g)[group] * rows + rank
    n_valid = tile_end[N_GROUPS - 1:N_GROUPS]
    tile_group = jnp.minimum(jnp.sum(jnp.arange(n_tiles)[:, None] >= tile_end[None, :], axis=1),
                             N_GROUPS - 1).astype(jnp.int32)

    xs = _row_permute(pos, x1e, n_out=n_tiles * rows, scatter=True)

    live = lambda i, nv: jnp.minimum(i, nv[0] - 1)
    w_spec = lambda a: pl.BlockSpec((1,) + a.shape[1:], lambda i, tg, nv: (tg[live(i, nv)], 0, 0, 0))
    grid_spec = pltpu.PrefetchScalarGridSpec(
        num_scalar_prefetch=2,
        grid=(n_tiles,),
        in_specs=[
            pl.BlockSpec((rows, d_model + LANES), lambda i, tg, nv: (live(i, nv), 0)),
            w_spec(w['w1']), w_spec(w['w3']), w_spec(w['w2']),
            pl.BlockSpec((1, d_model), lambda i, tg, nv: (0, 0)),
            pl.BlockSpec((1, d_model), lambda i, tg, nv: (0, 0)),
        ],
        out_specs=pl.BlockSpec((rows, d_model), lambda i, tg, nv: (i, 0)),
    )
    ys = pl.pallas_call(
        functools.partial(_moe_kernel, alpha=alpha, d_model=d_model),
        grid_spec=grid_spec,
        out_shape=jax.ShapeDtypeStruct((n_tiles * rows, d_model), F32),
        compiler_params=pltpu.CompilerParams(dimension_semantics=("arbitrary",),
                                             vmem_limit_bytes=VMEM_LIMIT_BYTES),
        name="experts",
    )(tile_group, n_valid, xs, w['w1'], w['w3'], w['w2'], w['ln2_g'], w['ln2_b'])
    return _row_permute(pos, ys, n_out=t, scatter=False)


def _prepare_weights(w_in, b_f, conv_w, w_conv_out, w_attn_out, w_o, ln1_g, ln1_b,
                     w_group, w_router, w1, w3, w2, ln2_g, ln2_b, rows_inproj):
    d_model = w_in.shape[0]
    d_conv = w_conv_out.shape[0]
    o_q = 3 * d_conv
    o_k = o_q + D_ATTN
    o_v = o_k + D_ATTN
    o_f = o_v + D_ATTN
    o_g = o_f + N_HEADS
    w_bf = w_in.astype(BF16)
    wm = w_bf[:, 0:o_f]
    wf = jnp.pad(w_bf[:, o_f:o_g], ((0, 0), (0, LANES - N_HEADS)))
    wg = w_bf[:, o_g:]
    bf = jnp.pad(b_f.astype(F32), (0, LANES - N_HEADS)).reshape(1, LANES)

    col = jnp.arange(N_HEADS * LANES)
    col_head, col_lane = col // LANES, col % LANES
    part = jnp.arange(LANES)
    part_k, part_h = part // N_HEADS, part % N_HEADS
    live = (part_k[:, None] < AUG_SPLITS) & (part_h[:, None] == col_head[None, :])
    pq = live & (col_lane[None, :] == AUG_Q_C + part_k[:, None])
    pk = live & (col_lane[None, :] == AUG_Q_C + AUG_SPLITS + part_k[:, None])
    oq = ((col_lane >= AUG_Q_C + AUG_SPLITS) & (col_lane < AUG_Q_C + 2 * AUG_SPLITS)).astype(F32).reshape(1, -1)
    ok = ((col_lane >= AUG_Q_C) & (col_lane < AUG_Q_C + AUG_SPLITS)).astype(F32).reshape(1, -1)
    r = jnp.arange(rows_inproj)
    tri = (r[None, :] <= r[:, None]).astype(BF16)
    r = jnp.arange(MOE_ROWS)
    tri_moe = (r[None, :] <= r[:, None]).astype(BF16)

    wr = jnp.pad(jnp.concatenate([w_group, w_router], axis=1), ((0, 0), (0, LANES - N_GROUPS - N_EXPERTS)))
    wr_hi = wr.astype(BF16)
    wr_lo = (wr - wr_hi.astype(F32)).astype(BF16)
    return dict(
        wm=wm, wf=wf, wg=wg, bf=bf, cw=conv_w.astype(F32), wco=w_conv_out.astype(BF16),
        tri=tri, pq=pq.astype(BF16), pk=pk.astype(BF16), oq=oq, ok=ok,
        wao=w_attn_out.astype(BF16), wo=w_o.astype(BF16),
        ln1_g=ln1_g.reshape(1, -1), ln1_b=ln1_b.reshape(1, -1), wr_hi=wr_hi, wr_lo=wr_lo,
        tri_moe=tri_moe,
        w1=w1.astype(BF16).reshape((N_GROUPS, EXPERTS_PER_GROUP) + w1.shape[1:]),
        w3=w3.astype(BF16).reshape((N_GROUPS, EXPERTS_PER_GROUP) + w3.shape[1:]),
        w2=w2.astype(BF16).reshape((N_GROUPS, EXPERTS_PER_GROUP) + w2.shape[1:]),
        ln2_g=ln2_g.reshape(1, -1), ln2_b=ln2_b.reshape(1, -1),
    )


def kernel(x_prompt, x_sample, cache_k, cache_v, cache_logf, state_conv, page_table, w_in, b_f, conv_w,
           w_conv_out, w_attn_out, w_o, ln1_g, ln1_b, w_group, w_router, w1, w3, w2, ln2_g, ln2_b):
    depth = w_in.shape[0]
    assert depth == 1, "single-layer stack"
    n_p, s_p, d_model = x_prompt.shape
    n_s, s_s, _ = x_sample.shape
    d_conv = w_conv_out.shape[1]
    alpha = (2 * depth) ** 0.25
    w = _prepare_weights(w_in[0], b_f[0], conv_w[0], w_conv_out[0], w_attn_out[0], w_o[0], ln1_g[0], ln1_b[0],
                         w_group[0], w_router[0], w1[0], w3[0], w2[0], ln2_g[0], ln2_b[0], rows_inproj=256)

    xp = x_prompt.reshape(n_p * s_p, d_model)
    qa, kt_p, vt_p, ka, vtb, lft_p, ap, sgb, tail = _inproj(xp, w, sample=False, seq_len=s_p)
    xs = x_sample.reshape(n_s * s_s, d_model)
    state = state_conv[0]
    s0 = jnp.repeat(state[:, 0, :], s_s, axis=0)
    s1 = jnp.repeat(state[:, 1, :], s_s, axis=0)
    q_s, k_s, v_s, lf_s, ap_s, sgb_s, u_s = _inproj(xs, w, sample=True, seq_len=s_s, state=(s0, s1))
    q5 = q_s.reshape(n_s, s_s, N_HEADS, HEAD_DIM)
    eye_h = jnp.eye(N_HEADS, dtype=BF16)
    qblk = (q5[:, :, :, None, :] * eye_h[None, None, :, :, None]).reshape(n_s, s_s * N_HEADS, D_ATTN)
    n_pool = cache_k.shape[1]
    cache_kt = cache_k[0].transpose(0, 2, 3, 1).reshape(n_pool, D_ATTN, PAGE_SIZE)
    cache_vt = cache_v[0].transpose(0, 2, 3, 1).reshape(n_pool, D_ATTN, PAGE_SIZE)
    cache_lft = cache_logf[0].transpose(0, 2, 1)
    o_p, o_blk = _attention(
        qa, ka, vtb, qblk, cache_kt, cache_vt, cache_lft, page_table, k_s.reshape(n_s, s_s, D_ATTN),
        v_s.reshape(n_s, s_s, D_ATTN), lf_s.reshape(n_s, s_s, N_HEADS), n_prompt_seq=n_p, seq_len=s_p)
    x1e_p, route_p, totals_p = _post_attention(o_p, ap, sgb, xp, w, alpha=alpha)
    y_p = _experts(x1e_p, route_p, totals_p, w, alpha=alpha)
    o5 = o_blk.reshape(n_s, s_s, N_HEADS, N_HEADS, HEAD_DIM)
    o_s = jnp.einsum('bthgd,hg->bthd', o5, jnp.eye(N_HEADS, dtype=F32)).reshape(n_s * s_s, D_ATTN).astype(BF16)
    x1e_s, route_s, totals_s = _post_attention(o_s, ap_s, sgb_s, xs, w, alpha=alpha)
    y_s = _experts(x1e_s, route_s, totals_s, w, alpha=alpha)

    return (
        y_p.reshape(n_p, s_p, d_model),
        y_s.reshape(n_s, s_s, d_model),
        kt_p.reshape(1, n_p, N_HEADS, HEAD_DIM, s_p).transpose(0, 1, 4, 2, 3),
        vt_p.reshape(1, n_p, N_HEADS, HEAD_DIM, s_p).transpose(0, 1, 4, 2, 3),
        lft_p.reshape(1, n_p, N_HEADS, s_p).transpose(0, 1, 3, 2),
        tail[:, SUBLANES - (CONV_WIDTH - 1):, :].reshape(1, n_p, CONV_WIDTH - 1, d_conv),
        k_s.reshape(1, n_s, s_s, N_HEADS, HEAD_DIM),
        v_s.reshape(1, n_s, s_s, N_HEADS, HEAD_DIM),
        lf_s.reshape(1, n_s, s_s, N_HEADS),
        u_s.reshape(n_s, s_s, d_conv)[:, s_s - (CONV_WIDTH - 1):, :].reshape(1, n_s, CONV_WIDTH - 1, d_conv),
    )
```

```python
import functools

import jax
import jax.numpy as jnp
from jax import lax
from jax.experimental import pallas as pl
from jax.experimental.pallas import tpu as pltpu

F32 = jnp.float32
BF16 = jnp.bfloat16

LANES = 128
SUBLANES = 8
VMEM_LIMIT_BYTES = 56 * 1024 * 1024

CONV_WIDTH = 3
N_HEADS = 8
HEAD_DIM = 64
D_ATTN = N_HEADS * HEAD_DIM
N_GROUPS = 4
EXPERTS_PER_GROUP = 4
N_EXPERTS = N_GROUPS * EXPERTS_PER_GROUP
LN_EPS = 1e-5
PAGE_SIZE = 128
MASK_VALUE = -1e30
AUG_Q_C = HEAD_DIM
AUG_SPLITS = 3
MOE_ROWS = 512
ROUTE_GROUP_LANE = EXPERTS_PER_GROUP
ROUTE_RANK_LANE = EXPERTS_PER_GROUP + 1
V_PAD_ROWS = 16
V_ROWS = HEAD_DIM + V_PAD_ROWS


def _dot(a, b):
    return jnp.dot(a, b, preferred_element_type=F32)


def _dot_nt(a, b):
    return lax.dot_general(a, b, (((1,), (1,)), ((), ())), preferred_element_type=F32)


def _split3(x):
    h1 = x.astype(BF16)
    r1 = x - h1.astype(F32)
    h2 = r1.astype(BF16)
    r2 = r1 - h2.astype(F32)
    h3 = r2.astype(BF16)
    return h1, h2, h3


def _dot_exact_rhs(a_bf16, x):
    h1, h2, h3 = _split3(x)
    return _dot(a_bf16, h1) + _dot(a_bf16, h2) + _dot(a_bf16, h3)


def _dot_exact_lhs(x, b_bf16):
    h1, h2, h3 = _split3(x)
    return _dot(h1, b_bf16) + _dot(h2, b_bf16) + _dot(h3, b_bf16)


def _dot_nt_exact_rhs(a_bf16, x):
    h1, h2, h3 = _split3(x)
    return _dot_nt(a_bf16, h1) + _dot_nt(a_bf16, h2) + _dot_nt(a_bf16, h3)


def _log_sigmoid(x):
    return jnp.minimum(x, 0.0) - jnp.log1p(jnp.exp(-jnp.abs(x)))


def _layer_norm(r, g, b):
    mu = jnp.mean(r, axis=-1, keepdims=True)
    d = r - mu
    var = jnp.mean(d * d, axis=-1, keepdims=True)
    return d * lax.rsqrt(var + LN_EPS) * g + b


def _const_spec(shape):
    nd = len(shape)
    return pl.BlockSpec(shape, lambda *_: (0,) * nd, pipeline_mode=pl.Buffered(1))


def _inproj_kernel(*refs, rows, tiles_per_seq, sample, d_conv, d_model):
    if sample:
        (x_ref, wm_ref, wf_ref, wg_ref, bf_ref, cw_ref, wco_ref, s0_ref, s1_ref,
         q_ref, k_ref, v_ref, lf_ref, ap_ref, sgb_ref, u_ref, ubuf) = refs
    else:
        (x_ref, wm_ref, wf_ref, wg_ref, bf_ref, cw_ref, wco_ref, tri_ref, pq_ref, pk_ref,
         oq_ref, ok_ref,
         q_ref, k_ref, v_ref, ka_ref, vb_ref, lf_ref, ap_ref, sgb_ref, tail_ref, ubuf, ccarry) = refs
    i = pl.program_id(0)
    xb = x_ref[...].astype(BF16)

    xc = _dot(xb, wm_ref[:, 0:d_conv])
    bg = _dot(xb, wm_ref[:, d_conv:2 * d_conv])
    cg = _dot(xb, wm_ref[:, 2 * d_conv:3 * d_conv])
    u = cg * xc
    if sample:
        ubuf[0:SUBLANES, :] = jnp.zeros((SUBLANES, d_conv), F32)
    else:
        @pl.when(i % tiles_per_seq == 0)
        def _():
            ubuf[0:SUBLANES, :] = jnp.zeros((SUBLANES, d_conv), F32)
    ubuf[SUBLANES:SUBLANES + rows, :] = u
    prev1 = ubuf[SUBLANES - 1:SUBLANES - 1 + rows, :]
    prev2 = ubuf[SUBLANES - 2:SUBLANES - 2 + rows, :]
    if sample:
        pos = lax.broadcasted_iota(jnp.int32, (rows, 1), 0) % 4
        prev1 = jnp.where(pos == 0, s1_ref[...], prev1)
        prev2 = jnp.where(pos == 0, s0_ref[...], jnp.where(pos == 1, s1_ref[...], prev2))
        u_ref[...] = u
    else:
        ubuf[0:SUBLANES, :] = u[rows - SUBLANES:rows, :]

        @pl.when(i % tiles_per_seq == tiles_per_seq - 1)
        def _():
            tail_ref[0] = u[rows - SUBLANES:rows, :]
    conv = cw_ref[0:1, :] * prev2 + cw_ref[1:2, :] * prev1 + cw_ref[2:3, :] * u
    ya = _dot((bg * conv).astype(BF16), wco_ref[...])

    q = _dot(xb, wm_ref[:, 3 * d_conv:3 * d_conv + D_ATTN]) * (HEAD_DIM ** -0.5)
    k = _dot(xb, wm_ref[:, 3 * d_conv + D_ATTN:3 * d_conv + 2 * D_ATTN])
    v = _dot(xb, wm_ref[:, 3 * d_conv + 2 * D_ATTN:3 * d_conv + 3 * D_ATTN])
    fg = _dot(xb, wf_ref[...]) + bf_ref[...]
    lane = lax.broadcasted_iota(jnp.int32, (rows, LANES), 1)
    logf = jnp.where(lane < N_HEADS, _log_sigmoid(fg), 0.0)
    if sample:
        k_ref[...] = k
        v_ref[...] = v
        lf_ref[...] = logf[:, 0:N_HEADS]
        q_ref[...] = q.astype(BF16)
    else:
        k_ref[0] = k.T
        vt = v.T
        v_ref[0] = vt
        lf_ref[0] = logf.T[0:N_HEADS, :]
        ones_blk = jnp.where(lax.broadcasted_iota(jnp.int32, (V_PAD_ROWS, rows), 0) == 0, 1.0, 0.0)
        vb_ref[0] = jnp.concatenate(
            [blk for h in range(N_HEADS) for blk in (vt[h * HEAD_DIM:(h + 1) * HEAD_DIM, :], ones_blk)],
            axis=0).astype(BF16)

        @pl.when(i % tiles_per_seq == 0)
        def _():
            ccarry[...] = jnp.zeros((1, LANES), F32)
        c = _dot_exact_rhs(tri_ref[...], logf) + ccarry[...]
        ccarry[...] = c[rows - 1:rows, :]
        c1, c2, c3 = _split3(c)
        c_parts = (c1.astype(F32) + pltpu.roll(c2.astype(F32), N_HEADS, 1)
                   + pltpu.roll(c3.astype(F32), 2 * N_HEADS, 1)).astype(BF16)
        aug_q = _dot(c_parts, pq_ref[...]) + oq_ref[...]
        aug_k = ok_ref[...] - _dot(c_parts, pk_ref[...])

        def spread_heads(x, aug):
            groups = []
            for h in range(N_HEADS):
                slab = x[:, (h // 2) * LANES:(h // 2 + 1) * LANES]
                if h % 2:
                    slab = pltpu.roll(slab, HEAD_DIM, 1)
                groups.append(jnp.where(lane < HEAD_DIM, slab, aug[:, h * LANES:(h + 1) * LANES]))
            return jnp.concatenate(groups, axis=1).astype(BF16)

        q_ref[...] = spread_heads(q, aug_q)
        ka_ref[...] = spread_heads(k, aug_k)

    ga = _dot(xb, wg_ref[:, 0:d_model])
    gb = _dot(xb, wg_ref[:, d_model:2 * d_model])
    ap_ref[...] = jax.nn.sigmoid(ga) * ya
    sgb_ref[...] = jax.nn.sigmoid(gb)


def _inproj(x, w, *, sample, seq_len, state=None):
    t, d_model = x.shape
    d_conv = w['wco'].shape[0]
    rows = 256
    tiles_per_seq = max(seq_len // rows, 1)
    n_tiles = t // rows
    row_spec = lambda width: pl.BlockSpec((rows, width), lambda i: (i, 0))
    weights = [w['wm'], w['wf'], w['wg'], w['bf'], w['cw'], w['wco']]
    in_specs = [row_spec(d_model)] + [_const_spec(a.shape) for a in weights]
    args = [x] + weights
    if sample:
        args += [state[0], state[1]]
        in_specs += [row_spec(d_conv), row_spec(d_conv)]
        out_shape = [
            jax.ShapeDtypeStruct((t, D_ATTN), BF16),
            jax.ShapeDtypeStruct((t, D_ATTN), F32),
            jax.ShapeDtypeStruct((t, D_ATTN), F32),
            jax.ShapeDtypeStruct((t, N_HEADS), F32),
            jax.ShapeDtypeStruct((t, d_model), F32),
            jax.ShapeDtypeStruct((t, d_model), F32),
            jax.ShapeDtypeStruct((t, d_conv), F32),
        ]
        out_specs = [row_spec(D_ATTN), row_spec(D_ATTN), row_spec(D_ATTN), row_spec(N_HEADS),
                     row_spec(d_model), row_spec(d_model), row_spec(d_conv)]
        scratch = [pltpu.VMEM((rows + SUBLANES, d_conv), F32)]
    else:
        consts = [w['tri'], w['pq'], w['pk'], w['oq'], w['ok']]
        args += consts
        in_specs += [_const_spec(a.shape) for a in consts]
        n_seq = t // seq_len
        out_shape = [
            jax.ShapeDtypeStruct((t, N_HEADS * LANES), BF16),
            jax.ShapeDtypeStruct((n_seq, D_ATTN, seq_len), F32),
            jax.ShapeDtypeStruct((n_seq, D_ATTN, seq_len), F32),
            jax.ShapeDtypeStruct((t, N_HEADS * LANES), BF16),
            jax.ShapeDtypeStruct((n_tiles, N_HEADS * V_ROWS, rows), BF16),
            jax.ShapeDtypeStruct((n_seq, N_HEADS, seq_len), F32),
            jax.ShapeDtypeStruct((t, d_model), F32),
            jax.ShapeDtypeStruct((t, d_model), F32),
            jax.ShapeDtypeStruct((n_seq, SUBLANES, d_conv), F32),
        ]
        pos_minor = lambda height: pl.BlockSpec((1, height, rows),
                                                lambda i: (i // tiles_per_seq, 0, i % tiles_per_seq))
        out_specs = [row_spec(N_HEADS * LANES), pos_minor(D_ATTN), pos_minor(D_ATTN), row_spec(N_HEADS * LANES),
                     pl.BlockSpec((1, N_HEADS * V_ROWS, rows), lambda i: (i, 0, 0)), pos_minor(N_HEADS),
                     row_spec(d_model), row_spec(d_model),
                     pl.BlockSpec((1, SUBLANES, d_conv), lambda i: (i // tiles_per_seq, 0, 0))]
        scratch = [pltpu.VMEM((rows + SUBLANES, d_conv), F32), pltpu.VMEM((1, LANES), F32)]
    kern = functools.partial(_inproj_kernel, rows=rows, tiles_per_seq=tiles_per_seq, sample=sample,
                             d_conv=d_conv, d_model=d_model)
    return pl.pallas_call(
        kern,
        grid=(n_tiles,),
        in_specs=in_specs,
        out_specs=out_specs,
        out_shape=out_shape,
        scratch_shapes=scratch,
        compiler_params=pltpu.CompilerParams(dimension_semantics=("arbitrary",),
                                             vmem_limit_bytes=VMEM_LIMIT_BYTES),
        name="inproj_sample" if sample else "inproj_prompt",
    )(*args)


def _prompt_tile(i, q_ref, k_ref, vt_ref, o_ref, m_sc, acc_sc, *, tq, v_chunk, q_chunk):
    chunks = tq // v_chunk
    for hl in range(2):
        m_sc[hl] = jnp.full((1, tq), MASK_VALUE, F32)
        acc_sc[hl] = jnp.zeros((V_ROWS, tq), F32)

    def step(j, masked):
        start = pl.multiple_of(j * tq, tq)
        blocks = [(hl, qc) for hl in range(2) for qc in range(tq // q_chunk)]
        m_old = [m_sc[hl] for hl in range(2)]
        acc_old = [acc_sc[hl] for hl in range(2)]
        sts = [_dot_nt(k_ref[pl.ds(start, tq), hl * LANES:(hl + 1) * LANES],
                       q_ref[qc * q_chunk:(qc + 1) * q_chunk, hl * LANES:(hl + 1) * LANES]) for hl, qc in blocks]
        m_out, acc_out = [[], []], [[], []]
        for (hl, qc), st in zip(blocks, sts):
            qs = slice(qc * q_chunk, (qc + 1) * q_chunk)
            if masked:
                key = lax.broadcasted_iota(jnp.int32, (tq, q_chunk), 0)
                qry = lax.broadcasted_iota(jnp.int32, (tq, q_chunk), 1) + qc * q_chunk
                st = jnp.where(key <= qry, st, MASK_VALUE)
            m_prev = m_old[hl][:, qs]
            m_new = jnp.maximum(m_prev, jnp.max(st, axis=0, keepdims=True))
            alpha = jnp.exp(m_prev - m_new)
            pb = jnp.exp(st - m_new).astype(BF16)
            acc = alpha * acc_old[hl][:, qs]
            for c in range(chunks):
                vt = vt_ref[j * chunks + c, hl * V_ROWS:(hl + 1) * V_ROWS, :]
                acc = acc + _dot(vt, pb[c * v_chunk:(c + 1) * v_chunk, :])
            acc_out[hl].append(acc)
            m_out[hl].append(m_new)
        for hl in range(2):
            m_sc[hl] = jnp.concatenate(m_out[hl], axis=1)
            acc_sc[hl] = jnp.concatenate(acc_out[hl], axis=1)

    def body(j, carry):
        step(j, False)
        return carry

    lax.fori_loop(0, i, body, 0)
    step(i, True)
    ot = jnp.concatenate([acc_sc[hl, 0:HEAD_DIM, :] / acc_sc[hl, HEAD_DIM:HEAD_DIM + 1, :] for hl in range(2)],
                         axis=0)
    o_ref[...] = ot.T.astype(BF16)


def _sample_sequence(slot, q_ref, kn_ref, vn_ref, fn_ref, e_ref, suffix_ref, later_ref, tri_ref, o_ref,
                     k_buf, v_buf, f_buf, pad_sc, kn_sc, vn_sc, *, n_pages, n_new):
    n_rows = N_HEADS * n_new
    qb = q_ref[0]
    e = e_ref[...]

    pad_sc[...] = jnp.zeros((PAGE_SIZE, LANES), F32)
    pad_sc[0:n_new, 0:N_HEADS] = fn_ref[0]
    kn_sc[...] = jnp.zeros((PAGE_SIZE, D_ATTN), F32)
    kn_sc[0:n_new, :] = kn_ref[0]
    vn_sc[...] = jnp.zeros((PAGE_SIZE, D_ATTN), F32)
    vn_sc[0:n_new, :] = vn_ref[0]
    cn = _dot_exact_rhs(tri_ref[...], pad_sc[...])
    cn_rows = _dot_nt_exact_rhs(e, cn)
    row_t = lax.broadcasted_iota(jnp.int32, (n_rows, LANES), 0) // N_HEADS
    col = lax.broadcasted_iota(jnp.int32, (n_rows, LANES), 1)
    cnq = jnp.sum(jnp.where(col == row_t, cn_rows, 0.0), axis=1, keepdims=True)
    s_new = _dot_nt(qb, kn_sc[...].astype(BF16)) + (cnq - cn_rows)
    s_new = jnp.where(col <= row_t, s_new, MASK_VALUE)

    lf = f_buf[slot].reshape(n_pages * N_HEADS, PAGE_SIZE)
    incl = _dot_exact_lhs(lf, suffix_ref[...])
    tot = jnp.broadcast_to(incl[:, 0:1], incl.shape)
    d = incl - lf + _dot_exact_rhs(later_ref[...], tot)
    s_pages = []
    for j in range(n_pages):
        bias = jnp.concatenate([d[j * N_HEADS:(j + 1) * N_HEADS, :]] * n_new, axis=0)
        s_pages.append(_dot(qb, k_buf[slot, j].astype(BF16)) + bias + cnq)

    m_blk = s_new
    for s in s_pages:
        m_blk = jnp.maximum(m_blk, s)
    m = jnp.max(m_blk, axis=1, keepdims=True)
    p_new = jnp.exp(s_new - m)
    l_blk = p_new
    acc = _dot(p_new.astype(BF16), vn_sc[...].astype(BF16))
    for j in range(n_pages):
        p = jnp.exp(s_pages[j] - m)
        l_blk = l_blk + p
        acc = acc + _dot_nt(p.astype(BF16), v_buf[slot, j].astype(BF16))
    o_ref[0] = acc / jnp.sum(l_blk, axis=1, keepdims=True)


def _attention_kernel(pt_ref, q_ref, k_ref, vt_ref, qs_ref, kn_ref, vn_ref, fn_ref, e_ref, suffix_ref, later_ref,
                      tri_ref, kt_hbm, vtc_hbm, lft_hbm, o_ref, os_ref, m_sc, acc_sc, k_buf, v_buf, f_buf, sems,
                      pad_sc, kn_sc, vn_sc, *, tq, v_chunk, q_chunk, n_pages, n_new):
    step = (pl.program_id(0) * pl.num_programs(1) + pl.program_id(1)) * pl.num_programs(2) + pl.program_id(2)
    n_steps = pl.num_programs(0) * pl.num_programs(1) * pl.num_programs(2)
    slot = step % 2
    pools = ((kt_hbm, k_buf), (vtc_hbm, v_buf), (lft_hbm, f_buf))

    def start_pages(seq, to_slot):
        for j in range(n_pages):
            page = pt_ref[seq * n_pages + j]
            for a, (hbm, buf) in enumerate(pools):
                pltpu.make_async_copy(hbm.at[page], buf.at[to_slot, j], sems.at[a, to_slot]).start()

    @pl.when(step == 0)
    def _():
        start_pages(0, 0)

    @pl.when(step + 1 < n_steps)
    def _():
        start_pages(step + 1, 1 - slot)

    _prompt_tile(pl.program_id(2), q_ref, k_ref, vt_ref, o_ref, m_sc, acc_sc, tq=tq, v_chunk=v_chunk,
                 q_chunk=q_chunk)

    for a, (hbm, buf) in enumerate(pools):
        pltpu.make_async_copy(hbm.at[pl.ds(0, n_pages)], buf.at[slot], sems.at[a, slot]).wait()
    _sample_sequence(slot, qs_ref, kn_ref, vn_ref, fn_ref, e_ref, suffix_ref, later_ref, tri_ref, os_ref,
                     k_buf, v_buf, f_buf, pad_sc, kn_sc, vn_sc, n_pages=n_pages, n_new=n_new)


def _attention(qa, ka, vtb, qblk, cache_kt, cache_vt, cache_lft, page_table, k_new, v_new, lf_new,
               *, n_prompt_seq, seq_len):
    t = qa.shape[0]
    v_chunk = vtb.shape[2]
    tq = 512
    qt = seq_len // tq
    n_pairs = N_HEADS // 2
    n_seq, n_rows, _ = qblk.shape
    assert n_seq == n_prompt_seq * n_pairs * qt, "one sample sequence per prompt attention tile"
    n_new = k_new.shape[1]
    n_pages = page_table.shape[1]
    pt = page_table.reshape(-1)
    row_head = jnp.arange(n_rows) % N_HEADS
    e = (row_head[:, None] == jnp.arange(LANES)[None, :]).astype(BF16)
    pos = jnp.arange(PAGE_SIZE)
    suffix = (pos[:, None] >= pos[None, :]).astype(BF16)
    tri = (pos[None, :] <= pos[:, None]).astype(BF16)
    r = jnp.arange(n_pages * N_HEADS)
    later = ((r[:, None] % N_HEADS == r[None, :] % N_HEADS)
             & (r[None, :] // N_HEADS > r[:, None] // N_HEADS)).astype(BF16)

    tile_q = lambda n, hp, i, pt_ref: (n * qt + i, hp)
    seq_map = lambda n, hp, i, pt_ref: ((n * n_pairs + hp) * qt + i, 0, 0)
    const_map = lambda n, hp, i, pt_ref: (0, 0)
    any_spec = pl.BlockSpec(memory_space=pl.ANY)
    in_specs = [pl.BlockSpec((tq, 2 * LANES), tile_q),
                pl.BlockSpec((seq_len, 2 * LANES), lambda n, hp, i, pt_ref: (n, hp)),
                pl.BlockSpec((seq_len // v_chunk, 2 * V_ROWS, v_chunk), lambda n, hp, i, pt_ref: (n, hp, 0)),
                pl.BlockSpec((1, n_rows, D_ATTN), seq_map),
                pl.BlockSpec((1, n_new, D_ATTN), seq_map), pl.BlockSpec((1, n_new, D_ATTN), seq_map),
                pl.BlockSpec((1, n_new, N_HEADS), seq_map),
                pl.BlockSpec(e.shape, const_map), pl.BlockSpec(suffix.shape, const_map),
                pl.BlockSpec(later.shape, const_map), pl.BlockSpec(tri.shape, const_map),
                any_spec, any_spec, any_spec]
    grid_spec = pltpu.PrefetchScalarGridSpec(
        num_scalar_prefetch=1,
        grid=(n_prompt_seq, n_pairs, qt),
        in_specs=in_specs,
        out_specs=[pl.BlockSpec((tq, LANES), tile_q), pl.BlockSpec((1, n_rows, D_ATTN), seq_map)],
        scratch_shapes=[pltpu.VMEM((2, 1, tq), F32), pltpu.VMEM((2, V_ROWS, tq), F32),
                        pltpu.VMEM((2, n_pages, D_ATTN, PAGE_SIZE), F32),
                        pltpu.VMEM((2, n_pages, D_ATTN, PAGE_SIZE), F32),
                        pltpu.VMEM((2, n_pages, N_HEADS, PAGE_SIZE), F32),
                        pltpu.SemaphoreType.DMA((3, 2)),
                        pltpu.VMEM((PAGE_SIZE, LANES), F32),
                        pltpu.VMEM((PAGE_SIZE, D_ATTN), F32), pltpu.VMEM((PAGE_SIZE, D_ATTN), F32)],
    )
    args = [pt, qa, ka, vtb, qblk, k_new, v_new, lf_new, e, suffix, later, tri, cache_kt, cache_vt, cache_lft]
    return pl.pallas_call(
        functools.partial(_attention_kernel, tq=tq, v_chunk=v_chunk, q_chunk=LANES, n_pages=n_pages, n_new=n_new),
        grid_spec=grid_spec,
        out_shape=[jax.ShapeDtypeStruct((t, D_ATTN), BF16), jax.ShapeDtypeStruct((n_seq, n_rows, D_ATTN), F32)],
        compiler_params=pltpu.CompilerParams(dimension_semantics=("arbitrary", "arbitrary", "arbitrary"),
                                             vmem_limit_bytes=VMEM_LIMIT_BYTES),
        name="attention",
    )(*args)


def _route(logits):
    rows = logits.shape[0]
    lane = lax.broadcasted_iota(jnp.int32, (rows, LANES), 1).astype(F32)
    big = float(LANES)
    in_g = lane < N_GROUPS
    gl = jnp.where(in_g, logits, MASK_VALUE)
    m_g = jnp.max(gl, axis=1, keepdims=True)
    z_g = jnp.sum(jnp.exp(gl - m_g), axis=1, keepdims=True)
    p_g = 1.0 / z_g
    g_idx = jnp.min(jnp.where(in_g & (gl == m_g), lane, big), axis=1, keepdims=True)
    lo = N_GROUPS + EXPERTS_PER_GROUP * g_idx
    sel = (lane >= lo) & (lane < lo + EXPERTS_PER_GROUP)
    el = jnp.where(sel, logits, MASK_VALUE)
    m_e = jnp.max(el, axis=1, keepdims=True)
    ex = jnp.exp(el - m_e)
    ep = ex / jnp.sum(ex, axis=1, keepdims=True)
    v1 = jnp.max(jnp.where(sel, ep, -1.0), axis=1, keepdims=True)
    i1 = jnp.min(jnp.where(sel & (ep == v1), lane, big), axis=1, keepdims=True)
    sel2 = sel & (lane != i1)
    v2 = jnp.max(jnp.where(sel2, ep, -1.0), axis=1, keepdims=True)
    i2 = jnp.min(jnp.where(sel2 & (ep == v2), lane, big), axis=1, keepdims=True)
    denom = v1 + v2
    w1 = p_g * (v1 / denom)
    w2 = p_g * (v2 / denom)
    gate4 = jnp.where(lane == i1 - lo, w1, 0.0) + jnp.where(lane == i2 - lo, w2, 0.0)
    return gate4, g_idx


def _post_kernel(o_ref, ap_ref, sgb_ref, x_ref, wao_ref, wo_ref, g_ref, b_ref, wr_ref, tri_ref,
                 x1e_ref, route_t_ref, totals_ref, count_sc, *, alpha, d_model):
    i = pl.program_id(0)
    rows = x_ref.shape[0]
    yb = _dot(o_ref[...], wao_ref[...])
    mix = _dot((ap_ref[...] + sgb_ref[...] * yb).astype(BF16), wo_ref[...])
    x1 = _layer_norm(alpha * x_ref[...] + mix, g_ref[...], b_ref[...])
    xh = x1.astype(BF16)
    xl = (x1 - xh.astype(F32)).astype(BF16)
    hi_lo = _dot(xh, wr_ref[...])
    logits = hi_lo[:, 0:LANES] + hi_lo[:, LANES:2 * LANES] + _dot(xl, wr_ref[:, 0:LANES])
    gate4, g_idx = _route(logits)

    @pl.when(i == 0)
    def _():
        count_sc[...] = jnp.zeros((1, LANES), F32)
    lane = lax.broadcasted_iota(jnp.int32, (rows, LANES), 1).astype(F32)
    onehot = jnp.where(lane == g_idx, 1.0, 0.0)
    count = _dot(tri_ref[...], onehot.astype(BF16)) + count_sc[...]
    rank = jnp.sum(onehot * count, axis=1, keepdims=True) - 1.0
    count_sc[...] = count[rows - 1:rows, :]
    totals_ref[...] = jnp.broadcast_to(count[rows - 1:rows, :], (SUBLANES, LANES))
    route = (gate4 + jnp.where(lane == ROUTE_GROUP_LANE, g_idx, 0.0)
             + jnp.where(lane == ROUTE_RANK_LANE, rank, 0.0))
    route_t_ref[...] = route.T[0:SUBLANES, :]
    x1e_ref[:, 0:d_model] = x1
    x1e_ref[:, d_model:d_model + LANES] = route


def _post_attention(o, ap, sgb, x, w, *, alpha):
    t, d_model = x.shape
    rows = MOE_ROWS
    row_spec = lambda width: pl.BlockSpec((rows, width), lambda i: (i, 0))
    weights = [w['wao'], w['wo'], w['ln1_g'], w['ln1_b'], w['wr'], w['tri_moe']]
    return pl.pallas_call(
        functools.partial(_post_kernel, alpha=alpha, d_model=d_model),
        grid=(t // rows,),
        in_specs=[row_spec(D_ATTN), row_spec(d_model), row_spec(d_model), row_spec(d_model)]
                 + [_const_spec(a.shape) for a in weights],
        out_specs=[row_spec(d_model + LANES), pl.BlockSpec((SUBLANES, rows), lambda i: (0, i)),
                   pl.BlockSpec((SUBLANES, LANES), lambda i: (0, 0))],
        out_shape=[jax.ShapeDtypeStruct((t, d_model + LANES), F32),
                   jax.ShapeDtypeStruct((SUBLANES, t), F32),
                   jax.ShapeDtypeStruct((SUBLANES, LANES), F32)],
        scratch_shapes=[pltpu.VMEM((1, LANES), F32)],
        compiler_params=pltpu.CompilerParams(dimension_semantics=("arbitrary",),
                                             vmem_limit_bytes=VMEM_LIMIT_BYTES),
        name="post_attn",
    )(o, ap, sgb, x, *weights)


def _row_permute_kernel(pos_ref, *refs, rows, scatter):
    if scatter:
        fill_ref, src_ref, dst_ref, zero_sc, sem, fill_sem = refs

        @pl.when(pl.program_id(0) == 0)
        def _():
            zero_sc[...] = jnp.zeros_like(zero_sc)
            for g in range(fill_ref.shape[0]):
                fill = pltpu.make_async_copy(zero_sc, dst_ref.at[pl.ds(fill_ref[g] * rows, rows)], fill_sem)
                fill.start()
                fill.wait()
    else:
        src_ref, dst_ref, sem = refs
    base = pl.program_id(0) * rows

    def issue(r, carry):
        p = pos_ref[base + r]
        if scatter:
            pltpu.make_async_copy(src_ref.at[pl.ds(r, 1)], dst_ref.at[pl.ds(p, 1)], sem).start()
        else:
            pltpu.make_async_copy(src_ref.at[pl.ds(p, 1)], dst_ref.at[pl.ds(r, 1)], sem).start()
        return carry

    lax.fori_loop(0, rows, issue, 0, unroll=True)
    if scatter:
        pltpu.make_async_copy(src_ref, dst_ref.at[pl.ds(0, rows)], sem).wait()
    else:
        pltpu.make_async_copy(src_ref.at[pl.ds(0, rows)], dst_ref, sem).wait()


def _row_permute(pos, src, *, n_out, scatter, fill_tiles=None):
    n_rows = pos.shape[0]
    width = src.shape[1]
    rows = MOE_ROWS
    any_spec = pl.BlockSpec(memory_space=pl.ANY)
    tile_spec = pl.BlockSpec((rows, width), lambda i, *_: (i, 0))
    if scatter:
        args = [pos, fill_tiles, src]
        in_specs, out_specs = [tile_spec], any_spec
        scratch = [pltpu.VMEM((rows, width), src.dtype), pltpu.SemaphoreType.DMA(()), pltpu.SemaphoreType.DMA(())]
    else:
        args = [pos, src]
        in_specs, out_specs = [any_spec], tile_spec
        scratch = [pltpu.SemaphoreType.DMA(())]
    grid_spec = pltpu.PrefetchScalarGridSpec(
        num_scalar_prefetch=len(args) - 1, grid=(n_rows // rows,), in_specs=in_specs, out_specs=out_specs,
        scratch_shapes=scratch)
    return pl.pallas_call(
        functools.partial(_row_permute_kernel, rows=rows, scatter=scatter),
        grid_spec=grid_spec,
        out_shape=jax.ShapeDtypeStruct((n_out, width), src.dtype),
        compiler_params=pltpu.CompilerParams(dimension_semantics=("arbitrary",)),
        name="row_scatter" if scatter else "row_gather",
    )(*args)


def _moe_kernel(tg_ref, nv_ref, xs_ref, w1_ref, w3_ref, w2_ref, g_ref, b_ref, y_ref, *, alpha, d_model):
    del tg_ref
    i = pl.program_id(0)

    @pl.when(i < nv_ref[0])
    def _():
        x = xs_ref[:, 0:d_model]
        route = xs_ref[:, d_model:d_model + LANES]
        xb = x.astype(BF16)
        lane = lax.broadcasted_iota(jnp.int32, route.shape, 1)
        acc = jnp.zeros_like(x)
        for e in range(EXPERTS_PER_GROUP):
            h1 = _dot(xb, w1_ref[0, e])
            h3 = _dot(xb, w3_ref[0, e])
            h = (h1 * jax.nn.sigmoid(h1)) * h3
            ye = _dot(h.astype(BF16), w2_ref[0, e])
            ge = jnp.sum(jnp.where(lane == e, route, 0.0), axis=1, keepdims=True)
            acc = acc + ge * ye
        y_ref[...] = _layer_norm(alpha * x + acc, g_ref[...], b_ref[...])

    @pl.when(i >= nv_ref[0])
    def _():
        y_ref[...] = jnp.zeros_like(y_ref)


def _experts(x1e, route_t, totals, w, *, alpha):
    t = x1e.shape[0]
    d_model = x1e.shape[1] - LANES
    _, _, _, d_exp = w['w1'].shape
    rows = MOE_ROWS
    n_tiles = t // rows + N_GROUPS
    group = route_t[ROUTE_GROUP_LANE].astype(jnp.int32)
    rank = route_t[ROUTE_RANK_LANE].astype(jnp.int32)
    count = totals[0, 0:N_GROUPS].astype(jnp.int32)
    tiles_g = (count + rows - 1) // rows
    tile_end = jnp.cumsum(tiles_g)
    pos = (tile_end - tiles_g)[group] * rows + rank
    n_valid = tile_end[N_GROUPS - 1:N_GROUPS]
    tile_group = jnp.minimum(jnp.sum(jnp.arange(n_tiles)[:, None] >= tile_end[None, :], axis=1),
                             N_GROUPS - 1).astype(jnp.int32)

    fill_tiles = jnp.concatenate([jnp.maximum(tile_end - 1, 0),
                                  jnp.minimum(n_valid + jnp.arange(N_GROUPS), n_tiles - 1)]).astype(jnp.int32)
    xs = _row_permute(pos, x1e, n_out=n_tiles * rows, scatter=True, fill_tiles=fill_tiles)

    live = lambda i, nv: jnp.minimum(i, nv[0] - 1)
    w_spec = lambda a: pl.BlockSpec((1,) + a.shape[1:], lambda i, tg, nv: (tg[live(i, nv)], 0, 0, 0))
    grid_spec = pltpu.PrefetchScalarGridSpec(
        num_scalar_prefetch=2,
        grid=(n_tiles,),
        in_specs=[
            pl.BlockSpec((rows, d_model + LANES), lambda i, tg, nv: (live(i, nv), 0)),
            w_spec(w['w1']), w_spec(w['w3']), w_spec(w['w2']),
            pl.BlockSpec((1, d_model), lambda i, tg, nv: (0, 0)),
            pl.BlockSpec((1, d_model), lambda i, tg, nv: (0, 0)),
        ],
        out_specs=pl.BlockSpec((rows, d_model), lambda i, tg, nv: (i, 0)),
    )
    ys = pl.pallas_call(
        functools.partial(_moe_kernel, alpha=alpha, d_model=d_model),
        grid_spec=grid_spec,
        out_shape=jax.ShapeDtypeStruct((n_tiles * rows, d_model), F32),
        compiler_params=pltpu.CompilerParams(dimension_semantics=("arbitrary",),
                                             vmem_limit_bytes=VMEM_LIMIT_BYTES),
        name="experts",
    )(tile_group, n_valid, xs, w['w1'], w['w3'], w['w2'], w['ln2_g'], w['ln2_b'])
    return _row_permute(pos, ys, n_out=t, scatter=False)


def _prepare_weights(w_in, b_f, conv_w, w_conv_out, w_attn_out, w_o, ln1_g, ln1_b,
                     w_group, w_router, w1, w3, w2, ln2_g, ln2_b, rows_inproj):
    d_model = w_in.shape[0]
    d_conv = w_conv_out.shape[0]
    o_q = 3 * d_conv
    o_k = o_q + D_ATTN
    o_v = o_k + D_ATTN
    o_f = o_v + D_ATTN
    o_g = o_f + N_HEADS
    w_bf = w_in.astype(BF16)
    wm = w_bf[:, 0:o_f]
    wf = jnp.pad(w_bf[:, o_f:o_g], ((0, 0), (0, LANES - N_HEADS)))
    wg = w_bf[:, o_g:]
    bf = jnp.pad(b_f.astype(F32), (0, LANES - N_HEADS)).reshape(1, LANES)

    col = jnp.arange(N_HEADS * LANES)
    col_head, col_lane = col // LANES, col % LANES
    part = jnp.arange(LANES)
    part_k, part_h = part // N_HEADS, part % N_HEADS
    live = (part_k[:, None] < AUG_SPLITS) & (part_h[:, None] == col_head[None, :])
    pq = live & (col_lane[None, :] == AUG_Q_C + part_k[:, None])
    pk = live & (col_lane[None, :] == AUG_Q_C + AUG_SPLITS + part_k[:, None])
    oq = ((col_lane >= AUG_Q_C + AUG_SPLITS) & (col_lane < AUG_Q_C + 2 * AUG_SPLITS)).astype(F32).reshape(1, -1)
    ok = ((col_lane >= AUG_Q_C) & (col_lane < AUG_Q_C + AUG_SPLITS)).astype(F32).reshape(1, -1)
    r = jnp.arange(rows_inproj)
    tri = (r[None, :] <= r[:, None]).astype(BF16)
    r = jnp.arange(MOE_ROWS)
    tri_moe = (r[None, :] <= r[:, None]).astype(BF16)

    wr = jnp.pad(jnp.concatenate([w_group, w_router], axis=1), ((0, 0), (0, LANES - N_GROUPS - N_EXPERTS)))
    wr_hi = wr.astype(BF16)
    wr_lo = (wr - wr_hi.astype(F32)).astype(BF16)
    wr = jnp.concatenate([wr_hi, wr_lo], axis=1)
    return dict(
        wm=wm, wf=wf, wg=wg, bf=bf, cw=conv_w.astype(F32), wco=w_conv_out.astype(BF16),
        tri=tri, pq=pq.astype(BF16), pk=pk.astype(BF16), oq=oq, ok=ok,
        wao=w_attn_out.astype(BF16), wo=w_o.astype(BF16),
        ln1_g=ln1_g.reshape(1, -1), ln1_b=ln1_b.reshape(1, -1), wr=wr,
        tri_moe=tri_moe,
        w1=w1.astype(BF16).reshape((N_GROUPS, EXPERTS_PER_GROUP) + w1.shape[1:]),
        w3=w3.astype(BF16).reshape((N_GROUPS, EXPERTS_PER_GROUP) + w3.shape[1:]),
        w2=w2.astype(BF16).reshape((N_GROUPS, EXPERTS_PER_GROUP) + w2.shape[1:]),
        ln2_g=ln2_g.reshape(1, -1), ln2_b=ln2_b.reshape(1, -1),
    )


def kernel(x_prompt, x_sample, cache_k, cache_v, cache_logf, state_conv, page_table, w_in, b_f, conv_w,
           w_conv_out, w_attn_out, w_o, ln1_g, ln1_b, w_group, w_router, w1, w3, w2, ln2_g, ln2_b):
    depth = w_in.shape[0]
    assert depth == 1, "single-layer stack"
    n_p, s_p, d_model = x_prompt.shape
    n_s, s_s, _ = x_sample.shape
    d_conv = w_conv_out.shape[1]
    alpha = (2 * depth) ** 0.25
    w = _prepare_weights(w_in[0], b_f[0], conv_w[0], w_conv_out[0], w_attn_out[0], w_o[0], ln1_g[0], ln1_b[0],
                         w_group[0], w_router[0], w1[0], w3[0], w2[0], ln2_g[0], ln2_b[0], rows_inproj=256)

    xp = x_prompt.reshape(n_p * s_p, d_model)
    qa, kt_p, vt_p, ka, vtb, lft_p, ap, sgb, tail = _inproj(xp, w, sample=False, seq_len=s_p)
    xs = x_sample.reshape(n_s * s_s, d_model)
    state = state_conv[0]
    s0 = jnp.repeat(state[:, 0, :], s_s, axis=0)
    s1 = jnp.repeat(state[:, 1, :], s_s, axis=0)
    q_s, k_s, v_s, lf_s, ap_s, sgb_s, u_s = _inproj(xs, w, sample=True, seq_len=s_s, state=(s0, s1))
    q5 = q_s.reshape(n_s, s_s, N_HEADS, HEAD_DIM)
    eye_h = jnp.eye(N_HEADS, dtype=BF16)
    qblk = (q5[:, :, :, None, :] * eye_h[None, None, :, :, None]).reshape(n_s, s_s * N_HEADS, D_ATTN)
    n_pool = cache_k.shape[1]
    cache_kt = cache_k[0].transpose(0, 2, 3, 1).reshape(n_pool, D_ATTN, PAGE_SIZE)
    cache_vt = cache_v[0].transpose(0, 2, 3, 1).reshape(n_pool, D_ATTN, PAGE_SIZE)
    cache_lft = cache_logf[0].transpose(0, 2, 1)
    o_p, o_blk = _attention(
        qa, ka, vtb, qblk, cache_kt, cache_vt, cache_lft, page_table, k_s.reshape(n_s, s_s, D_ATTN),
        v_s.reshape(n_s, s_s, D_ATTN), lf_s.reshape(n_s, s_s, N_HEADS), n_prompt_seq=n_p, seq_len=s_p)
    x1e_p, route_p, totals_p = _post_attention(o_p, ap, sgb, xp, w, alpha=alpha)
    y_p = _experts(x1e_p, route_p, totals_p, w, alpha=alpha)
    o5 = o_blk.reshape(n_s, s_s, N_HEADS, N_HEADS, HEAD_DIM)
    o_s = jnp.einsum('bthgd,hg->bthd', o5, jnp.eye(N_HEADS, dtype=F32)).reshape(n_s * s_s, D_ATTN).astype(BF16)
    x1e_s, route_s, totals_s = _post_attention(o_s, ap_s, sgb_s, xs, w, alpha=alpha)
    y_s = _experts(x1e_s, route_s, totals_s, w, alpha=alpha)

    return (
        y_p.reshape(n_p, s_p, d_model),
        y_s.reshape(n_s, s_s, d_model),
        kt_p.reshape(1, n_p, N_HEADS, HEAD_DIM, s_p).transpose(0, 1, 4, 2, 3),
        vt_p.reshape(1, n_p, N_HEADS, HEAD_DIM, s_p).transpose(0, 1, 4, 2, 3),
        lft_p.reshape(1, n_p, N_HEADS, s_p).transpose(0, 1, 3, 2),
        tail[:, SUBLANES - (CONV_WIDTH - 1):, :].reshape(1, n_p, CONV_WIDTH - 1, d_conv),
        k_s.reshape(1, n_s, s_s, N_HEADS, HEAD_DIM),
        v_s.reshape(1, n_s, s_s, N_HEADS, HEAD_DIM),
        lf_s.reshape(1, n_s, s_s, N_HEADS),
        u_s.reshape(n_s, s_s, d_conv)[:, s_s - (CONV_WIDTH - 1):, :].reshape(1, n_s, CONV_WIDTH - 1, d_conv),
    )
```

```python
import functools

import jax
import jax.numpy as jnp
from jax import lax
from jax.experimental import pallas as pl
from jax.experimental.pallas import tpu as pltpu

F32 = jnp.float32
BF16 = jnp.bfloat16

LANES = 128
SUBLANES = 8
VMEM_LIMIT_BYTES = 56 * 1024 * 1024

CONV_WIDTH = 3
N_HEADS = 8
HEAD_DIM = 64
D_ATTN = N_HEADS * HEAD_DIM
N_GROUPS = 4
EXPERTS_PER_GROUP = 4
N_EXPERTS = N_GROUPS * EXPERTS_PER_GROUP
LN_EPS = 1e-5
PAGE_SIZE = 128
MASK_VALUE = -1e30
AUG_Q_C = HEAD_DIM
AUG_SPLITS = 3
INPROJ_ROWS = 256
ATTN_TILE = 512
MOE_ROWS = 512
ROUTE_GROUP_LANE = EXPERTS_PER_GROUP
ROUTE_RANK_LANE = EXPERTS_PER_GROUP + 1
V_PAD_ROWS = 16
V_ROWS = HEAD_DIM + V_PAD_ROWS


def _dot(a, b):
    return jnp.dot(a, b, preferred_element_type=F32)


def _dot_nt(a, b):
    return lax.dot_general(a, b, (((1,), (1,)), ((), ())), preferred_element_type=F32)


def _split3(x):
    h1 = x.astype(BF16)
    r1 = x - h1.astype(F32)
    h2 = r1.astype(BF16)
    r2 = r1 - h2.astype(F32)
    h3 = r2.astype(BF16)
    return h1, h2, h3


def _dot_exact_rhs(a_bf16, x):
    h1, h2, h3 = _split3(x)
    return _dot(a_bf16, h1) + _dot(a_bf16, h2) + _dot(a_bf16, h3)


def _dot_exact_lhs(x, b_bf16):
    h1, h2, h3 = _split3(x)
    return _dot(h1, b_bf16) + _dot(h2, b_bf16) + _dot(h3, b_bf16)


def _dot_nt_exact_rhs(a_bf16, x):
    h1, h2, h3 = _split3(x)
    return _dot_nt(a_bf16, h1) + _dot_nt(a_bf16, h2) + _dot_nt(a_bf16, h3)


def _log_sigmoid(x):
    return jnp.minimum(x, 0.0) - jnp.log1p(jnp.exp(-jnp.abs(x)))


def _layer_norm(r, g, b):
    mu = jnp.mean(r, axis=-1, keepdims=True)
    d = r - mu
    var = jnp.mean(d * d, axis=-1, keepdims=True)
    return d * lax.rsqrt(var + LN_EPS) * g + b


def _const_spec(shape):
    nd = len(shape)
    return pl.BlockSpec(shape, lambda *_: (0,) * nd, pipeline_mode=pl.Buffered(1))


def _inproj_kernel(*refs, rows, tiles_per_seq, sample, d_conv, d_model):
    if sample:
        (x_ref, wm_ref, wf_ref, wg_ref, bf_ref, cw_ref, wco_ref, s0_ref, s1_ref,
         q_ref, k_ref, v_ref, lf_ref, ap_ref, sgb_ref, u_ref, ubuf) = refs
    else:
        (x_ref, wm_ref, wf_ref, wg_ref, bf_ref, cw_ref, wco_ref, tri_ref, pq_ref, pk_ref,
         oq_ref, ok_ref,
         q_ref, k_ref, v_ref, ka_ref, vb_ref, lf_ref, ap_ref, sgb_ref, tail_ref, ubuf, ccarry) = refs
    i = pl.program_id(0)
    xb = x_ref[...].astype(BF16)

    xc = _dot(xb, wm_ref[:, 0:d_conv])
    bg = _dot(xb, wm_ref[:, d_conv:2 * d_conv])
    cg = _dot(xb, wm_ref[:, 2 * d_conv:3 * d_conv])
    u = cg * xc
    if sample:
        ubuf[0:SUBLANES, :] = jnp.zeros((SUBLANES, d_conv), F32)
    else:
        @pl.when(i % tiles_per_seq == 0)
        def _():
            ubuf[0:SUBLANES, :] = jnp.zeros((SUBLANES, d_conv), F32)
    ubuf[SUBLANES:SUBLANES + rows, :] = u
    prev1 = ubuf[SUBLANES - 1:SUBLANES - 1 + rows, :]
    prev2 = ubuf[SUBLANES - 2:SUBLANES - 2 + rows, :]
    if sample:
        pos = lax.broadcasted_iota(jnp.int32, (rows, 1), 0) % 4
        prev1 = jnp.where(pos == 0, s1_ref[...], prev1)
        prev2 = jnp.where(pos == 0, s0_ref[...], jnp.where(pos == 1, s1_ref[...], prev2))
        u_ref[...] = u
    else:
        ubuf[0:SUBLANES, :] = u[rows - SUBLANES:rows, :]

        @pl.when(i % tiles_per_seq == tiles_per_seq - 1)
        def _():
            tail_ref[0] = u[rows - SUBLANES:rows, :]
    conv = cw_ref[0:1, :] * prev2 + cw_ref[1:2, :] * prev1 + cw_ref[2:3, :] * u
    ya = _dot((bg * conv).astype(BF16), wco_ref[...])

    q = _dot(xb, wm_ref[:, 3 * d_conv:3 * d_conv + D_ATTN]) * (HEAD_DIM ** -0.5)
    k = _dot(xb, wm_ref[:, 3 * d_conv + D_ATTN:3 * d_conv + 2 * D_ATTN])
    v = _dot(xb, wm_ref[:, 3 * d_conv + 2 * D_ATTN:3 * d_conv + 3 * D_ATTN])
    fg = _dot(xb, wf_ref[...]) + bf_ref[...]
    lane = lax.broadcasted_iota(jnp.int32, (rows, LANES), 1)
    logf = jnp.where(lane < N_HEADS, _log_sigmoid(fg), 0.0)
    if sample:
        k_ref[...] = k
        v_ref[...] = v
        lf_ref[...] = logf[:, 0:N_HEADS]
        q_ref[...] = q.astype(BF16)
    else:
        k_ref[0] = k.T
        vt = v.T
        v_ref[0] = vt
        lf_ref[0] = logf.T[0:N_HEADS, :]
        ones_blk = jnp.where(lax.broadcasted_iota(jnp.int32, (V_PAD_ROWS, rows), 0) == 0, 1.0, 0.0)
        vb_ref[0] = jnp.concatenate(
            [blk for h in range(N_HEADS) for blk in (vt[h * HEAD_DIM:(h + 1) * HEAD_DIM, :], ones_blk)],
            axis=0).astype(BF16)

        @pl.when(i % tiles_per_seq == 0)
        def _():
            ccarry[...] = jnp.zeros((1, LANES), F32)
        c = _dot_exact_rhs(tri_ref[...], logf) + ccarry[...]
        ccarry[...] = c[rows - 1:rows, :]
        c1, c2, c3 = _split3(c)
        c_parts = (c1.astype(F32) + pltpu.roll(c2.astype(F32), N_HEADS, 1)
                   + pltpu.roll(c3.astype(F32), 2 * N_HEADS, 1)).astype(BF16)
        aug_q = _dot(c_parts, pq_ref[...]) + oq_ref[...]
        aug_k = ok_ref[...] - _dot(c_parts, pk_ref[...])

        def spread_heads(x, aug):
            groups = []
            for h in range(N_HEADS):
                slab = x[:, (h // 2) * LANES:(h // 2 + 1) * LANES]
                if h % 2:
                    slab = pltpu.roll(slab, HEAD_DIM, 1)
                groups.append(jnp.where(lane < HEAD_DIM, slab, aug[:, h * LANES:(h + 1) * LANES]))
            return jnp.concatenate(groups, axis=1).astype(BF16)

        q_ref[...] = spread_heads(q, aug_q)
        ka_ref[...] = spread_heads(k, aug_k)

    ga = _dot(xb, wg_ref[:, 0:d_model])
    gb = _dot(xb, wg_ref[:, d_model:2 * d_model])
    ap_ref[...] = jax.nn.sigmoid(ga) * ya
    sgb_ref[...] = jax.nn.sigmoid(gb)


def _inproj(x, w, *, sample, seq_len, state=None):
    t, d_model = x.shape
    d_conv = w['wco'].shape[0]
    rows = INPROJ_ROWS
    tiles_per_seq = max(seq_len // rows, 1)
    n_tiles = t // rows
    row_spec = lambda width: pl.BlockSpec((rows, width), lambda i: (i, 0))
    weights = [w['wm'], w['wf'], w['wg'], w['bf'], w['cw'], w['wco']]
    in_specs = [row_spec(d_model)] + [_const_spec(a.shape) for a in weights]
    args = [x] + weights
    if sample:
        args += [state[0], state[1]]
        in_specs += [row_spec(d_conv), row_spec(d_conv)]
        out_shape = [
            jax.ShapeDtypeStruct((t, D_ATTN), BF16),
            jax.ShapeDtypeStruct((t, D_ATTN), F32),
            jax.ShapeDtypeStruct((t, D_ATTN), F32),
            jax.ShapeDtypeStruct((t, N_HEADS), F32),
            jax.ShapeDtypeStruct((t, d_model), F32),
            jax.ShapeDtypeStruct((t, d_model), F32),
            jax.ShapeDtypeStruct((t, d_conv), F32),
        ]
        out_specs = [row_spec(D_ATTN), row_spec(D_ATTN), row_spec(D_ATTN), row_spec(N_HEADS),
                     row_spec(d_model), row_spec(d_model), row_spec(d_conv)]
        scratch = [pltpu.VMEM((rows + SUBLANES, d_conv), F32)]
    else:
        consts = [w['tri'], w['pq'], w['pk'], w['oq'], w['ok']]
        args += consts
        in_specs += [_const_spec(a.shape) for a in consts]
        n_seq = t // seq_len
        out_shape = [
            jax.ShapeDtypeStruct((t, N_HEADS * LANES), BF16),
            jax.ShapeDtypeStruct((n_seq, D_ATTN, seq_len), F32),
            jax.ShapeDtypeStruct((n_seq, D_ATTN, seq_len), F32),
            jax.ShapeDtypeStruct((t, N_HEADS * LANES), BF16),
            jax.ShapeDtypeStruct((n_tiles, N_HEADS * V_ROWS, rows), BF16),
            jax.ShapeDtypeStruct((n_seq, N_HEADS, seq_len), F32),
            jax.ShapeDtypeStruct((t, d_model), F32),
            jax.ShapeDtypeStruct((t, d_model), F32),
            jax.ShapeDtypeStruct((n_seq, SUBLANES, d_conv), F32),
        ]
        pos_minor = lambda height: pl.BlockSpec((1, height, rows),
                                                lambda i: (i // tiles_per_seq, 0, i % tiles_per_seq))
        out_specs = [row_spec(N_HEADS * LANES), pos_minor(D_ATTN), pos_minor(D_ATTN), row_spec(N_HEADS * LANES),
                     pl.BlockSpec((1, N_HEADS * V_ROWS, rows), lambda i: (i, 0, 0)), pos_minor(N_HEADS),
                     row_spec(d_model), row_spec(d_model),
                     pl.BlockSpec((1, SUBLANES, d_conv), lambda i: (i // tiles_per_seq, 0, 0))]
        scratch = [pltpu.VMEM((rows + SUBLANES, d_conv), F32), pltpu.VMEM((1, LANES), F32)]
    kern = functools.partial(_inproj_kernel, rows=rows, tiles_per_seq=tiles_per_seq, sample=sample,
                             d_conv=d_conv, d_model=d_model)
    return pl.pallas_call(
        kern,
        grid=(n_tiles,),
        in_specs=in_specs,
        out_specs=out_specs,
        out_shape=out_shape,
        scratch_shapes=scratch,
        compiler_params=pltpu.CompilerParams(dimension_semantics=("arbitrary",),
                                             vmem_limit_bytes=VMEM_LIMIT_BYTES),
        name="inproj_sample" if sample else "inproj_prompt",
    )(*args)


def _prompt_tile(i, q_ref, k_ref, vt_ref, o_ref, m_sc, acc_sc, *, tq, v_chunk, q_chunk):
    chunks = tq // v_chunk
    for hl in range(2):
        m_sc[hl] = jnp.full((1, tq), MASK_VALUE, F32)
        acc_sc[hl] = jnp.zeros((V_ROWS, tq), F32)

    def step(j, masked):
        start = pl.multiple_of(j * tq, tq)
        blocks = [(hl, qc) for hl in range(2) for qc in range(tq // q_chunk)]
        m_old = [m_sc[hl] for hl in range(2)]
        acc_old = [acc_sc[hl] for hl in range(2)]
        sts = [_dot_nt(k_ref[pl.ds(start, tq), hl * LANES:(hl + 1) * LANES],
                       q_ref[qc * q_chunk:(qc + 1) * q_chunk, hl * LANES:(hl + 1) * LANES]) for hl, qc in blocks]
        m_out, acc_out = [[], []], [[], []]
        for (hl, qc), st in zip(blocks, sts):
            qs = slice(qc * q_chunk, (qc + 1) * q_chunk)
            if masked:
                key = lax.broadcasted_iota(jnp.int32, (tq, q_chunk), 0)
                qry = lax.broadcasted_iota(jnp.int32, (tq, q_chunk), 1) + qc * q_chunk
                st = jnp.where(key <= qry, st, MASK_VALUE)
            m_prev = m_old[hl][:, qs]
            m_new = jnp.maximum(m_prev, jnp.max(st, axis=0, keepdims=True))
            alpha = jnp.exp(m_prev - m_new)
            pb = jnp.exp(st - m_new).astype(BF16)
            acc = alpha * acc_old[hl][:, qs]
            for c in range(chunks):
                vt = vt_ref[j * chunks + c, hl * V_ROWS:(hl + 1) * V_ROWS, :]
                acc = acc + _dot(vt, pb[c * v_chunk:(c + 1) * v_chunk, :])
            acc_out[hl].append(acc)
            m_out[hl].append(m_new)
        for hl in range(2):
            m_sc[hl] = jnp.concatenate(m_out[hl], axis=1)
            acc_sc[hl] = jnp.concatenate(acc_out[hl], axis=1)

    def body(j, carry):
        step(j, False)
        return carry

    lax.fori_loop(0, i, body, 0)
    step(i, True)
    ot = jnp.concatenate([acc_sc[hl, 0:HEAD_DIM, :] / acc_sc[hl, HEAD_DIM:HEAD_DIM + 1, :] for hl in range(2)],
                         axis=0)
    o_ref[...] = ot.T.astype(BF16)


def _sample_sequence(slot, q_ref, kn_ref, vn_ref, fn_ref, e_ref, suffix_ref, later_ref, tri_ref, o_ref,
                     k_buf, v_buf, f_buf, pad_sc, kn_sc, vn_sc, *, n_pages, n_new):
    n_rows = N_HEADS * n_new
    qb = q_ref[0]
    e = e_ref[...]

    pad_sc[...] = jnp.zeros((PAGE_SIZE, LANES), F32)
    pad_sc[0:n_new, 0:N_HEADS] = fn_ref[0]
    kn_sc[...] = jnp.zeros((PAGE_SIZE, D_ATTN), F32)
    kn_sc[0:n_new, :] = kn_ref[0]
    vn_sc[...] = jnp.zeros((PAGE_SIZE, D_ATTN), F32)
    vn_sc[0:n_new, :] = vn_ref[0]
    cn = _dot_exact_rhs(tri_ref[...], pad_sc[...])
    cn_rows = _dot_nt_exact_rhs(e, cn)
    row_t = lax.broadcasted_iota(jnp.int32, (n_rows, LANES), 0) // N_HEADS
    col = lax.broadcasted_iota(jnp.int32, (n_rows, LANES), 1)
    cnq = jnp.sum(jnp.where(col == row_t, cn_rows, 0.0), axis=1, keepdims=True)
    s_new = _dot_nt(qb, kn_sc[...].astype(BF16)) + (cnq - cn_rows)
    s_new = jnp.where(col <= row_t, s_new, MASK_VALUE)

    lf = f_buf[slot].reshape(n_pages * N_HEADS, PAGE_SIZE)
    incl = _dot_exact_lhs(lf, suffix_ref[...])
    tot = jnp.broadcast_to(jnp.sum(lf, axis=1, keepdims=True), lf.shape)
    d = incl - lf + _dot_exact_rhs(later_ref[...], tot)
    s_pages = []
    for j in range(n_pages):
        bias = jnp.concatenate([d[j * N_HEADS:(j + 1) * N_HEADS, :]] * n_new, axis=0)
        s_pages.append(_dot(qb, k_buf[slot, j].astype(BF16)) + bias + cnq)

    m_blk = s_new
    for s in s_pages:
        m_blk = jnp.maximum(m_blk, s)
    m = jnp.max(m_blk, axis=1, keepdims=True)
    p_new = jnp.exp(s_new - m)
    l_blk = p_new
    acc = _dot(p_new.astype(BF16), vn_sc[...].astype(BF16))
    for j in range(n_pages):
        p = jnp.exp(s_pages[j] - m)
        l_blk = l_blk + p
        acc = acc + _dot_nt(p.astype(BF16), v_buf[slot, j].astype(BF16))
    o_ref[0] = acc / jnp.sum(l_blk, axis=1, keepdims=True)


def _attention_kernel(pt_ref, q_ref, k_ref, vt_ref, qs_ref, kn_ref, vn_ref, fn_ref, e_ref, suffix_ref, later_ref,
                      tri_ref, kt_hbm, vtc_hbm, lft_hbm, o_ref, os_ref, m_sc, acc_sc, k_buf, v_buf, f_buf, sems,
                      pad_sc, kn_sc, vn_sc, *, tq, v_chunk, q_chunk, n_pages, n_new):
    step = (pl.program_id(0) * pl.num_programs(1) + pl.program_id(1)) * pl.num_programs(2) + pl.program_id(2)
    n_steps = pl.num_programs(0) * pl.num_programs(1) * pl.num_programs(2)
    slot = step % 2
    pools = ((kt_hbm, k_buf), (vtc_hbm, v_buf), (lft_hbm, f_buf))

    def start_pages(seq, to_slot):
        for j in range(n_pages):
            page = pt_ref[seq * n_pages + j]
            for a, (hbm, buf) in enumerate(pools):
                pltpu.make_async_copy(hbm.at[page], buf.at[to_slot, j], sems.at[a, to_slot]).start()

    @pl.when(step == 0)
    def _():
        start_pages(0, 0)

    @pl.when(step + 1 < n_steps)
    def _():
        start_pages(step + 1, 1 - slot)

    _prompt_tile(pl.program_id(2), q_ref, k_ref, vt_ref, o_ref, m_sc, acc_sc, tq=tq, v_chunk=v_chunk,
                 q_chunk=q_chunk)

    for a, (hbm, buf) in enumerate(pools):
        pltpu.make_async_copy(hbm.at[pl.ds(0, n_pages)], buf.at[slot], sems.at[a, slot]).wait()
    _sample_sequence(slot, qs_ref, kn_ref, vn_ref, fn_ref, e_ref, suffix_ref, later_ref, tri_ref, os_ref,
                     k_buf, v_buf, f_buf, pad_sc, kn_sc, vn_sc, n_pages=n_pages, n_new=n_new)


def _attention(qa, ka, vtb, qblk, cache_kt, cache_vt, cache_lft, page_table, k_new, v_new, lf_new,
               *, n_prompt_seq, seq_len):
    t = qa.shape[0]
    v_chunk = vtb.shape[2]
    tq = ATTN_TILE
    qt = seq_len // tq
    n_pairs = N_HEADS // 2
    n_seq, n_rows, _ = qblk.shape
    assert n_seq == n_prompt_seq * n_pairs * qt, "one sample sequence per prompt attention tile"
    n_new = k_new.shape[1]
    n_pages = page_table.shape[1]
    pt = page_table.reshape(-1)
    row_head = jnp.arange(n_rows) % N_HEADS
    e = (row_head[:, None] == jnp.arange(LANES)[None, :]).astype(BF16)
    pos = jnp.arange(PAGE_SIZE)
    suffix = (pos[:, None] >= pos[None, :]).astype(BF16)
    tri = (pos[None, :] <= pos[:, None]).astype(BF16)
    r = jnp.arange(n_pages * N_HEADS)
    later = ((r[:, None] % N_HEADS == r[None, :] % N_HEADS)
             & (r[None, :] // N_HEADS > r[:, None] // N_HEADS)).astype(BF16)

    tile_q = lambda n, hp, i, pt_ref: (n * qt + i, hp)
    seq_map = lambda n, hp, i, pt_ref: ((n * n_pairs + hp) * qt + i, 0, 0)
    const_map = lambda n, hp, i, pt_ref: (0, 0)
    any_spec = pl.BlockSpec(memory_space=pl.ANY)
    in_specs = [pl.BlockSpec((tq, 2 * LANES), tile_q),
                pl.BlockSpec((seq_len, 2 * LANES), lambda n, hp, i, pt_ref: (n, hp)),
                pl.BlockSpec((seq_len // v_chunk, 2 * V_ROWS, v_chunk), lambda n, hp, i, pt_ref: (n, hp, 0)),
                pl.BlockSpec((1, n_rows, D_ATTN), seq_map),
                pl.BlockSpec((1, n_new, D_ATTN), seq_map), pl.BlockSpec((1, n_new, D_ATTN), seq_map),
                pl.BlockSpec((1, n_new, N_HEADS), seq_map),
                pl.BlockSpec(e.shape, const_map), pl.BlockSpec(suffix.shape, const_map),
                pl.BlockSpec(later.shape, const_map), pl.BlockSpec(tri.shape, const_map),
                any_spec, any_spec, any_spec]
    grid_spec = pltpu.PrefetchScalarGridSpec(
        num_scalar_prefetch=1,
        grid=(n_prompt_seq, n_pairs, qt),
        in_specs=in_specs,
        out_specs=[pl.BlockSpec((tq, LANES), tile_q), pl.BlockSpec((1, n_rows, D_ATTN), seq_map)],
        scratch_shapes=[pltpu.VMEM((2, 1, tq), F32), pltpu.VMEM((2, V_ROWS, tq), F32),
                        pltpu.VMEM((2, n_pages, D_ATTN, PAGE_SIZE), F32),
                        pltpu.VMEM((2, n_pages, D_ATTN, PAGE_SIZE), F32),
                        pltpu.VMEM((2, n_pages, N_HEADS, PAGE_SIZE), F32),
                        pltpu.SemaphoreType.DMA((3, 2)),
                        pltpu.VMEM((PAGE_SIZE, LANES), F32),
                        pltpu.VMEM((PAGE_SIZE, D_ATTN), F32), pltpu.VMEM((PAGE_SIZE, D_ATTN), F32)],
    )
    args = [pt, qa, ka, vtb, qblk, k_new, v_new, lf_new, e, suffix, later, tri, cache_kt, cache_vt, cache_lft]
    return pl.pallas_call(
        functools.partial(_attention_kernel, tq=tq, v_chunk=v_chunk, q_chunk=LANES, n_pages=n_pages, n_new=n_new),
        grid_spec=grid_spec,
        out_shape=[jax.ShapeDtypeStruct((t, D_ATTN), BF16), jax.ShapeDtypeStruct((n_seq, n_rows, D_ATTN), F32)],
        compiler_params=pltpu.CompilerParams(dimension_semantics=("arbitrary", "arbitrary", "arbitrary"),
                                             vmem_limit_bytes=VMEM_LIMIT_BYTES),
        name="attention",
    )(*args)


def _route(logits):
    rows = logits.shape[0]
    lane = lax.broadcasted_iota(jnp.int32, (rows, LANES), 1).astype(F32)
    big = float(LANES)
    in_g = lane < N_GROUPS
    gl = jnp.where(in_g, logits, MASK_VALUE)
    m_g = jnp.max(gl, axis=1, keepdims=True)
    z_g = jnp.sum(jnp.exp(gl - m_g), axis=1, keepdims=True)
    p_g = 1.0 / z_g
    g_idx = jnp.min(jnp.where(in_g & (gl == m_g), lane, big), axis=1, keepdims=True)
    lo = N_GROUPS + EXPERTS_PER_GROUP * g_idx
    sel = (lane >= lo) & (lane < lo + EXPERTS_PER_GROUP)
    el = jnp.where(sel, logits, MASK_VALUE)
    m_e = jnp.max(el, axis=1, keepdims=True)
    ex = jnp.exp(el - m_e)
    ep = ex / jnp.sum(ex, axis=1, keepdims=True)
    v1 = jnp.max(jnp.where(sel, ep, -1.0), axis=1, keepdims=True)
    i1 = jnp.min(jnp.where(sel & (ep == v1), lane, big), axis=1, keepdims=True)
    sel2 = sel & (lane != i1)
    v2 = jnp.max(jnp.where(sel2, ep, -1.0), axis=1, keepdims=True)
    i2 = jnp.min(jnp.where(sel2 & (ep == v2), lane, big), axis=1, keepdims=True)
    denom = v1 + v2
    w1 = p_g * (v1 / denom)
    w2 = p_g * (v2 / denom)
    gate4 = jnp.where(lane == i1 - lo, w1, 0.0) + jnp.where(lane == i2 - lo, w2, 0.0)
    return gate4, g_idx


def _post_kernel(o_ref, ap_ref, sgb_ref, x_ref, wao_ref, wo_ref, g_ref, b_ref, wr_ref, tri_ref,
                 x1e_ref, route_t_ref, totals_ref, count_sc, *, alpha, d_model):
    i = pl.program_id(0)
    rows = x_ref.shape[0]
    yb = _dot(o_ref[...], wao_ref[...])
    mix = _dot((ap_ref[...] + sgb_ref[...] * yb).astype(BF16), wo_ref[...])
    x1 = _layer_norm(alpha * x_ref[...] + mix, g_ref[...], b_ref[...])
    xh = x1.astype(BF16)
    xl = (x1 - xh.astype(F32)).astype(BF16)
    hi_lo = _dot(xh, wr_ref[...])
    logits = hi_lo[:, 0:LANES] + hi_lo[:, LANES:2 * LANES] + _dot(xl, wr_ref[:, 0:LANES])
    gate4, g_idx = _route(logits)

    @pl.when(i == 0)
    def _():
        count_sc[...] = jnp.zeros((1, LANES), F32)
    lane = lax.broadcasted_iota(jnp.int32, (rows, LANES), 1).astype(F32)
    onehot = jnp.where(lane == g_idx, 1.0, 0.0)
    count = _dot(tri_ref[...], onehot.astype(BF16)) + count_sc[...]
    rank = jnp.sum(onehot * count, axis=1, keepdims=True) - 1.0
    count_sc[...] = count[rows - 1:rows, :]
    totals_ref[...] = jnp.broadcast_to(count[rows - 1:rows, :], (SUBLANES, LANES))
    route = (gate4 + jnp.where(lane == ROUTE_GROUP_LANE, g_idx, 0.0)
             + jnp.where(lane == ROUTE_RANK_LANE, rank, 0.0))
    route_t_ref[...] = route.T[0:SUBLANES, :]
    x1e_ref[:, 0:d_model] = x1
    x1e_ref[:, d_model:d_model + LANES] = route


def _post_attention(o, ap, sgb, x, w, *, alpha):
    t, d_model = x.shape
    rows = MOE_ROWS
    row_spec = lambda width: pl.BlockSpec((rows, width), lambda i: (i, 0))
    weights = [w['wao'], w['wo'], w['ln1_g'], w['ln1_b'], w['wr'], w['tri_moe']]
    return pl.pallas_call(
        functools.partial(_post_kernel, alpha=alpha, d_model=d_model),
        grid=(t // rows,),
        in_specs=[row_spec(D_ATTN), row_spec(d_model), row_spec(d_model), row_spec(d_model)]
                 + [_const_spec(a.shape) for a in weights],
        out_specs=[row_spec(d_model + LANES), pl.BlockSpec((SUBLANES, rows), lambda i: (0, i)),
                   pl.BlockSpec((SUBLANES, LANES), lambda i: (0, 0))],
        out_shape=[jax.ShapeDtypeStruct((t, d_model + LANES), F32),
                   jax.ShapeDtypeStruct((SUBLANES, t), F32),
                   jax.ShapeDtypeStruct((SUBLANES, LANES), F32)],
        scratch_shapes=[pltpu.VMEM((1, LANES), F32)],
        compiler_params=pltpu.CompilerParams(dimension_semantics=("arbitrary",),
                                             vmem_limit_bytes=VMEM_LIMIT_BYTES),
        name="post_attn",
    )(o, ap, sgb, x, *weights)


def _row_permute_kernel(pos_ref, *refs, rows, scatter):
    if scatter:
        fill_ref, src_ref, dst_ref, zero_sc, sem, fill_sem = refs

        @pl.when(pl.program_id(0) == 0)
        def _():
            zero_sc[...] = jnp.zeros_like(zero_sc)
            for g in range(fill_ref.shape[0]):
                fill = pltpu.make_async_copy(zero_sc, dst_ref.at[pl.ds(fill_ref[g] * rows, rows)], fill_sem)
                fill.start()
                fill.wait()
    else:
        src_ref, dst_ref, sem = refs
    base = pl.program_id(0) * rows

    def issue(r, carry):
        p = pos_ref[base + r]
        if scatter:
            pltpu.make_async_copy(src_ref.at[pl.ds(r, 1)], dst_ref.at[pl.ds(p, 1)], sem).start()
        else:
            pltpu.make_async_copy(src_ref.at[pl.ds(p, 1)], dst_ref.at[pl.ds(r, 1)], sem).start()
        return carry

    lax.fori_loop(0, rows, issue, 0, unroll=True)
    if scatter:
        pltpu.make_async_copy(src_ref, dst_ref.at[pl.ds(0, rows)], sem).wait()
    else:
        pltpu.make_async_copy(src_ref.at[pl.ds(0, rows)], dst_ref, sem).wait()


def _row_permute(pos, src, *, n_out, scatter, fill_tiles=None):
    n_rows = pos.shape[0]
    width = src.shape[1]
    rows = MOE_ROWS
    any_spec = pl.BlockSpec(memory_space=pl.ANY)
    tile_spec = pl.BlockSpec((rows, width), lambda i, *_: (i, 0))
    if scatter:
        args = [pos, fill_tiles, src]
        in_specs, out_specs = [tile_spec], any_spec
        scratch = [pltpu.VMEM((rows, width), src.dtype), pltpu.SemaphoreType.DMA(()), pltpu.SemaphoreType.DMA(())]
    else:
        args = [pos, src]
        in_specs, out_specs = [any_spec], tile_spec
        scratch = [pltpu.SemaphoreType.DMA(())]
    grid_spec = pltpu.PrefetchScalarGridSpec(
        num_scalar_prefetch=len(args) - 1, grid=(n_rows // rows,), in_specs=in_specs, out_specs=out_specs,
        scratch_shapes=scratch)
    return pl.pallas_call(
        functools.partial(_row_permute_kernel, rows=rows, scatter=scatter),
        grid_spec=grid_spec,
        out_shape=jax.ShapeDtypeStruct((n_out, width), src.dtype),
        compiler_params=pltpu.CompilerParams(dimension_semantics=("arbitrary",)),
        name="row_scatter" if scatter else "row_gather",
    )(*args)


def _moe_kernel(tg_ref, nv_ref, xs_ref, w1_ref, w3_ref, w2_ref, g_ref, b_ref, y_ref, *, alpha, d_model):
    del tg_ref
    i = pl.program_id(0)

    @pl.when(i < nv_ref[0])
    def _():
        x = xs_ref[:, 0:d_model]
        route = xs_ref[:, d_model:d_model + LANES]
        xb = x.astype(BF16)
        lane = lax.broadcasted_iota(jnp.int32, route.shape, 1)
        acc = jnp.zeros_like(x)
        for e in range(EXPERTS_PER_GROUP):
            h1 = _dot(xb, w1_ref[0, e])
            h3 = _dot(xb, w3_ref[0, e])
            h = (h1 * jax.nn.sigmoid(h1)) * h3
            ye = _dot(h.astype(BF16), w2_ref[0, e])
            ge = jnp.sum(jnp.where(lane == e, route, 0.0), axis=1, keepdims=True)
            acc = acc + ge * ye
        y_ref[...] = _layer_norm(alpha * x + acc, g_ref[...], b_ref[...])

    @pl.when(i >= nv_ref[0])
    def _():
        y_ref[...] = jnp.zeros_like(y_ref)


def _experts(x1e, route_t, totals, w, *, alpha):
    t = x1e.shape[0]
    d_model = x1e.shape[1] - LANES
    _, _, _, d_exp = w['w1'].shape
    rows = MOE_ROWS
    n_tiles = t // rows + N_GROUPS
    group = route_t[ROUTE_GROUP_LANE].astype(jnp.int32)
    rank = route_t[ROUTE_RANK_LANE].astype(jnp.int32)
    count = totals[0, 0:N_GROUPS].astype(jnp.int32)
    tiles_g = (count + rows - 1) // rows
    tile_end = jnp.cumsum(tiles_g)
    pos = (tile_end - tiles_g)[group] * rows + rank
    n_valid = tile_end[N_GROUPS - 1:N_GROUPS]
    tile_group = jnp.minimum(jnp.sum(jnp.arange(n_tiles)[:, None] >= tile_end[None, :], axis=1),
                             N_GROUPS - 1).astype(jnp.int32)

    fill_tiles = jnp.concatenate([jnp.maximum(tile_end - 1, 0),
                                  jnp.minimum(n_valid + jnp.arange(N_GROUPS), n_tiles - 1)]).astype(jnp.int32)
    xs = _row_permute(pos, x1e, n_out=n_tiles * rows, scatter=True, fill_tiles=fill_tiles)

    live = lambda i, nv: jnp.minimum(i, nv[0] - 1)
    w_spec = lambda a: pl.BlockSpec((1,) + a.shape[1:], lambda i, tg, nv: (tg[live(i, nv)], 0, 0, 0))
    grid_spec = pltpu.PrefetchScalarGridSpec(
        num_scalar_prefetch=2,
        grid=(n_tiles,),
        in_specs=[
            pl.BlockSpec((rows, d_model + LANES), lambda i, tg, nv: (live(i, nv), 0)),
            w_spec(w['w1']), w_spec(w['w3']), w_spec(w['w2']),
            pl.BlockSpec((1, d_model), lambda i, tg, nv: (0, 0)),
            pl.BlockSpec((1, d_model), lambda i, tg, nv: (0, 0)),
        ],
        out_specs=pl.BlockSpec((rows, d_model), lambda i, tg, nv: (i, 0)),
    )
    ys = pl.pallas_call(
        functools.partial(_moe_kernel, alpha=alpha, d_model=d_model),
        grid_spec=grid_spec,
        out_shape=jax.ShapeDtypeStruct((n_tiles * rows, d_model), F32),
        compiler_params=pltpu.CompilerParams(dimension_semantics=("arbitrary",),
                                             vmem_limit_bytes=VMEM_LIMIT_BYTES),
        name="experts",
    )(tile_group, n_valid, xs, w['w1'], w['w3'], w['w2'], w['ln2_g'], w['ln2_b'])
    return _row_permute(pos, ys, n_out=t, scatter=False)


def _prepare_weights(w_in, b_f, conv_w, w_conv_out, w_attn_out, w_o, ln1_g, ln1_b,
                     w_group, w_router, w1, w3, w2, ln2_g, ln2_b):
    d_conv = w_conv_out.shape[0]
    o_q = 3 * d_conv
    o_f = o_q + 3 * D_ATTN
    o_g = o_f + N_HEADS
    w_bf = w_in.astype(BF16)
    wm = w_bf[:, 0:o_f]
    wf = jnp.pad(w_bf[:, o_f:o_g], ((0, 0), (0, LANES - N_HEADS)))
    wg = w_bf[:, o_g:]
    bf = jnp.pad(b_f.astype(F32), (0, LANES - N_HEADS)).reshape(1, LANES)

    col = jnp.arange(N_HEADS * LANES)
    col_head, col_lane = col // LANES, col % LANES
    part = jnp.arange(LANES)
    part_k, part_h = part // N_HEADS, part % N_HEADS
    live = (part_k[:, None] < AUG_SPLITS) & (part_h[:, None] == col_head[None, :])
    pq = live & (col_lane[None, :] == AUG_Q_C + part_k[:, None])
    pk = live & (col_lane[None, :] == AUG_Q_C + AUG_SPLITS + part_k[:, None])
    oq = ((col_lane >= AUG_Q_C + AUG_SPLITS) & (col_lane < AUG_Q_C + 2 * AUG_SPLITS)).astype(F32).reshape(1, -1)
    ok = ((col_lane >= AUG_Q_C) & (col_lane < AUG_Q_C + AUG_SPLITS)).astype(F32).reshape(1, -1)
    r = jnp.arange(INPROJ_ROWS)
    tri = (r[None, :] <= r[:, None]).astype(BF16)
    r = jnp.arange(MOE_ROWS)
    tri_moe = (r[None, :] <= r[:, None]).astype(BF16)

    wr = jnp.pad(jnp.concatenate([w_group, w_router], axis=1), ((0, 0), (0, LANES - N_GROUPS - N_EXPERTS)))
    wr_hi = wr.astype(BF16)
    wr_lo = (wr - wr_hi.astype(F32)).astype(BF16)
    wr = jnp.concatenate([wr_hi, wr_lo], axis=1)
    return dict(
        wm=wm, wf=wf, wg=wg, bf=bf, cw=conv_w.astype(F32), wco=w_conv_out.astype(BF16),
        tri=tri, pq=pq.astype(BF16), pk=pk.astype(BF16), oq=oq, ok=ok,
        wao=w_attn_out.astype(BF16), wo=w_o.astype(BF16),
        ln1_g=ln1_g.reshape(1, -1), ln1_b=ln1_b.reshape(1, -1), wr=wr,
        tri_moe=tri_moe,
        w1=w1.astype(BF16).reshape((N_GROUPS, EXPERTS_PER_GROUP) + w1.shape[1:]),
        w3=w3.astype(BF16).reshape((N_GROUPS, EXPERTS_PER_GROUP) + w3.shape[1:]),
        w2=w2.astype(BF16).reshape((N_GROUPS, EXPERTS_PER_GROUP) + w2.shape[1:]),
        ln2_g=ln2_g.reshape(1, -1), ln2_b=ln2_b.reshape(1, -1),
    )


def kernel(x_prompt, x_sample, cache_k, cache_v, cache_logf, state_conv, page_table, w_in, b_f, conv_w,
           w_conv_out, w_attn_out, w_o, ln1_g, ln1_b, w_group, w_router, w1, w3, w2, ln2_g, ln2_b):
    depth = w_in.shape[0]
    assert depth == 1, "single-layer stack"
    n_p, s_p, d_model = x_prompt.shape
    n_s, s_s, _ = x_sample.shape
    d_conv = w_conv_out.shape[1]
    alpha = (2 * depth) ** 0.25
    w = _prepare_weights(w_in[0], b_f[0], conv_w[0], w_conv_out[0], w_attn_out[0], w_o[0], ln1_g[0], ln1_b[0],
                         w_group[0], w_router[0], w1[0], w3[0], w2[0], ln2_g[0], ln2_b[0])

    xp = x_prompt.reshape(n_p * s_p, d_model)
    qa, kt_p, vt_p, ka, vtb, lft_p, ap, sgb, tail = _inproj(xp, w, sample=False, seq_len=s_p)
    xs = x_sample.reshape(n_s * s_s, d_model)
    state = state_conv[0]
    s0 = jnp.repeat(state[:, 0, :], s_s, axis=0)
    s1 = jnp.repeat(state[:, 1, :], s_s, axis=0)
    q_s, k_s, v_s, lf_s, ap_s, sgb_s, u_s = _inproj(xs, w, sample=True, seq_len=s_s, state=(s0, s1))
    q5 = q_s.reshape(n_s, s_s, N_HEADS, HEAD_DIM)
    eye_h = jnp.eye(N_HEADS, dtype=BF16)
    qblk = (q5[:, :, :, None, :] * eye_h[None, None, :, :, None]).reshape(n_s, s_s * N_HEADS, D_ATTN)
    n_pool = cache_k.shape[1]
    cache_kt = cache_k[0].transpose(0, 2, 3, 1).reshape(n_pool, D_ATTN, PAGE_SIZE)
    cache_vt = cache_v[0].transpose(0, 2, 3, 1).reshape(n_pool, D_ATTN, PAGE_SIZE)
    cache_lft = cache_logf[0].transpose(0, 2, 1)
    o_p, o_blk = _attention(
        qa, ka, vtb, qblk, cache_kt, cache_vt, cache_lft, page_table, k_s.reshape(n_s, s_s, D_ATTN),
        v_s.reshape(n_s, s_s, D_ATTN), lf_s.reshape(n_s, s_s, N_HEADS), n_prompt_seq=n_p, seq_len=s_p)
    x1e_p, route_p, totals_p = _post_attention(o_p, ap, sgb, xp, w, alpha=alpha)
    y_p = _experts(x1e_p, route_p, totals_p, w, alpha=alpha)
    o5 = o_blk.reshape(n_s, s_s, N_HEADS, N_HEADS, HEAD_DIM)
    o_s = jnp.einsum('bthgd,hg->bthd', o5, jnp.eye(N_HEADS, dtype=F32)).reshape(n_s * s_s, D_ATTN).astype(BF16)
    x1e_s, route_s, totals_s = _post_attention(o_s, ap_s, sgb_s, xs, w, alpha=alpha)
    y_s = _experts(x1e_s, route_s, totals_s, w, alpha=alpha)

    return (
        y_p.reshape(n_p, s_p, d_model),
        y_s.reshape(n_s, s_s, d_model),
        kt_p.reshape(1, n_p, N_HEADS, HEAD_DIM, s_p).transpose(0, 1, 4, 2, 3),
        vt_p.reshape(1, n_p, N_HEADS, HEAD_DIM, s_p).transpose(0, 1, 4, 2, 3),
        lft_p.reshape(1, n_p, N_HEADS, s_p).transpose(0, 1, 3, 2),
        tail[:, SUBLANES - (CONV_WIDTH - 1):, :].reshape(1, n_p, CONV_WIDTH - 1, d_conv),
        k_s.reshape(1, n_s, s_s, N_HEADS, HEAD_DIM),
        v_s.reshape(1, n_s, s_s, N_HEADS, HEAD_DIM),
        lf_s.reshape(1, n_s, s_s, N_HEADS),
        u_s.reshape(n_s, s_s, d_conv)[:, s_s - (CONV_WIDTH - 1):, :].reshape(1, n_s, CONV_WIDTH - 1, d_conv),
    )
```

```python
import functools

import jax
import jax.numpy as jnp
from jax import lax
from jax.experimental import pallas as pl
from jax.experimental.pallas import tpu as pltpu

F32 = jnp.float32
BF16 = jnp.bfloat16

LANES = 128
SUBLANES = 8
VMEM_LIMIT_BYTES = 56 * 1024 * 1024

CONV_WIDTH = 3
N_HEADS = 8
HEAD_DIM = 64
D_ATTN = N_HEADS * HEAD_DIM
N_GROUPS = 4
EXPERTS_PER_GROUP = 4
N_EXPERTS = N_GROUPS * EXPERTS_PER_GROUP
LN_EPS = 1e-5
PAGE_SIZE = 128
MASK_VALUE = -1e30
AUG_Q_C = HEAD_DIM
AUG_SPLITS = 3
INPROJ_ROWS = 256
ATTN_TILE = 512
MOE_ROWS = 512
ROUTE_GROUP_LANE = EXPERTS_PER_GROUP
ROUTE_RANK_LANE = EXPERTS_PER_GROUP + 1
V_PAD_ROWS = 16
V_ROWS = HEAD_DIM + V_PAD_ROWS


def _dot(a, b):
    return jnp.dot(a, b, preferred_element_type=F32)


def _dot_nt(a, b):
    return lax.dot_general(a, b, (((1,), (1,)), ((), ())), preferred_element_type=F32)


def _split3(x):
    h1 = x.astype(BF16)
    r1 = x - h1.astype(F32)
    h2 = r1.astype(BF16)
    r2 = r1 - h2.astype(F32)
    h3 = r2.astype(BF16)
    return h1, h2, h3


def _dot_exact_rhs(a_bf16, x):
    h1, h2, h3 = _split3(x)
    return _dot(a_bf16, h1) + _dot(a_bf16, h2) + _dot(a_bf16, h3)


def _dot_exact_lhs(x, b_bf16):
    h1, h2, h3 = _split3(x)
    return _dot(h1, b_bf16) + _dot(h2, b_bf16) + _dot(h3, b_bf16)


def _dot_nt_exact_rhs(a_bf16, x):
    h1, h2, h3 = _split3(x)
    return _dot_nt(a_bf16, h1) + _dot_nt(a_bf16, h2) + _dot_nt(a_bf16, h3)


def _log_sigmoid(x):
    return jnp.minimum(x, 0.0) - jnp.log1p(jnp.exp(-jnp.abs(x)))


def _layer_norm(r, g, b):
    mu = jnp.mean(r, axis=-1, keepdims=True)
    d = r - mu
    var = jnp.mean(d * d, axis=-1, keepdims=True)
    return d * lax.rsqrt(var + LN_EPS) * g + b


def _const_spec(shape):
    nd = len(shape)
    return pl.BlockSpec(shape, lambda *_: (0,) * nd, pipeline_mode=pl.Buffered(1))


def _inproj_kernel(*refs, rows, tiles_per_seq, sample, d_conv, d_model):
    if sample:
        (x_ref, wm_ref, wf_ref, wg_ref, bf_ref, cw_ref, wco_ref, s0_ref, s1_ref,
         q_ref, k_ref, v_ref, lf_ref, ap_ref, sgb_ref, u_ref, ubuf) = refs
    else:
        (x_ref, wm_ref, wf_ref, wg_ref, bf_ref, cw_ref, wco_ref, tri_ref, pq_ref, pk_ref,
         oq_ref, ok_ref,
         q_ref, k_ref, v_ref, ka_ref, vb_ref, lf_ref, ap_ref, sgb_ref, tail_ref, ubuf, ccarry) = refs
    i = pl.program_id(0)
    xb = x_ref[...].astype(BF16)

    xc = _dot(xb, wm_ref[:, 0:d_conv])
    bg = _dot(xb, wm_ref[:, d_conv:2 * d_conv])
    cg = _dot(xb, wm_ref[:, 2 * d_conv:3 * d_conv])
    u = cg * xc
    if sample:
        ubuf[0:SUBLANES, :] = jnp.zeros((SUBLANES, d_conv), F32)
    else:
        @pl.when(i % tiles_per_seq == 0)
        def _():
            ubuf[0:SUBLANES, :] = jnp.zeros((SUBLANES, d_conv), F32)
    ubuf[SUBLANES:SUBLANES + rows, :] = u
    prev1 = ubuf[SUBLANES - 1:SUBLANES - 1 + rows, :]
    prev2 = ubuf[SUBLANES - 2:SUBLANES - 2 + rows, :]
    if sample:
        pos = lax.broadcasted_iota(jnp.int32, (rows, 1), 0) % 4
        prev1 = jnp.where(pos == 0, s1_ref[...], prev1)
        prev2 = jnp.where(pos == 0, s0_ref[...], jnp.where(pos == 1, s1_ref[...], prev2))
        u_ref[...] = u
    else:
        ubuf[0:SUBLANES, :] = u[rows - SUBLANES:rows, :]

        @pl.when(i % tiles_per_seq == tiles_per_seq - 1)
        def _():
            tail_ref[0] = u[rows - SUBLANES:rows, :]
    conv = cw_ref[0:1, :] * prev2 + cw_ref[1:2, :] * prev1 + cw_ref[2:3, :] * u
    ya = _dot((bg * conv).astype(BF16), wco_ref[...])

    q = _dot(xb, wm_ref[:, 3 * d_conv:3 * d_conv + D_ATTN]) * (HEAD_DIM ** -0.5)
    k = _dot(xb, wm_ref[:, 3 * d_conv + D_ATTN:3 * d_conv + 2 * D_ATTN])
    v = _dot(xb, wm_ref[:, 3 * d_conv + 2 * D_ATTN:3 * d_conv + 3 * D_ATTN])
    fg = _dot(xb, wf_ref[...]) + bf_ref[...]
    lane = lax.broadcasted_iota(jnp.int32, (rows, LANES), 1)
    logf = jnp.where(lane < N_HEADS, _log_sigmoid(fg), 0.0)
    if sample:
        k_ref[...] = k
        v_ref[...] = v
        lf_ref[...] = logf[:, 0:N_HEADS]
        q_ref[...] = q.astype(BF16)
    else:
        k_ref[0] = k.T
        vt = v.T
        v_ref[0] = vt
        lf_ref[0] = logf.T[0:N_HEADS, :]
        ones_blk = jnp.where(lax.broadcasted_iota(jnp.int32, (V_PAD_ROWS, rows), 0) == 0, 1.0, 0.0)
        vb_ref[0] = jnp.concatenate(
            [blk for h in range(N_HEADS) for blk in (vt[h * HEAD_DIM:(h + 1) * HEAD_DIM, :], ones_blk)],
            axis=0).astype(BF16)

        @pl.when(i % tiles_per_seq == 0)
        def _():
            ccarry[...] = jnp.zeros((1, LANES), F32)
        c = _dot_exact_rhs(tri_ref[...], logf) + ccarry[...]
        ccarry[...] = c[rows - 1:rows, :]
        c1, c2, c3 = _split3(c)
        c_parts = (c1.astype(F32) + pltpu.roll(c2.astype(F32), N_HEADS, 1)
                   + pltpu.roll(c3.astype(F32), 2 * N_HEADS, 1)).astype(BF16)
        aug_q = _dot(c_parts, pq_ref[...]) + oq_ref[...]
        aug_k = ok_ref[...] - _dot(c_parts, pk_ref[...])

        def spread_heads(x, aug):
            groups = []
            for h in range(N_HEADS):
                slab = x[:, (h // 2) * LANES:(h // 2 + 1) * LANES]
                if h % 2:
                    slab = pltpu.roll(slab, HEAD_DIM, 1)
                groups.append(jnp.where(lane < HEAD_DIM, slab, aug[:, h * LANES:(h + 1) * LANES]))
            return jnp.concatenate(groups, axis=1).astype(BF16)

        q_ref[...] = spread_heads(q, aug_q)
        ka_ref[...] = spread_heads(k, aug_k)

    ga = _dot(xb, wg_ref[:, 0:d_model])
    gb = _dot(xb, wg_ref[:, d_model:2 * d_model])
    ap_ref[...] = jax.nn.sigmoid(ga) * ya
    sgb_ref[...] = jax.nn.sigmoid(gb)


def _inproj(x, w, *, sample, seq_len, state=None):
    t, d_model = x.shape
    d_conv = w['wco'].shape[0]
    rows = INPROJ_ROWS
    tiles_per_seq = max(seq_len // rows, 1)
    n_tiles = t // rows
    row_spec = lambda width: pl.BlockSpec((rows, width), lambda i: (i, 0))
    weights = [w['wm'], w['wf'], w['wg'], w['bf'], w['cw'], w['wco']]
    in_specs = [row_spec(d_model)] + [_const_spec(a.shape) for a in weights]
    args = [x] + weights
    if sample:
        args += [state[0], state[1]]
        in_specs += [row_spec(d_conv), row_spec(d_conv)]
        out_shape = [
            jax.ShapeDtypeStruct((t, D_ATTN), BF16),
            jax.ShapeDtypeStruct((t, D_ATTN), F32),
            jax.ShapeDtypeStruct((t, D_ATTN), F32),
            jax.ShapeDtypeStruct((t, N_HEADS), F32),
            jax.ShapeDtypeStruct((t, d_model), F32),
            jax.ShapeDtypeStruct((t, d_model), F32),
            jax.ShapeDtypeStruct((t, d_conv), F32),
        ]
        out_specs = [row_spec(D_ATTN), row_spec(D_ATTN), row_spec(D_ATTN), row_spec(N_HEADS),
                     row_spec(d_model), row_spec(d_model), row_spec(d_conv)]
        scratch = [pltpu.VMEM((rows + SUBLANES, d_conv), F32)]
    else:
        consts = [w['tri'], w['pq'], w['pk'], w['oq'], w['ok']]
        args += consts
        in_specs += [_const_spec(a.shape) for a in consts]
        n_seq = t // seq_len
        out_shape = [
            jax.ShapeDtypeStruct((t, N_HEADS * LANES), BF16),
            jax.ShapeDtypeStruct((n_seq, D_ATTN, seq_len), F32),
            jax.ShapeDtypeStruct((n_seq, D_ATTN, seq_len), F32),
            jax.ShapeDtypeStruct((t, N_HEADS * LANES), BF16),
            jax.ShapeDtypeStruct((n_tiles, N_HEADS * V_ROWS, rows), BF16),
            jax.ShapeDtypeStruct((n_seq, N_HEADS, seq_len), F32),
            jax.ShapeDtypeStruct((t, d_model), F32),
            jax.ShapeDtypeStruct((t, d_model), F32),
            jax.ShapeDtypeStruct((n_seq, SUBLANES, d_conv), F32),
        ]
        pos_minor = lambda height: pl.BlockSpec((1, height, rows),
                                                lambda i: (i // tiles_per_seq, 0, i % tiles_per_seq))
        out_specs = [row_spec(N_HEADS * LANES), pos_minor(D_ATTN), pos_minor(D_ATTN), row_spec(N_HEADS * LANES),
                     pl.BlockSpec((1, N_HEADS * V_ROWS, rows), lambda i: (i, 0, 0)), pos_minor(N_HEADS),
                     row_spec(d_model), row_spec(d_model),
                     pl.BlockSpec((1, SUBLANES, d_conv), lambda i: (i // tiles_per_seq, 0, 0))]
        scratch = [pltpu.VMEM((rows + SUBLANES, d_conv), F32), pltpu.VMEM((1, LANES), F32)]
    kern = functools.partial(_inproj_kernel, rows=rows, tiles_per_seq=tiles_per_seq, sample=sample,
                             d_conv=d_conv, d_model=d_model)
    return pl.pallas_call(
        kern,
        grid=(n_tiles,),
        in_specs=in_specs,
        out_specs=out_specs,
        out_shape=out_shape,
        scratch_shapes=scratch,
        compiler_params=pltpu.CompilerParams(dimension_semantics=("arbitrary",),
                                             vmem_limit_bytes=VMEM_LIMIT_BYTES),
        name="inproj_sample" if sample else "inproj_prompt",
    )(*args)


def _prompt_tile(i, q_ref, k_ref, vt_ref, o_ref, m_sc, acc_sc, *, tq, v_chunk, q_chunk):
    chunks = tq // v_chunk
    for hl in range(2):
        m_sc[hl] = jnp.full((1, tq), MASK_VALUE, F32)
        acc_sc[hl] = jnp.zeros((V_ROWS, tq), F32)

    def step(j, masked):
        start = pl.multiple_of(j * tq, tq)
        blocks = [(hl, qc) for hl in range(2) for qc in range(tq // q_chunk)]
        m_old = [m_sc[hl] for hl in range(2)]
        acc_old = [acc_sc[hl] for hl in range(2)]
        sts = [_dot_nt(k_ref[pl.ds(start, tq), hl * LANES:(hl + 1) * LANES],
                       q_ref[qc * q_chunk:(qc + 1) * q_chunk, hl * LANES:(hl + 1) * LANES]) for hl, qc in blocks]
        m_out, acc_out = [[], []], [[], []]
        for (hl, qc), st in zip(blocks, sts):
            qs = slice(qc * q_chunk, (qc + 1) * q_chunk)
            if masked:
                key = lax.broadcasted_iota(jnp.int32, (tq, q_chunk), 0)
                qry = lax.broadcasted_iota(jnp.int32, (tq, q_chunk), 1) + qc * q_chunk
                st = jnp.where(key <= qry, st, MASK_VALUE)
            m_prev = m_old[hl][:, qs]
            m_new = jnp.maximum(m_prev, jnp.max(st, axis=0, keepdims=True))
            alpha = jnp.exp(m_prev - m_new)
            pb = jnp.exp(st - m_new).astype(BF16)
            acc = alpha * acc_old[hl][:, qs]
            for c in range(chunks):
                vt = vt_ref[j * chunks + c, hl * V_ROWS:(hl + 1) * V_ROWS, :]
                acc = acc + _dot(vt, pb[c * v_chunk:(c + 1) * v_chunk, :])
            acc_out[hl].append(acc)
            m_out[hl].append(m_new)
        for hl in range(2):
            m_sc[hl] = jnp.concatenate(m_out[hl], axis=1)
            acc_sc[hl] = jnp.concatenate(acc_out[hl], axis=1)

    def body(j, carry):
        step(j, False)
        return carry

    lax.fori_loop(0, i, body, 0)
    step(i, True)
    ot = jnp.concatenate([acc_sc[hl, 0:HEAD_DIM, :] / acc_sc[hl, HEAD_DIM:HEAD_DIM + 1, :] for hl in range(2)],
                         axis=0)
    o_ref[...] = ot.T.astype(BF16)


def _sample_sequence(slot, q_ref, kn_ref, vn_ref, fn_ref, e_ref, suffix_ref, later_ref, tri_ref, o_ref,
                     k_buf, v_buf, f_buf, pad_sc, kn_sc, vn_sc, *, n_pages, n_new):
    n_rows = N_HEADS * n_new
    qb = q_ref[0]
    e = e_ref[...]

    pad_sc[...] = jnp.zeros((PAGE_SIZE, LANES), F32)
    pad_sc[0:n_new, 0:N_HEADS] = fn_ref[0]
    kn_sc[...] = jnp.zeros((PAGE_SIZE, D_ATTN), F32)
    kn_sc[0:n_new, :] = kn_ref[0]
    vn_sc[...] = jnp.zeros((PAGE_SIZE, D_ATTN), F32)
    vn_sc[0:n_new, :] = vn_ref[0]
    cn = _dot_exact_rhs(tri_ref[...], pad_sc[...])
    cn_rows = _dot_nt_exact_rhs(e, cn)
    row_t = lax.broadcasted_iota(jnp.int32, (n_rows, LANES), 0) // N_HEADS
    col = lax.broadcasted_iota(jnp.int32, (n_rows, LANES), 1)
    cnq = jnp.sum(jnp.where(col == row_t, cn_rows, 0.0), axis=1, keepdims=True)
    s_new = _dot_nt(qb, kn_sc[...].astype(BF16)) + (cnq - cn_rows)
    s_new = jnp.where(col <= row_t, s_new, MASK_VALUE)

    lf = f_buf[slot].reshape(n_pages * N_HEADS, PAGE_SIZE)
    incl = _dot_exact_lhs(lf, suffix_ref[...])
    tot = jnp.broadcast_to(jnp.sum(lf, axis=1, keepdims=True), lf.shape)
    d = incl - lf + _dot_exact_rhs(later_ref[...], tot)
    s_pages = []
    for j in range(n_pages):
        bias = jnp.concatenate([d[j * N_HEADS:(j + 1) * N_HEADS, :]] * n_new, axis=0)
        s_pages.append(_dot(qb, k_buf[slot, j].astype(BF16)) + bias + cnq)

    m_blk = s_new
    for s in s_pages:
        m_blk = jnp.maximum(m_blk, s)
    m = jnp.max(m_blk, axis=1, keepdims=True)
    p_new = jnp.exp(s_new - m)
    l_blk = p_new
    acc = _dot(p_new.astype(BF16), vn_sc[...].astype(BF16))
    for j in range(n_pages):
        p = jnp.exp(s_pages[j] - m)
        l_blk = l_blk + p
        acc = acc + _dot_nt(p.astype(BF16), v_buf[slot, j].astype(BF16))
    o_ref[0] = acc / jnp.sum(l_blk, axis=1, keepdims=True)


def _attention_kernel(pt_ref, q_ref, k_ref, vt_ref, qs_ref, kn_ref, vn_ref, fn_ref, e_ref, suffix_ref, later_ref,
                      tri_ref, kt_hbm, vtc_hbm, lft_hbm, o_ref, os_ref, m_sc, acc_sc, k_buf, v_buf, f_buf, sems,
                      pad_sc, kn_sc, vn_sc, *, tq, v_chunk, q_chunk, n_pages, n_new):
    step = (pl.program_id(0) * pl.num_programs(1) + pl.program_id(1)) * pl.num_programs(2) + pl.program_id(2)
    n_steps = pl.num_programs(0) * pl.num_programs(1) * pl.num_programs(2)
    slot = step % 2
    pools = ((kt_hbm, k_buf), (vtc_hbm, v_buf), (lft_hbm, f_buf))

    def start_pages(seq, to_slot):
        for j in range(n_pages):
            page = pt_ref[seq * n_pages + j]
            for a, (hbm, buf) in enumerate(pools):
                pltpu.make_async_copy(hbm.at[page], buf.at[to_slot, j], sems.at[a, to_slot]).start()

    @pl.when(step == 0)
    def _():
        start_pages(0, 0)

    @pl.when(step + 1 < n_steps)
    def _():
        start_pages(step + 1, 1 - slot)

    _prompt_tile(pl.program_id(2), q_ref, k_ref, vt_ref, o_ref, m_sc, acc_sc, tq=tq, v_chunk=v_chunk,
                 q_chunk=q_chunk)

    for a, (hbm, buf) in enumerate(pools):
        pltpu.make_async_copy(hbm.at[pl.ds(0, n_pages)], buf.at[slot], sems.at[a, slot]).wait()
    _sample_sequence(slot, qs_ref, kn_ref, vn_ref, fn_ref, e_ref, suffix_ref, later_ref, tri_ref, os_ref,
                     k_buf, v_buf, f_buf, pad_sc, kn_sc, vn_sc, n_pages=n_pages, n_new=n_new)


def _attention(qa, ka, vtb, qblk, cache_kt, cache_vt, cache_lft, page_table, k_new, v_new, lf_new,
               *, n_prompt_seq, seq_len):
    t = qa.shape[0]
    v_chunk = vtb.shape[2]
    tq = ATTN_TILE
    qt = seq_len // tq
    n_pairs = N_HEADS // 2
    n_seq, n_rows, _ = qblk.shape
    assert n_seq == n_prompt_seq * n_pairs * qt, "one sample sequence per prompt attention tile"
    n_new = k_new.shape[1]
    n_pages = page_table.shape[1]
    pt = page_table.reshape(-1)
    row_head = jnp.arange(n_rows) % N_HEADS
    e = (row_head[:, None] == jnp.arange(LANES)[None, :]).astype(BF16)
    pos = jnp.arange(PAGE_SIZE)
    suffix = (pos[:, None] >= pos[None, :]).astype(BF16)
    tri = (pos[None, :] <= pos[:, None]).astype(BF16)
    r = jnp.arange(n_pages * N_HEADS)
    later = ((r[:, None] % N_HEADS == r[None, :] % N_HEADS)
             & (r[None, :] // N_HEADS > r[:, None] // N_HEADS)).astype(BF16)

    tile_q = lambda n, hp, i, pt_ref: (n * qt + i, hp)
    seq_map = lambda n, hp, i, pt_ref: ((n * n_pairs + hp) * qt + i, 0, 0)
    const_map = lambda n, hp, i, pt_ref: (0, 0)
    any_spec = pl.BlockSpec(memory_space=pl.ANY)
    in_specs = [pl.BlockSpec((tq, 2 * LANES), tile_q),
                pl.BlockSpec((seq_len, 2 * LANES), lambda n, hp, i, pt_ref: (n, hp)),
                pl.BlockSpec((seq_len // v_chunk, 2 * V_ROWS, v_chunk), lambda n, hp, i, pt_ref: (n, hp, 0)),
                pl.BlockSpec((1, n_rows, D_ATTN), seq_map),
                pl.BlockSpec((1, n_new, D_ATTN), seq_map), pl.BlockSpec((1, n_new, D_ATTN), seq_map),
                pl.BlockSpec((1, n_new, N_HEADS), seq_map),
                pl.BlockSpec(e.shape, const_map), pl.BlockSpec(suffix.shape, const_map),
                pl.BlockSpec(later.shape, const_map), pl.BlockSpec(tri.shape, const_map),
                any_spec, any_spec, any_spec]
    grid_spec = pltpu.PrefetchScalarGridSpec(
        num_scalar_prefetch=1,
        grid=(n_prompt_seq, n_pairs, qt),
        in_specs=in_specs,
        out_specs=[pl.BlockSpec((tq, LANES), tile_q), pl.BlockSpec((1, n_rows, D_ATTN), seq_map)],
        scratch_shapes=[pltpu.VMEM((2, 1, tq), F32), pltpu.VMEM((2, V_ROWS, tq), F32),
                        pltpu.VMEM((2, n_pages, D_ATTN, PAGE_SIZE), F32),
                        pltpu.VMEM((2, n_pages, D_ATTN, PAGE_SIZE), F32),
                        pltpu.VMEM((2, n_pages, N_HEADS, PAGE_SIZE), F32),
                        pltpu.SemaphoreType.DMA((3, 2)),
                        pltpu.VMEM((PAGE_SIZE, LANES), F32),
                        pltpu.VMEM((PAGE_SIZE, D_ATTN), F32), pltpu.VMEM((PAGE_SIZE, D_ATTN), F32)],
    )
    args = [pt, qa, ka, vtb, qblk, k_new, v_new, lf_new, e, suffix, later, tri, cache_kt, cache_vt, cache_lft]
    return pl.pallas_call(
        functools.partial(_attention_kernel, tq=tq, v_chunk=v_chunk, q_chunk=LANES, n_pages=n_pages, n_new=n_new),
        grid_spec=grid_spec,
        out_shape=[jax.ShapeDtypeStruct((t, D_ATTN), BF16), jax.ShapeDtypeStruct((n_seq, n_rows, D_ATTN), F32)],
        compiler_params=pltpu.CompilerParams(dimension_semantics=("arbitrary", "arbitrary", "arbitrary"),
                                             vmem_limit_bytes=VMEM_LIMIT_BYTES),
        name="attention",
    )(*args)


def _route(logits):
    rows = logits.shape[0]
    lane = lax.broadcasted_iota(jnp.int32, (rows, LANES), 1).astype(F32)
    big = float(LANES)
    in_g = lane < N_GROUPS
    gl = jnp.where(in_g, logits, MASK_VALUE)
    m_g = jnp.max(gl, axis=1, keepdims=True)
    z_g = jnp.sum(jnp.exp(gl - m_g), axis=1, keepdims=True)
    p_g = 1.0 / z_g
    g_idx = jnp.min(jnp.where(in_g & (gl == m_g), lane, big), axis=1, keepdims=True)
    lo = N_GROUPS + EXPERTS_PER_GROUP * g_idx
    sel = (lane >= lo) & (lane < lo + EXPERTS_PER_GROUP)
    el = jnp.where(sel, logits, MASK_VALUE)
    m_e = jnp.max(el, axis=1, keepdims=True)
    ex = jnp.exp(el - m_e)
    ep = ex / jnp.sum(ex, axis=1, keepdims=True)
    v1 = jnp.max(jnp.where(sel, ep, -1.0), axis=1, keepdims=True)
    i1 = jnp.min(jnp.where(sel & (ep == v1), lane, big), axis=1, keepdims=True)
    sel2 = sel & (lane != i1)
    v2 = jnp.max(jnp.where(sel2, ep, -1.0), axis=1, keepdims=True)
    i2 = jnp.min(jnp.where(sel2 & (ep == v2), lane, big), axis=1, keepdims=True)
    denom = v1 + v2
    w1 = p_g * (v1 / denom)
    w2 = p_g * (v2 / denom)
    gate4 = jnp.where(lane == i1 - lo, w1, 0.0) + jnp.where(lane == i2 - lo, w2, 0.0)
    return gate4, g_idx


def _post_kernel(o_ref, ap_ref, sgb_ref, x_ref, wao_ref, wo_ref, g_ref, b_ref, wr_ref, tri_ref,
                 x1e_ref, route_t_ref, totals_ref, count_sc, *, alpha, d_model):
    i = pl.program_id(0)
    rows = x_ref.shape[0]
    yb = _dot(o_ref[...], wao_ref[...])
    mix = _dot((ap_ref[...] + sgb_ref[...] * yb).astype(BF16), wo_ref[...])
    x1 = _layer_norm(alpha * x_ref[...] + mix, g_ref[...], b_ref[...])
    xh = x1.astype(BF16)
    xl = (x1 - xh.astype(F32)).astype(BF16)
    hi_lo = _dot(xh, wr_ref[...])
    logits = hi_lo[:, 0:LANES] + hi_lo[:, LANES:2 * LANES] + _dot(xl, wr_ref[:, 0:LANES])
    gate4, g_idx = _route(logits)

    @pl.when(i == 0)
    def _():
        count_sc[...] = jnp.zeros((1, LANES), F32)
    lane = lax.broadcasted_iota(jnp.int32, (rows, LANES), 1).astype(F32)
    onehot = jnp.where(lane == g_idx, 1.0, 0.0)
    count = _dot(tri_ref[...], onehot.astype(BF16)) + count_sc[...]
    rank = jnp.sum(onehot * count, axis=1, keepdims=True) - 1.0
    count_sc[...] = count[rows - 1:rows, :]
    totals_ref[...] = jnp.broadcast_to(count[rows - 1:rows, :], (SUBLANES, LANES))
    route = (gate4 + jnp.where(lane == ROUTE_GROUP_LANE, g_idx, 0.0)
             + jnp.where(lane == ROUTE_RANK_LANE, rank, 0.0))
    route_t_ref[...] = route.T[0:SUBLANES, :]
    x1e_ref[:, 0:d_model] = x1
    x1e_ref[:, d_model:d_model + LANES] = route


def _post_attention(o, ap, sgb, x, w, *, alpha):
    t, d_model = x.shape
    rows = MOE_ROWS
    row_spec = lambda width: pl.BlockSpec((rows, width), lambda i: (i, 0))
    weights = [w['wao'], w['wo'], w['ln1_g'], w['ln1_b'], w['wr'], w['tri_moe']]
    return pl.pallas_call(
        functools.partial(_post_kernel, alpha=alpha, d_model=d_model),
        grid=(t // rows,),
        in_specs=[row_spec(D_ATTN), row_spec(d_model), row_spec(d_model), row_spec(d_model)]
                 + [_const_spec(a.shape) for a in weights],
        out_specs=[row_spec(d_model + LANES), pl.BlockSpec((SUBLANES, rows), lambda i: (0, i)),
                   pl.BlockSpec((SUBLANES, LANES), lambda i: (0, 0))],
        out_shape=[jax.ShapeDtypeStruct((t, d_model + LANES), F32),
                   jax.ShapeDtypeStruct((SUBLANES, t), F32),
                   jax.ShapeDtypeStruct((SUBLANES, LANES), F32)],
        scratch_shapes=[pltpu.VMEM((1, LANES), F32)],
        compiler_params=pltpu.CompilerParams(dimension_semantics=("arbitrary",),
                                             vmem_limit_bytes=VMEM_LIMIT_BYTES),
        name="post_attn",
    )(o, ap, sgb, x, *weights)


def _row_permute_kernel(pos_ref, *refs, rows, scatter):
    if scatter:
        fill_ref, src_ref, dst_ref, zero_sc, sem, fill_sem = refs

        @pl.when(pl.program_id(0) == 0)
        def _():
            zero_sc[...] = jnp.zeros_like(zero_sc)

            def clear(g):
                return pltpu.make_async_copy(zero_sc, dst_ref.at[pl.ds(fill_ref[g] * rows, rows)], fill_sem)

            def for_each_distinct(action):
                action(0)
                for g in range(1, fill_ref.shape[0]):
                    pl.when(fill_ref[g] != fill_ref[g - 1])(functools.partial(action, g))

            for_each_distinct(lambda g: clear(g).start())
            for_each_distinct(lambda g: clear(g).wait())
    else:
        src_ref, dst_ref, sem = refs
    base = pl.program_id(0) * rows

    def issue(r, carry):
        p = pos_ref[base + r]
        if scatter:
            pltpu.make_async_copy(src_ref.at[pl.ds(r, 1)], dst_ref.at[pl.ds(p, 1)], sem).start()
        else:
            pltpu.make_async_copy(src_ref.at[pl.ds(p, 1)], dst_ref.at[pl.ds(r, 1)], sem).start()
        return carry

    lax.fori_loop(0, rows, issue, 0, unroll=True)
    if scatter:
        pltpu.make_async_copy(src_ref, dst_ref.at[pl.ds(0, rows)], sem).wait()
    else:
        pltpu.make_async_copy(src_ref.at[pl.ds(0, rows)], dst_ref, sem).wait()


def _row_permute(pos, src, *, n_out, scatter, fill_tiles=None):
    n_rows = pos.shape[0]
    width = src.shape[1]
    rows = MOE_ROWS
    any_spec = pl.BlockSpec(memory_space=pl.ANY)
    tile_spec = pl.BlockSpec((rows, width), lambda i, *_: (i, 0))
    if scatter:
        args = [pos, fill_tiles, src]
        in_specs, out_specs = [tile_spec], any_spec
        scratch = [pltpu.VMEM((rows, width), src.dtype), pltpu.SemaphoreType.DMA(()), pltpu.SemaphoreType.DMA(())]
    else:
        args = [pos, src]
        in_specs, out_specs = [any_spec], tile_spec
        scratch = [pltpu.SemaphoreType.DMA(())]
    grid_spec = pltpu.PrefetchScalarGridSpec(
        num_scalar_prefetch=len(args) - 1, grid=(n_rows // rows,), in_specs=in_specs, out_specs=out_specs,
        scratch_shapes=scratch)
    return pl.pallas_call(
        functools.partial(_row_permute_kernel, rows=rows, scatter=scatter),
        grid_spec=grid_spec,
        out_shape=jax.ShapeDtypeStruct((n_out, width), src.dtype),
        compiler_params=pltpu.CompilerParams(dimension_semantics=("arbitrary",)),
        name="row_scatter" if scatter else "row_gather",
    )(*args)


def _moe_kernel(tg_ref, nv_ref, xs_ref, w1_ref, w3_ref, w2_ref, g_ref, b_ref, y_ref, *, alpha, d_model):
    del tg_ref
    i = pl.program_id(0)

    @pl.when(i < nv_ref[0])
    def _():
        x = xs_ref[:, 0:d_model]
        route = xs_ref[:, d_model:d_model + LANES]
        xb = x.astype(BF16)
        lane = lax.broadcasted_iota(jnp.int32, route.shape, 1)
        acc = jnp.zeros_like(x)
        for e in range(EXPERTS_PER_GROUP):
            h1 = _dot(xb, w1_ref[0, e])
            h3 = _dot(xb, w3_ref[0, e])
            h = (h1 * jax.nn.sigmoid(h1)) * h3
            ye = _dot(h.astype(BF16), w2_ref[0, e])
            ge = jnp.sum(jnp.where(lane == e, route, 0.0), axis=1, keepdims=True)
            acc = acc + ge * ye
        y_ref[...] = _layer_norm(alpha * x + acc, g_ref[...], b_ref[...])

    @pl.when(i >= nv_ref[0])
    def _():
        y_ref[...] = jnp.zeros_like(y_ref)


def _experts(x1e, route_t, totals, w, *, alpha):
    t = x1e.shape[0]
    d_model = x1e.shape[1] - LANES
    _, _, _, d_exp = w['w1'].shape
    rows = MOE_ROWS
    n_tiles = t // rows + N_GROUPS
    group = route_t[ROUTE_GROUP_LANE].astype(jnp.int32)
    rank = route_t[ROUTE_RANK_LANE].astype(jnp.int32)
    count = totals[0, 0:N_GROUPS].astype(jnp.int32)
    tiles_g = (count + rows - 1) // rows
    tile_end = jnp.cumsum(tiles_g)
    pos = (tile_end - tiles_g)[group] * rows + rank
    n_valid = tile_end[N_GROUPS - 1:N_GROUPS]
    tile_group = jnp.minimum(jnp.sum(jnp.arange(n_tiles)[:, None] >= tile_end[None, :], axis=1),
                             N_GROUPS - 1).astype(jnp.int32)

    fill_tiles = jnp.concatenate([jnp.maximum(tile_end - 1, 0),
                                  jnp.minimum(n_valid + jnp.arange(N_GROUPS), n_tiles - 1)]).astype(jnp.int32)
    xs = _row_permute(pos, x1e, n_out=n_tiles * rows, scatter=True, fill_tiles=fill_tiles)

    live = lambda i, nv: jnp.minimum(i, nv[0] - 1)
    w_spec = lambda a: pl.BlockSpec((1,) + a.shape[1:], lambda i, tg, nv: (tg[live(i, nv)], 0, 0, 0))
    grid_spec = pltpu.PrefetchScalarGridSpec(
        num_scalar_prefetch=2,
        grid=(n_tiles,),
        in_specs=[
            pl.BlockSpec((rows, d_model + LANES), lambda i, tg, nv: (live(i, nv), 0)),
            w_spec(w['w1']), w_spec(w['w3']), w_spec(w['w2']),
            pl.BlockSpec((1, d_model), lambda i, tg, nv: (0, 0)),
            pl.BlockSpec((1, d_model), lambda i, tg, nv: (0, 0)),
        ],
        out_specs=pl.BlockSpec((rows, d_model), lambda i, tg, nv: (i, 0)),
    )
    ys = pl.pallas_call(
        functools.partial(_moe_kernel, alpha=alpha, d_model=d_model),
        grid_spec=grid_spec,
        out_shape=jax.ShapeDtypeStruct((n_tiles * rows, d_model), F32),
        compiler_params=pltpu.CompilerParams(dimension_semantics=("arbitrary",),
                                             vmem_limit_bytes=VMEM_LIMIT_BYTES),
        name="experts",
    )(tile_group, n_valid, xs, w['w1'], w['w3'], w['w2'], w['ln2_g'], w['ln2_b'])
    return _row_permute(pos, ys, n_out=t, scatter=False)


def _prepare_weights(w_in, b_f, conv_w, w_conv_out, w_attn_out, w_o, ln1_g, ln1_b,
                     w_group, w_router, w1, w3, w2, ln2_g, ln2_b):
    d_conv = w_conv_out.shape[0]
    o_q = 3 * d_conv
    o_f = o_q + 3 * D_ATTN
    o_g = o_f + N_HEADS
    w_bf = w_in.astype(BF16)
    wm = w_bf[:, 0:o_f]
    wf = jnp.pad(w_bf[:, o_f:o_g], ((0, 0), (0, LANES - N_HEADS)))
    wg = w_bf[:, o_g:]
    bf = jnp.pad(b_f.astype(F32), (0, LANES - N_HEADS)).reshape(1, LANES)

    col = jnp.arange(N_HEADS * LANES)
    col_head, col_lane = col // LANES, col % LANES
    part = jnp.arange(LANES)
    part_k, part_h = part // N_HEADS, part % N_HEADS
    live = (part_k[:, None] < AUG_SPLITS) & (part_h[:, None] == col_head[None, :])
    pq = live & (col_lane[None, :] == AUG_Q_C + part_k[:, None])
    pk = live & (col_lane[None, :] == AUG_Q_C + AUG_SPLITS + part_k[:, None])
    oq = ((col_lane >= AUG_Q_C + AUG_SPLITS) & (col_lane < AUG_Q_C + 2 * AUG_SPLITS)).astype(F32).reshape(1, -1)
    ok = ((col_lane >= AUG_Q_C) & (col_lane < AUG_Q_C + AUG_SPLITS)).astype(F32).reshape(1, -1)
    r = jnp.arange(INPROJ_ROWS)
    tri = (r[None, :] <= r[:, None]).astype(BF16)
    r = jnp.arange(MOE_ROWS)
    tri_moe = (r[None, :] <= r[:, None]).astype(BF16)

    wr = jnp.pad(jnp.concatenate([w_group, w_router], axis=1), ((0, 0), (0, LANES - N_GROUPS - N_EXPERTS)))
    wr_hi = wr.astype(BF16)
    wr_lo = (wr - wr_hi.astype(F32)).astype(BF16)
    wr = jnp.concatenate([wr_hi, wr_lo], axis=1)
    return dict(
        wm=wm, wf=wf, wg=wg, bf=bf, cw=conv_w.astype(F32), wco=w_conv_out.astype(BF16),
        tri=tri, pq=pq.astype(BF16), pk=pk.astype(BF16), oq=oq, ok=ok,
        wao=w_attn_out.astype(BF16), wo=w_o.astype(BF16),
        ln1_g=ln1_g.reshape(1, -1), ln1_b=ln1_b.reshape(1, -1), wr=wr,
        tri_moe=tri_moe,
        w1=w1.astype(BF16).reshape((N_GROUPS, EXPERTS_PER_GROUP) + w1.shape[1:]),
        w3=w3.astype(BF16).reshape((N_GROUPS, EXPERTS_PER_GROUP) + w3.shape[1:]),
        w2=w2.astype(BF16).reshape((N_GROUPS, EXPERTS_PER_GROUP) + w2.shape[1:]),
        ln2_g=ln2_g.reshape(1, -1), ln2_b=ln2_b.reshape(1, -1),
    )


def kernel(x_prompt, x_sample, cache_k, cache_v, cache_logf, state_conv, page_table, w_in, b_f, conv_w,
           w_conv_out, w_attn_out, w_o, ln1_g, ln1_b, w_group, w_router, w1, w3, w2, ln2_g, ln2_b):
    depth = w_in.shape[0]
    assert depth == 1, "single-layer stack"
    n_p, s_p, d_model = x_prompt.shape
    n_s, s_s, _ = x_sample.shape
    d_conv = w_conv_out.shape[1]
    alpha = (2 * depth) ** 0.25
    w = _prepare_weights(w_in[0], b_f[0], conv_w[0], w_conv_out[0], w_attn_out[0], w_o[0], ln1_g[0], ln1_b[0],
                         w_group[0], w_router[0], w1[0], w3[0], w2[0], ln2_g[0], ln2_b[0])

    xp = x_prompt.reshape(n_p * s_p, d_model)
    qa, kt_p, vt_p, ka, vtb, lft_p, ap, sgb, tail = _inproj(xp, w, sample=False, seq_len=s_p)
    xs = x_sample.reshape(n_s * s_s, d_model)
    state = state_conv[0]
    s0 = jnp.repeat(state[:, 0, :], s_s, axis=0)
    s1 = jnp.repeat(state[:, 1, :], s_s, axis=0)
    q_s, k_s, v_s, lf_s, ap_s, sgb_s, u_s = _inproj(xs, w, sample=True, seq_len=s_s, state=(s0, s1))
    q5 = q_s.reshape(n_s, s_s, N_HEADS, HEAD_DIM)
    eye_h = jnp.eye(N_HEADS, dtype=BF16)
    qblk = (q5[:, :, :, None, :] * eye_h[None, None, :, :, None]).reshape(n_s, s_s * N_HEADS, D_ATTN)
    n_pool = cache_k.shape[1]
    cache_kt = cache_k[0].transpose(0, 2, 3, 1).reshape(n_pool, D_ATTN, PAGE_SIZE)
    cache_vt = cache_v[0].transpose(0, 2, 3, 1).reshape(n_pool, D_ATTN, PAGE_SIZE)
    cache_lft = cache_logf[0].transpose(0, 2, 1)
    o_p, o_blk = _attention(
        qa, ka, vtb, qblk, cache_kt, cache_vt, cache_lft, page_table, k_s.reshape(n_s, s_s, D_ATTN),
        v_s.reshape(n_s, s_s, D_ATTN), lf_s.reshape(n_s, s_s, N_HEADS), n_prompt_seq=n_p, seq_len=s_p)
    x1e_p, route_p, totals_p = _post_attention(o_p, ap, sgb, xp, w, alpha=alpha)
    y_p = _experts(x1e_p, route_p, totals_p, w, alpha=alpha)
    o5 = o_blk.reshape(n_s, s_s, N_HEADS, N_HEADS, HEAD_DIM)
    o_s = jnp.einsum('bthgd,hg->bthd', o5, jnp.eye(N_HEADS, dtype=F32)).reshape(n_s * s_s, D_ATTN).astype(BF16)
    x1e_s, route_s, totals_s = _post_attention(o_s, ap_s, sgb_s, xs, w, alpha=alpha)
    y_s = _experts(x1e_s, route_s, totals_s, w, alpha=alpha)

    return (
        y_p.reshape(n_p, s_p, d_model),
        y_s.reshape(n_s, s_s, d_model),
        kt_p.reshape(1, n_p, N_HEADS, HEAD_DIM, s_p).transpose(0, 1, 4, 2, 3),
        vt_p.reshape(1, n_p, N_HEADS, HEAD_DIM, s_p).transpose(0, 1, 4, 2, 3),
        lft_p.reshape(1, n_p, N_HEADS, s_p).transpose(0, 1, 3, 2),
        tail[:, SUBLANES - (CONV_WIDTH - 1):, :].reshape(1, n_p, CONV_WIDTH - 1, d_conv),
        k_s.reshape(1, n_s, s_s, N_HEADS, HEAD_DIM),
        v_s.reshape(1, n_s, s_s, N_HEADS, HEAD_DIM),
        lf_s.reshape(1, n_s, s_s, N_HEADS),
        u_s.reshape(n_s, s_s, d_conv)[:, s_s - (CONV_WIDTH - 1):, :].reshape(1, n_s, CONV_WIDTH - 1, d_conv),
    )
```

```python
import functools

import jax
import jax.numpy as jnp
from jax import lax
from jax.experimental import pallas as pl
from jax.experimental.pallas import tpu as pltpu

F32 = jnp.float32
BF16 = jnp.bfloat16

LANES = 128
SUBLANES = 8
VMEM_LIMIT_BYTES = 56 * 1024 * 1024

CONV_WIDTH = 3
N_HEADS = 8
HEAD_DIM = 64
D_ATTN = N_HEADS * HEAD_DIM
N_GROUPS = 4
EXPERTS_PER_GROUP = 4
N_EXPERTS = N_GROUPS * EXPERTS_PER_GROUP
LN_EPS = 1e-5
PAGE_SIZE = 128
MASK_VALUE = -1e30
AUG_Q_C = HEAD_DIM
AUG_SPLITS = 3
INPROJ_ROWS = 256
ATTN_TILE = 512
MOE_ROWS = 512
ROUTE_GROUP_LANE = EXPERTS_PER_GROUP
ROUTE_RANK_LANE = EXPERTS_PER_GROUP + 1
V_PAD_ROWS = 16
V_ROWS = HEAD_DIM + V_PAD_ROWS


def _dot(a, b):
    return jnp.dot(a, b, preferred_element_type=F32)


def _dot_nt(a, b):
    return lax.dot_general(a, b, (((1,), (1,)), ((), ())), preferred_element_type=F32)


def _split3(x):
    h1 = x.astype(BF16)
    r1 = x - h1.astype(F32)
    h2 = r1.astype(BF16)
    r2 = r1 - h2.astype(F32)
    h3 = r2.astype(BF16)
    return h1, h2, h3


def _dot_exact_rhs(a_bf16, x):
    h1, h2, h3 = _split3(x)
    return _dot(a_bf16, h1) + _dot(a_bf16, h2) + _dot(a_bf16, h3)


def _dot_exact_lhs(x, b_bf16):
    h1, h2, h3 = _split3(x)
    return _dot(h1, b_bf16) + _dot(h2, b_bf16) + _dot(h3, b_bf16)


def _dot_nt_exact_rhs(a_bf16, x):
    h1, h2, h3 = _split3(x)
    return _dot_nt(a_bf16, h1) + _dot_nt(a_bf16, h2) + _dot_nt(a_bf16, h3)


def _log_sigmoid(x):
    return jnp.minimum(x, 0.0) - jnp.log1p(jnp.exp(-jnp.abs(x)))


def _layer_norm(r, g, b):
    mu = jnp.mean(r, axis=-1, keepdims=True)
    d = r - mu
    var = jnp.mean(d * d, axis=-1, keepdims=True)
    return d * lax.rsqrt(var + LN_EPS) * g + b


def _const_spec(shape):
    nd = len(shape)
    return pl.BlockSpec(shape, lambda *_: (0,) * nd, pipeline_mode=pl.Buffered(1))


def _inproj_kernel(*refs, rows, tiles_per_seq, sample, d_conv, d_model):
    if sample:
        (x_ref, wm_ref, wf_ref, wg_ref, bf_ref, cw_ref, wco_ref, s0_ref, s1_ref,
         q_ref, k_ref, v_ref, lf_ref, ap_ref, sgb_ref, u_ref, ubuf) = refs
    else:
        (x_ref, wm_ref, wf_ref, wg_ref, bf_ref, cw_ref, wco_ref, tri_ref, pq_ref, pk_ref,
         oq_ref, ok_ref,
         q_ref, k_ref, v_ref, ka_ref, vb_ref, lf_ref, ap_ref, sgb_ref, tail_ref, ubuf, ccarry) = refs
    i = pl.program_id(0)
    xb = x_ref[...].astype(BF16)

    xc = _dot(xb, wm_ref[:, 0:d_conv])
    bg = _dot(xb, wm_ref[:, d_conv:2 * d_conv])
    cg = _dot(xb, wm_ref[:, 2 * d_conv:3 * d_conv])
    u = cg * xc
    if sample:
        ubuf[0:SUBLANES, :] = jnp.zeros((SUBLANES, d_conv), F32)
    else:
        @pl.when(i % tiles_per_seq == 0)
        def _():
            ubuf[0:SUBLANES, :] = jnp.zeros((SUBLANES, d_conv), F32)
    ubuf[SUBLANES:SUBLANES + rows, :] = u
    prev1 = ubuf[SUBLANES - 1:SUBLANES - 1 + rows, :]
    prev2 = ubuf[SUBLANES - 2:SUBLANES - 2 + rows, :]
    if sample:
        pos = lax.broadcasted_iota(jnp.int32, (rows, 1), 0) % 4
        prev1 = jnp.where(pos == 0, s1_ref[...], prev1)
        prev2 = jnp.where(pos == 0, s0_ref[...], jnp.where(pos == 1, s1_ref[...], prev2))
        u_ref[...] = u
    else:
        ubuf[0:SUBLANES, :] = u[rows - SUBLANES:rows, :]

        @pl.when(i % tiles_per_seq == tiles_per_seq - 1)
        def _():
            tail_ref[0] = u[rows - SUBLANES:rows, :]
    conv = cw_ref[0:1, :] * prev2 + cw_ref[1:2, :] * prev1 + cw_ref[2:3, :] * u
    ya = _dot((bg * conv).astype(BF16), wco_ref[...])

    q = _dot(xb, wm_ref[:, 3 * d_conv:3 * d_conv + D_ATTN]) * (HEAD_DIM ** -0.5)
    k = _dot(xb, wm_ref[:, 3 * d_conv + D_ATTN:3 * d_conv + 2 * D_ATTN])
    v = _dot(xb, wm_ref[:, 3 * d_conv + 2 * D_ATTN:3 * d_conv + 3 * D_ATTN])
    fg = _dot(xb, wf_ref[...]) + bf_ref[...]
    lane = lax.broadcasted_iota(jnp.int32, (rows, LANES), 1)
    logf = jnp.where(lane < N_HEADS, _log_sigmoid(fg), 0.0)
    if sample:
        k_ref[...] = k
        v_ref[...] = v
        lf_ref[...] = logf[:, 0:N_HEADS]
        q_ref[...] = q.astype(BF16)
    else:
        k_ref[0] = k.T
        vt = v.T
        v_ref[0] = vt
        lf_ref[0] = logf.T[0:N_HEADS, :]
        ones_blk = jnp.where(lax.broadcasted_iota(jnp.int32, (V_PAD_ROWS, rows), 0) == 0, 1.0, 0.0)
        vb_ref[0] = jnp.concatenate(
            [blk for h in range(N_HEADS) for blk in (vt[h * HEAD_DIM:(h + 1) * HEAD_DIM, :], ones_blk)],
            axis=0).astype(BF16)

        @pl.when(i % tiles_per_seq == 0)
        def _():
            ccarry[...] = jnp.zeros((1, LANES), F32)
        c = _dot_exact_rhs(tri_ref[...], logf) + ccarry[...]
        ccarry[...] = c[rows - 1:rows, :]
        c1, c2, c3 = _split3(c)
        c_parts = (c1.astype(F32) + pltpu.roll(c2.astype(F32), N_HEADS, 1)
                   + pltpu.roll(c3.astype(F32), 2 * N_HEADS, 1)).astype(BF16)
        aug_q = _dot(c_parts, pq_ref[...]) + oq_ref[...]
        aug_k = ok_ref[...] - _dot(c_parts, pk_ref[...])

        def spread_heads(x, aug):
            groups = []
            for h in range(N_HEADS):
                slab = x[:, (h // 2) * LANES:(h // 2 + 1) * LANES]
                if h % 2:
                    slab = pltpu.roll(slab, HEAD_DIM, 1)
                groups.append(jnp.where(lane < HEAD_DIM, slab, aug[:, h * LANES:(h + 1) * LANES]))
            return jnp.concatenate(groups, axis=1).astype(BF16)

        q_ref[...] = spread_heads(q, aug_q)
        ka_ref[...] = spread_heads(k, aug_k)

    ga = _dot(xb, wg_ref[:, 0:d_model])
    gb = _dot(xb, wg_ref[:, d_model:2 * d_model])
    ap_ref[...] = jax.nn.sigmoid(ga) * ya
    sgb_ref[...] = jax.nn.sigmoid(gb)


def _inproj(x, w, *, sample, seq_len, state=None):
    t, d_model = x.shape
    d_conv = w['wco'].shape[0]
    rows = INPROJ_ROWS
    tiles_per_seq = max(seq_len // rows, 1)
    n_tiles = t // rows
    row_spec = lambda width: pl.BlockSpec((rows, width), lambda i: (i, 0))
    weights = [w['wm'], w['wf'], w['wg'], w['bf'], w['cw'], w['wco']]
    in_specs = [row_spec(d_model)] + [_const_spec(a.shape) for a in weights]
    args = [x] + weights
    if sample:
        args += [state[0], state[1]]
        in_specs += [row_spec(d_conv), row_spec(d_conv)]
        out_shape = [
            jax.ShapeDtypeStruct((t, D_ATTN), BF16),
            jax.ShapeDtypeStruct((t, D_ATTN), F32),
            jax.ShapeDtypeStruct((t, D_ATTN), F32),
            jax.ShapeDtypeStruct((t, N_HEADS), F32),
            jax.ShapeDtypeStruct((t, d_model), F32),
            jax.ShapeDtypeStruct((t, d_model), F32),
            jax.ShapeDtypeStruct((t, d_conv), F32),
        ]
        out_specs = [row_spec(D_ATTN), row_spec(D_ATTN), row_spec(D_ATTN), row_spec(N_HEADS),
                     row_spec(d_model), row_spec(d_model), row_spec(d_conv)]
        scratch = [pltpu.VMEM((rows + SUBLANES, d_conv), F32)]
    else:
        consts = [w['tri'], w['pq'], w['pk'], w['oq'], w['ok']]
        args += consts
        in_specs += [_const_spec(a.shape) for a in consts]
        n_seq = t // seq_len
        out_shape = [
            jax.ShapeDtypeStruct((t, N_HEADS * LANES), BF16),
            jax.ShapeDtypeStruct((n_seq, D_ATTN, seq_len), F32),
            jax.ShapeDtypeStruct((n_seq, D_ATTN, seq_len), F32),
            jax.ShapeDtypeStruct((t, N_HEADS * LANES), BF16),
            jax.ShapeDtypeStruct((n_tiles, N_HEADS * V_ROWS, rows), BF16),
            jax.ShapeDtypeStruct((n_seq, N_HEADS, seq_len), F32),
            jax.ShapeDtypeStruct((t, d_model), F32),
            jax.ShapeDtypeStruct((t, d_model), F32),
            jax.ShapeDtypeStruct((n_seq, SUBLANES, d_conv), F32),
        ]
        pos_minor = lambda height: pl.BlockSpec((1, height, rows),
                                                lambda i: (i // tiles_per_seq, 0, i % tiles_per_seq))
        out_specs = [row_spec(N_HEADS * LANES), pos_minor(D_ATTN), pos_minor(D_ATTN), row_spec(N_HEADS * LANES),
                     pl.BlockSpec((1, N_HEADS * V_ROWS, rows), lambda i: (i, 0, 0)), pos_minor(N_HEADS),
                     row_spec(d_model), row_spec(d_model),
                     pl.BlockSpec((1, SUBLANES, d_conv), lambda i: (i // tiles_per_seq, 0, 0))]
        scratch = [pltpu.VMEM((rows + SUBLANES, d_conv), F32), pltpu.VMEM((1, LANES), F32)]
    kern = functools.partial(_inproj_kernel, rows=rows, tiles_per_seq=tiles_per_seq, sample=sample,
                             d_conv=d_conv, d_model=d_model)
    return pl.pallas_call(
        kern,
        grid=(n_tiles,),
        in_specs=in_specs,
        out_specs=out_specs,
        out_shape=out_shape,
        scratch_shapes=scratch,
        compiler_params=pltpu.CompilerParams(dimension_semantics=("arbitrary",),
                                             vmem_limit_bytes=VMEM_LIMIT_BYTES),
        name="inproj_sample" if sample else "inproj_prompt",
    )(*args)


def _prompt_tile(i, q_ref, k_ref, vt_ref, o_ref, m_sc, acc_sc, *, tq, v_chunk, q_chunk):
    chunks = tq // v_chunk
    for hl in range(2):
        m_sc[hl] = jnp.full((1, tq), MASK_VALUE, F32)
        acc_sc[hl] = jnp.zeros((V_ROWS, tq), F32)

    def step(j, masked):
        start = pl.multiple_of(j * tq, tq)
        blocks = [(hl, qc) for hl in range(2) for qc in range(tq // q_chunk)]
        m_old = [m_sc[hl] for hl in range(2)]
        acc_old = [acc_sc[hl] for hl in range(2)]
        sts = [_dot_nt(k_ref[pl.ds(start, tq), hl * LANES:(hl + 1) * LANES],
                       q_ref[qc * q_chunk:(qc + 1) * q_chunk, hl * LANES:(hl + 1) * LANES]) for hl, qc in blocks]
        m_out, acc_out = [[], []], [[], []]
        for (hl, qc), st in zip(blocks, sts):
            qs = slice(qc * q_chunk, (qc + 1) * q_chunk)
            if masked:
                key = lax.broadcasted_iota(jnp.int32, (tq, q_chunk), 0)
                qry = lax.broadcasted_iota(jnp.int32, (tq, q_chunk), 1) + qc * q_chunk
                st = jnp.where(key <= qry, st, MASK_VALUE)
            m_prev = m_old[hl][:, qs]
            m_new = jnp.maximum(m_prev, jnp.max(st, axis=0, keepdims=True))
            alpha = jnp.exp(m_prev - m_new)
            pb = jnp.exp(st - m_new).astype(BF16)
            acc = alpha * acc_old[hl][:, qs]
            for c in range(chunks):
                vt = vt_ref[j * chunks + c, hl * V_ROWS:(hl + 1) * V_ROWS, :]
                acc = acc + _dot(vt, pb[c * v_chunk:(c + 1) * v_chunk, :])
            acc_out[hl].append(acc)
            m_out[hl].append(m_new)
        for hl in range(2):
            m_sc[hl] = jnp.concatenate(m_out[hl], axis=1)
            acc_sc[hl] = jnp.concatenate(acc_out[hl], axis=1)

    def body(j, carry):
        step(j, False)
        return carry

    lax.fori_loop(0, i, body, 0)
    step(i, True)
    ot = jnp.concatenate([acc_sc[hl, 0:HEAD_DIM, :] / acc_sc[hl, HEAD_DIM:HEAD_DIM + 1, :] for hl in range(2)],
                         axis=0)
    o_ref[...] = ot.T.astype(BF16)


def _sample_sequence(slot, q_ref, kn_ref, vn_ref, fn_ref, e_ref, suffix_ref, later_ref, tri_ref, o_ref,
                     k_buf, v_buf, f_buf, pad_sc, kn_sc, vn_sc, *, n_pages, n_new):
    n_rows = N_HEADS * n_new
    qb = q_ref[0]
    e = e_ref[...]

    pad_sc[...] = jnp.zeros((PAGE_SIZE, LANES), F32)
    pad_sc[0:n_new, 0:N_HEADS] = fn_ref[0]
    kn_sc[...] = jnp.zeros((PAGE_SIZE, D_ATTN), F32)
    kn_sc[0:n_new, :] = kn_ref[0]
    vn_sc[...] = jnp.zeros((PAGE_SIZE, D_ATTN), F32)
    vn_sc[0:n_new, :] = vn_ref[0]
    cn = _dot_exact_rhs(tri_ref[...], pad_sc[...])
    cn_rows = _dot_nt_exact_rhs(e, cn)
    row_t = lax.broadcasted_iota(jnp.int32, (n_rows, LANES), 0) // N_HEADS
    col = lax.broadcasted_iota(jnp.int32, (n_rows, LANES), 1)
    cnq = jnp.sum(jnp.where(col == row_t, cn_rows, 0.0), axis=1, keepdims=True)
    s_new = _dot_nt(qb, kn_sc[...].astype(BF16)) + (cnq - cn_rows)
    s_new = jnp.where(col <= row_t, s_new, MASK_VALUE)

    lf = f_buf[slot].reshape(n_pages * N_HEADS, PAGE_SIZE)
    incl = _dot_exact_lhs(lf, suffix_ref[...])
    tot = jnp.broadcast_to(jnp.sum(lf, axis=1, keepdims=True), lf.shape)
    d = incl - lf + _dot_exact_rhs(later_ref[...], tot)
    s_pages = []
    for j in range(n_pages):
        bias = jnp.concatenate([d[j * N_HEADS:(j + 1) * N_HEADS, :]] * n_new, axis=0)
        s_pages.append(_dot(qb, k_buf[slot, j].astype(BF16)) + bias + cnq)

    m_blk = s_new
    for s in s_pages:
        m_blk = jnp.maximum(m_blk, s)
    m = jnp.max(m_blk, axis=1, keepdims=True)
    p_new = jnp.exp(s_new - m)
    l_blk = p_new
    acc = _dot(p_new.astype(BF16), vn_sc[...].astype(BF16))
    for j in range(n_pages):
        p = jnp.exp(s_pages[j] - m)
        l_blk = l_blk + p
        acc = acc + _dot_nt(p.astype(BF16), v_buf[slot, j].astype(BF16))
    o_ref[0] = acc / jnp.sum(l_blk, axis=1, keepdims=True)


def _attention_kernel(pt_ref, q_ref, k_ref, vt_ref, qs_ref, kn_ref, vn_ref, fn_ref, e_ref, suffix_ref, later_ref,
                      tri_ref, kt_hbm, vtc_hbm, lft_hbm, o_ref, os_ref, m_sc, acc_sc, k_buf, v_buf, f_buf, sems,
                      pad_sc, kn_sc, vn_sc, *, tq, v_chunk, q_chunk, n_pages, n_new):
    step = (pl.program_id(0) * pl.num_programs(1) + pl.program_id(1)) * pl.num_programs(2) + pl.program_id(2)
    n_steps = pl.num_programs(0) * pl.num_programs(1) * pl.num_programs(2)
    slot = step % 2
    pools = ((kt_hbm, k_buf), (vtc_hbm, v_buf), (lft_hbm, f_buf))

    def start_pages(seq, to_slot):
        for j in range(n_pages):
            page = pt_ref[seq * n_pages + j]
            for a, (hbm, buf) in enumerate(pools):
                pltpu.make_async_copy(hbm.at[page], buf.at[to_slot, j], sems.at[a, to_slot]).start()

    @pl.when(step == 0)
    def _():
        start_pages(0, 0)

    @pl.when(step + 1 < n_steps)
    def _():
        start_pages(step + 1, 1 - slot)

    _prompt_tile(pl.program_id(2), q_ref, k_ref, vt_ref, o_ref, m_sc, acc_sc, tq=tq, v_chunk=v_chunk,
                 q_chunk=q_chunk)

    for a, (hbm, buf) in enumerate(pools):
        pltpu.make_async_copy(hbm.at[pl.ds(0, n_pages)], buf.at[slot], sems.at[a, slot]).wait()
    _sample_sequence(slot, qs_ref, kn_ref, vn_ref, fn_ref, e_ref, suffix_ref, later_ref, tri_ref, os_ref,
                     k_buf, v_buf, f_buf, pad_sc, kn_sc, vn_sc, n_pages=n_pages, n_new=n_new)


def _attention(qa, ka, vtb, qblk, cache_kt, cache_vt, cache_lft, page_table, k_new, v_new, lf_new,
               *, n_prompt_seq, seq_len):
    t = qa.shape[0]
    v_chunk = vtb.shape[2]
    tq = ATTN_TILE
    qt = seq_len // tq
    n_pairs = N_HEADS // 2
    n_seq, n_rows, _ = qblk.shape
    assert n_seq == n_prompt_seq * n_pairs * qt, "one sample sequence per prompt attention tile"
    n_new = k_new.shape[1]
    n_pages = page_table.shape[1]
    pt = page_table.reshape(-1)
    row_head = jnp.arange(n_rows) % N_HEADS
    e = (row_head[:, None] == jnp.arange(LANES)[None, :]).astype(BF16)
    pos = jnp.arange(PAGE_SIZE)
    suffix = (pos[:, None] >= pos[None, :]).astype(BF16)
    tri = (pos[None, :] <= pos[:, None]).astype(BF16)
    r = jnp.arange(n_pages * N_HEADS)
    later = ((r[:, None] % N_HEADS == r[None, :] % N_HEADS)
             & (r[None, :] // N_HEADS > r[:, None] // N_HEADS)).astype(BF16)

    tile_q = lambda n, hp, i, pt_ref: (n * qt + i, hp)
    seq_map = lambda n, hp, i, pt_ref: ((n * n_pairs + hp) * qt + i, 0, 0)
    const_map = lambda n, hp, i, pt_ref: (0, 0)
    any_spec = pl.BlockSpec(memory_space=pl.ANY)
    in_specs = [pl.BlockSpec((tq, 2 * LANES), tile_q),
                pl.BlockSpec((seq_len, 2 * LANES), lambda n, hp, i, pt_ref: (n, hp)),
                pl.BlockSpec((seq_len // v_chunk, 2 * V_ROWS, v_chunk), lambda n, hp, i, pt_ref: (n, hp, 0)),
                pl.BlockSpec((1, n_rows, D_ATTN), seq_map),
                pl.BlockSpec((1, n_new, D_ATTN), seq_map), pl.BlockSpec((1, n_new, D_ATTN), seq_map),
                pl.BlockSpec((1, n_new, N_HEADS), seq_map),
                pl.BlockSpec(e.shape, const_map), pl.BlockSpec(suffix.shape, const_map),
                pl.BlockSpec(later.shape, const_map), pl.BlockSpec(tri.shape, const_map),
                any_spec, any_spec, any_spec]
    grid_spec = pltpu.PrefetchScalarGridSpec(
        num_scalar_prefetch=1,
        grid=(n_prompt_seq, n_pairs, qt),
        in_specs=in_specs,
        out_specs=[pl.BlockSpec((tq, LANES), tile_q), pl.BlockSpec((1, n_rows, D_ATTN), seq_map)],
        scratch_shapes=[pltpu.VMEM((2, 1, tq), F32), pltpu.VMEM((2, V_ROWS, tq), F32),
                        pltpu.VMEM((2, n_pages, D_ATTN, PAGE_SIZE), F32),
                        pltpu.VMEM((2, n_pages, D_ATTN, PAGE_SIZE), F32),
                        pltpu.VMEM((2, n_pages, N_HEADS, PAGE_SIZE), F32),
                        pltpu.SemaphoreType.DMA((3, 2)),
                        pltpu.VMEM((PAGE_SIZE, LANES), F32),
                        pltpu.VMEM((PAGE_SIZE, D_ATTN), F32), pltpu.VMEM((PAGE_SIZE, D_ATTN), F32)],
    )
    args = [pt, qa, ka, vtb, qblk, k_new, v_new, lf_new, e, suffix, later, tri, cache_kt, cache_vt, cache_lft]
    return pl.pallas_call(
        functools.partial(_attention_kernel, tq=tq, v_chunk=v_chunk, q_chunk=LANES, n_pages=n_pages, n_new=n_new),
        grid_spec=grid_spec,
        out_shape=[jax.ShapeDtypeStruct((t, D_ATTN), BF16), jax.ShapeDtypeStruct((n_seq, n_rows, D_ATTN), F32)],
        compiler_params=pltpu.CompilerParams(dimension_semantics=("arbitrary", "arbitrary", "arbitrary"),
                                             vmem_limit_bytes=VMEM_LIMIT_BYTES),
        name="attention",
    )(*args)


def _route(logits):
    rows = logits.shape[0]
    lane = lax.broadcasted_iota(jnp.int32, (rows, LANES), 1).astype(F32)
    big = float(LANES)
    in_g = lane < N_GROUPS
    gl = jnp.where(in_g, logits, MASK_VALUE)
    m_g = jnp.max(gl, axis=1, keepdims=True)
    z_g = jnp.sum(jnp.exp(gl - m_g), axis=1, keepdims=True)
    p_g = 1.0 / z_g
    g_idx = jnp.min(jnp.where(in_g & (gl == m_g), lane, big), axis=1, keepdims=True)
    lo = N_GROUPS + EXPERTS_PER_GROUP * g_idx
    sel = (lane >= lo) & (lane < lo + EXPERTS_PER_GROUP)
    el = jnp.where(sel, logits, MASK_VALUE)
    m_e = jnp.max(el, axis=1, keepdims=True)
    ex = jnp.exp(el - m_e)
    ep = ex / jnp.sum(ex, axis=1, keepdims=True)
    v1 = jnp.max(jnp.where(sel, ep, -1.0), axis=1, keepdims=True)
    i1 = jnp.min(jnp.where(sel & (ep == v1), lane, big), axis=1, keepdims=True)
    sel2 = sel & (lane != i1)
    v2 = jnp.max(jnp.where(sel2, ep, -1.0), axis=1, keepdims=True)
    i2 = jnp.min(jnp.where(sel2 & (ep == v2), lane, big), axis=1, keepdims=True)
    denom = v1 + v2
    w1 = p_g * (v1 / denom)
    w2 = p_g * (v2 / denom)
    gate4 = jnp.where(lane == i1 - lo, w1, 0.0) + jnp.where(lane == i2 - lo, w2, 0.0)
    return gate4, g_idx


def _post_kernel(o_ref, ap_ref, sgb_ref, x_ref, wao_ref, wo_ref, g_ref, b_ref, wr_ref, tri_ref,
                 x1e_ref, route_t_ref, totals_ref, count_sc, *, alpha, d_model):
    i = pl.program_id(0)
    rows = x_ref.shape[0]
    yb = _dot(o_ref[...], wao_ref[...])
    mix = _dot((ap_ref[...] + sgb_ref[...] * yb).astype(BF16), wo_ref[...])
    x1 = _layer_norm(alpha * x_ref[...] + mix, g_ref[...], b_ref[...])
    xh = x1.astype(BF16)
    xl = (x1 - xh.astype(F32)).astype(BF16)
    hi_lo = _dot(xh, wr_ref[...])
    logits = hi_lo[:, 0:LANES] + hi_lo[:, LANES:2 * LANES] + _dot(xl, wr_ref[:, 0:LANES])
    gate4, g_idx = _route(logits)

    @pl.when(i == 0)
    def _():
        count_sc[...] = jnp.zeros((1, LANES), F32)
    lane = lax.broadcasted_iota(jnp.int32, (rows, LANES), 1).astype(F32)
    onehot = jnp.where(lane == g_idx, 1.0, 0.0)
    count = _dot(tri_ref[...], onehot.astype(BF16)) + count_sc[...]
    rank = jnp.sum(onehot * count, axis=1, keepdims=True) - 1.0
    count_sc[...] = count[rows - 1:rows, :]
    totals_ref[...] = jnp.broadcast_to(count[rows - 1:rows, :], (SUBLANES, LANES))
    route = (gate4 + jnp.where(lane == ROUTE_GROUP_LANE, g_idx, 0.0)
             + jnp.where(lane == ROUTE_RANK_LANE, rank, 0.0))
    route_t_ref[...] = route.T[0:SUBLANES, :]
    x1e_ref[:, 0:d_model] = x1
    x1e_ref[:, d_model:d_model + LANES] = route


def _post_attention(o, ap, sgb, x, w, *, alpha):
    t, d_model = x.shape
    rows = MOE_ROWS
    row_spec = lambda width: pl.BlockSpec((rows, width), lambda i: (i, 0))
    weights = [w['wao'], w['wo'], w['ln1_g'], w['ln1_b'], w['wr'], w['tri_moe']]
    return pl.pallas_call(
        functools.partial(_post_kernel, alpha=alpha, d_model=d_model),
        grid=(t // rows,),
        in_specs=[row_spec(D_ATTN), row_spec(d_model), row_spec(d_model), row_spec(d_model)]
                 + [_const_spec(a.shape) for a in weights],
        out_specs=[row_spec(d_model + LANES), pl.BlockSpec((SUBLANES, rows), lambda i: (0, i)),
                   pl.BlockSpec((SUBLANES, LANES), lambda i: (0, 0))],
        out_shape=[jax.ShapeDtypeStruct((t, d_model + LANES), F32),
                   jax.ShapeDtypeStruct((SUBLANES, t), F32),
                   jax.ShapeDtypeStruct((SUBLANES, LANES), F32)],
        scratch_shapes=[pltpu.VMEM((1, LANES), F32)],
        compiler_params=pltpu.CompilerParams(dimension_semantics=("arbitrary",),
                                             vmem_limit_bytes=VMEM_LIMIT_BYTES),
        name="post_attn",
    )(o, ap, sgb, x, *weights)


def _row_permute_kernel(pos_ref, *refs, rows, scatter):
    if scatter:
        fill_ref, src_ref, dst_ref, zero_sc, sem, fill_sem = refs

        @pl.when(pl.program_id(0) == 0)
        def _():
            zero_sc[...] = jnp.zeros_like(zero_sc)

            def clear(g):
                return pltpu.make_async_copy(zero_sc, dst_ref.at[pl.ds(fill_ref[g] * rows, rows)], fill_sem)

            def for_each_distinct(action):
                action(0)
                for g in range(1, fill_ref.shape[0]):
                    pl.when(fill_ref[g] != fill_ref[g - 1])(functools.partial(action, g))

            for_each_distinct(lambda g: clear(g).start())
            for_each_distinct(lambda g: clear(g).wait())
    else:
        src_ref, dst_ref, sem = refs
    base = pl.program_id(0) * rows

    for r in range(rows):
        p = pos_ref[base + r]
        if scatter:
            row_copy = pltpu.make_async_copy(src_ref.at[pl.ds(r, 1)], dst_ref.at[pl.ds(p, 1)], sem)
        else:
            row_copy = pltpu.make_async_copy(src_ref.at[pl.ds(p, 1)], dst_ref.at[pl.ds(r, 1)], sem)
        row_copy.start(priority=r % 2)
    if scatter:
        pltpu.make_async_copy(src_ref, dst_ref.at[pl.ds(0, rows)], sem).wait()
    else:
        pltpu.make_async_copy(src_ref.at[pl.ds(0, rows)], dst_ref, sem).wait()


def _row_permute(pos, src, *, n_out, scatter, fill_tiles=None):
    n_rows = pos.shape[0]
    width = src.shape[1]
    rows = MOE_ROWS
    any_spec = pl.BlockSpec(memory_space=pl.ANY)
    tile_spec = pl.BlockSpec((rows, width), lambda i, *_: (i, 0))
    if scatter:
        args = [pos, fill_tiles, src]
        in_specs, out_specs = [tile_spec], any_spec
        scratch = [pltpu.VMEM((rows, width), src.dtype), pltpu.SemaphoreType.DMA(()), pltpu.SemaphoreType.DMA(())]
    else:
        args = [pos, src]
        in_specs, out_specs = [any_spec], tile_spec
        scratch = [pltpu.SemaphoreType.DMA(())]
    grid_spec = pltpu.PrefetchScalarGridSpec(
        num_scalar_prefetch=len(args) - 1, grid=(n_rows // rows,), in_specs=in_specs, out_specs=out_specs,
        scratch_shapes=scratch)
    return pl.pallas_call(
        functools.partial(_row_permute_kernel, rows=rows, scatter=scatter),
        grid_spec=grid_spec,
        out_shape=jax.ShapeDtypeStruct((n_out, width), src.dtype),
        compiler_params=pltpu.CompilerParams(dimension_semantics=("arbitrary",)),
        name="row_scatter" if scatter else "row_gather",
    )(*args)


def _moe_kernel(tg_ref, nv_ref, xs_ref, w1_ref, w3_ref, w2_ref, g_ref, b_ref, y_ref, *, alpha, d_model):
    del tg_ref
    i = pl.program_id(0)

    @pl.when(i < nv_ref[0])
    def _():
        x = xs_ref[:, 0:d_model]
        route = xs_ref[:, d_model:d_model + LANES]
        xb = x.astype(BF16)
        lane = lax.broadcasted_iota(jnp.int32, route.shape, 1)
        acc = jnp.zeros_like(x)
        for e in range(EXPERTS_PER_GROUP):
            h1 = _dot(xb, w1_ref[0, e])
            h3 = _dot(xb, w3_ref[0, e])
            h = (h1 * jax.nn.sigmoid(h1)) * h3
            ye = _dot(h.astype(BF16), w2_ref[0, e])
            ge = jnp.sum(jnp.where(lane == e, route, 0.0), axis=1, keepdims=True)
            acc = acc + ge * ye
        y_ref[...] = _layer_norm(alpha * x + acc, g_ref[...], b_ref[...])

    @pl.when(i >= nv_ref[0])
    def _():
        y_ref[...] = jnp.zeros_like(y_ref)


def _experts(x1e, route_t, totals, w, *, alpha):
    t = x1e.shape[0]
    d_model = x1e.shape[1] - LANES
    _, _, _, d_exp = w['w1'].shape
    rows = MOE_ROWS
    n_tiles = t // rows + N_GROUPS
    group = route_t[ROUTE_GROUP_LANE].astype(jnp.int32)
    rank = route_t[ROUTE_RANK_LANE].astype(jnp.int32)
    count = totals[0, 0:N_GROUPS].astype(jnp.int32)
    tiles_g = (count + rows - 1) // rows
    tile_end = jnp.cumsum(tiles_g)
    pos = (tile_end - tiles_g)[group] * rows + rank
    n_valid = tile_end[N_GROUPS - 1:N_GROUPS]
    tile_group = jnp.minimum(jnp.sum(jnp.arange(n_tiles)[:, None] >= tile_end[None, :], axis=1),
                             N_GROUPS - 1).astype(jnp.int32)

    fill_tiles = jnp.concatenate([jnp.maximum(tile_end - 1, 0),
                                  jnp.minimum(n_valid + jnp.arange(N_GROUPS), n_tiles - 1)]).astype(jnp.int32)
    xs = _row_permute(pos, x1e, n_out=n_tiles * rows, scatter=True, fill_tiles=fill_tiles)

    live = lambda i, nv: jnp.minimum(i, nv[0] - 1)
    w_spec = lambda a: pl.BlockSpec((1,) + a.shape[1:], lambda i, tg, nv: (tg[live(i, nv)], 0, 0, 0))
    grid_spec = pltpu.PrefetchScalarGridSpec(
        num_scalar_prefetch=2,
        grid=(n_tiles,),
        in_specs=[
            pl.BlockSpec((rows, d_model + LANES), lambda i, tg, nv: (live(i, nv), 0)),
            w_spec(w['w1']), w_spec(w['w3']), w_spec(w['w2']),
            pl.BlockSpec((1, d_model), lambda i, tg, nv: (0, 0)),
            pl.BlockSpec((1, d_model), lambda i, tg, nv: (0, 0)),
        ],
        out_specs=pl.BlockSpec((rows, d_model), lambda i, tg, nv: (i, 0)),
    )
    ys = pl.pallas_call(
        functools.partial(_moe_kernel, alpha=alpha, d_model=d_model),
        grid_spec=grid_spec,
        out_shape=jax.ShapeDtypeStruct((n_tiles * rows, d_model), F32),
        compiler_params=pltpu.CompilerParams(dimension_semantics=("arbitrary",),
                                             vmem_limit_bytes=VMEM_LIMIT_BYTES),
        name="experts",
    )(tile_group, n_valid, xs, w['w1'], w['w3'], w['w2'], w['ln2_g'], w['ln2_b'])
    return _row_permute(pos, ys, n_out=t, scatter=False)


def _prepare_weights(w_in, b_f, conv_w, w_conv_out, w_attn_out, w_o, ln1_g, ln1_b,
                     w_group, w_router, w1, w3, w2, ln2_g, ln2_b):
    d_conv = w_conv_out.shape[0]
    o_q = 3 * d_conv
    o_f = o_q + 3 * D_ATTN
    o_g = o_f + N_HEADS
    w_bf = w_in.astype(BF16)
    wm = w_bf[:, 0:o_f]
    wf = jnp.pad(w_bf[:, o_f:o_g], ((0, 0), (0, LANES - N_HEADS)))
    wg = w_bf[:, o_g:]
    bf = jnp.pad(b_f.astype(F32), (0, LANES - N_HEADS)).reshape(1, LANES)

    col = jnp.arange(N_HEADS * LANES)
    col_head, col_lane = col // LANES, col % LANES
    part = jnp.arange(LANES)
    part_k, part_h = part // N_HEADS, part % N_HEADS
    live = (part_k[:, None] < AUG_SPLITS) & (part_h[:, None] == col_head[None, :])
    pq = live & (col_lane[None, :] == AUG_Q_C + part_k[:, None])
    pk = live & (col_lane[None, :] == AUG_Q_C + AUG_SPLITS + part_k[:, None])
    oq = ((col_lane >= AUG_Q_C + AUG_SPLITS) & (col_lane < AUG_Q_C + 2 * AUG_SPLITS)).astype(F32).reshape(1, -1)
    ok = ((col_lane >= AUG_Q_C) & (col_lane < AUG_Q_C + AUG_SPLITS)).astype(F32).reshape(1, -1)
    r = jnp.arange(INPROJ_ROWS)
    tri = (r[None, :] <= r[:, None]).astype(BF16)
    r = jnp.arange(MOE_ROWS)
    tri_moe = (r[None, :] <= r[:, None]).astype(BF16)

    wr = jnp.pad(jnp.concatenate([w_group, w_router], axis=1), ((0, 0), (0, LANES - N_GROUPS - N_EXPERTS)))
    wr_hi = wr.astype(BF16)
    wr_lo = (wr - wr_hi.astype(F32)).astype(BF16)
    wr = jnp.concatenate([wr_hi, wr_lo], axis=1)
    return dict(
        wm=wm, wf=wf, wg=wg, bf=bf, cw=conv_w.astype(F32), wco=w_conv_out.astype(BF16),
        tri=tri, pq=pq.astype(BF16), pk=pk.astype(BF16), oq=oq, ok=ok,
        wao=w_attn_out.astype(BF16), wo=w_o.astype(BF16),
        ln1_g=ln1_g.reshape(1, -1), ln1_b=ln1_b.reshape(1, -1), wr=wr,
        tri_moe=tri_moe,
        w1=w1.astype(BF16).reshape((N_GROUPS, EXPERTS_PER_GROUP) + w1.shape[1:]),
        w3=w3.astype(BF16).reshape((N_GROUPS, EXPERTS_PER_GROUP) + w3.shape[1:]),
        w2=w2.astype(BF16).reshape((N_GROUPS, EXPERTS_PER_GROUP) + w2.shape[1:]),
        ln2_g=ln2_g.reshape(1, -1), ln2_b=ln2_b.reshape(1, -1),
    )


def kernel(x_prompt, x_sample, cache_k, cache_v, cache_logf, state_conv, page_table, w_in, b_f, conv_w,
           w_conv_out, w_attn_out, w_o, ln1_g, ln1_b, w_group, w_router, w1, w3, w2, ln2_g, ln2_b):
    depth = w_in.shape[0]
    assert depth == 1, "single-layer stack"
    n_p, s_p, d_model = x_prompt.shape
    n_s, s_s, _ = x_sample.shape
    d_conv = w_conv_out.shape[1]
    alpha = (2 * depth) ** 0.25
    w = _prepare_weights(w_in[0], b_f[0], conv_w[0], w_conv_out[0], w_attn_out[0], w_o[0], ln1_g[0], ln1_b[0],
                         w_group[0], w_router[0], w1[0], w3[0], w2[0], ln2_g[0], ln2_b[0])

    xp = x_prompt.reshape(n_p * s_p, d_model)
    qa, kt_p, vt_p, ka, vtb, lft_p, ap, sgb, tail = _inproj(xp, w, sample=False, seq_len=s_p)
    xs = x_sample.reshape(n_s * s_s, d_model)
    state = state_conv[0]
    s0 = jnp.repeat(state[:, 0, :], s_s, axis=0)
    s1 = jnp.repeat(state[:, 1, :], s_s, axis=0)
    q_s, k_s, v_s, lf_s, ap_s, sgb_s, u_s = _inproj(xs, w, sample=True, seq_len=s_s, state=(s0, s1))
    q5 = q_s.reshape(n_s, s_s, N_HEADS, HEAD_DIM)
    eye_h = jnp.eye(N_HEADS, dtype=BF16)
    qblk = (q5[:, :, :, None, :] * eye_h[None, None, :, :, None]).reshape(n_s, s_s * N_HEADS, D_ATTN)
    n_pool = cache_k.shape[1]
    cache_kt = cache_k[0].transpose(0, 2, 3, 1).reshape(n_pool, D_ATTN, PAGE_SIZE)
    cache_vt = cache_v[0].transpose(0, 2, 3, 1).reshape(n_pool, D_ATTN, PAGE_SIZE)
    cache_lft = cache_logf[0].transpose(0, 2, 1)
    o_p, o_blk = _attention(
        qa, ka, vtb, qblk, cache_kt, cache_vt, cache_lft, page_table, k_s.reshape(n_s, s_s, D_ATTN),
        v_s.reshape(n_s, s_s, D_ATTN), lf_s.reshape(n_s, s_s, N_HEADS), n_prompt_seq=n_p, seq_len=s_p)
    x1e_p, route_p, totals_p = _post_attention(o_p, ap, sgb, xp, w, alpha=alpha)
    y_p = _experts(x1e_p, route_p, totals_p, w, alpha=alpha)
    o5 = o_blk.reshape(n_s, s_s, N_HEADS, N_HEADS, HEAD_DIM)
    o_s = jnp.einsum('bthgd,hg->bthd', o5, jnp.eye(N_HEADS, dtype=F32)).reshape(n_s * s_s, D_ATTN).astype(BF16)
    x1e_s, route_s, totals_s = _post_attention(o_s, ap_s, sgb_s, xs, w, alpha=alpha)
    y_s = _experts(x1e_s, route_s, totals_s, w, alpha=alpha)

    return (
        y_p.reshape(n_p, s_p, d_model),
        y_s.reshape(n_s, s_s, d_model),
        kt_p.reshape(1, n_p, N_HEADS, HEAD_DIM, s_p).transpose(0, 1, 4, 2, 3),
        vt_p.reshape(1, n_p, N_HEADS, HEAD_DIM, s_p).transpose(0, 1, 4, 2, 3),
        lft_p.reshape(1, n_p, N_HEADS, s_p).transpose(0, 1, 3, 2),
        tail[:, SUBLANES - (CONV_WIDTH - 1):, :].reshape(1, n_p, CONV_WIDTH - 1, d_conv),
        k_s.reshape(1, n_s, s_s, N_HEADS, HEAD_DIM),
        v_s.reshape(1, n_s, s_s, N_HEADS, HEAD_DIM),
        lf_s.reshape(1, n_s, s_s, N_HEADS),
        u_s.reshape(n_s, s_s, d_conv)[:, s_s - (CONV_WIDTH - 1):, :].reshape(1, n_s, CONV_WIDTH - 1, d_conv),
    )
```
